```python
import math
import jax, jax.numpy as jnp
from jax import lax
import numpy as np

D_MODEL = 2048
BATCH = 4
SEQ = 4096
DEPTH = 1
DEC_BATCH = 2
DEC_SEQ = 16384
PAST_LEN = 128

N_META = 16
BLOCK = 128
FRONT_PAD = BLOCK - N_META
N_Q_HEADS = 16
N_KV_HEADS = 4
HEAD_DIM = D_MODEL // N_Q_HEADS
Q_PER_KV = N_Q_HEADS // N_KV_HEADS
WINDOW = 128
ROPE_THETA = 10000.0
M_HEADS = 8
M_QK_DIM = D_MODEL // (2 * M_HEADS)
M_V_DIM = D_MODEL // M_HEADS
M_NORM_EPS = 1e-6
N_GROUPS = 4
EXPERTS_PER_GROUP = 8
N_EXPERTS = N_GROUPS * EXPERTS_PER_GROUP
D_EXPERT = D_MODEL // 2
TOP_K = 2
MOE_BLOCK = 256
ALPHA = (2 * DEPTH) ** 0.25
BETA = (8 * DEPTH) ** -0.25
LN_EPS = 1e-5
NEG = -1e30

A_Q = N_Q_HEADS * HEAD_DIM
A_KV = N_KV_HEADS * HEAD_DIM
M_QK = M_HEADS * M_QK_DIM
M_V = M_HEADS * M_V_DIM
M_GATES = 4 * M_HEADS
BR_GATES = 2 * D_MODEL
SPLITS = (A_Q, A_Q + A_KV, A_Q + 2 * A_KV, A_Q + 2 * A_KV + M_QK, A_Q + 2 * A_KV + 2 * M_QK, A_Q + 2 * A_KV + 2 * M_QK + M_V, A_Q + 2 * A_KV + 2 * M_QK + 2 * M_V, A_Q + 2 * A_KV + 2 * M_QK + 2 * M_V + M_GATES)
N_IN = SPLITS[-1] + BR_GATES

kernel_name = "hybrid_swa_mlstm_hmoe_encoder"


def layer_norm(x, g, b, eps=LN_EPS):
    xf = x.astype(jnp.float32)
    mu = jnp.mean(xf, axis=-1, keepdims=True)
    var = jnp.mean(jnp.square(xf - mu), axis=-1, keepdims=True)
    return ((xf - mu) * lax.rsqrt(var + eps) * g.astype(jnp.float32) + b.astype(jnp.float32)).astype(x.dtype)


def rope(x, pos):
    half = x.shape[-1] // 2
    inv = ROPE_THETA ** (-jnp.arange(half, dtype=jnp.float32) / half)
    ang = pos.astype(jnp.float32)[:, None] * inv[None, :]
    cos = jnp.cos(ang)[None, :, None, :]
    sin = jnp.sin(ang)[None, :, None, :]
    x1 = x[..., :half].astype(jnp.float32)
    x2 = x[..., half:].astype(jnp.float32)
    return jnp.concatenate([x1 * cos - x2 * sin, x2 * cos + x1 * sin], axis=-1).astype(x.dtype)


def windowed_attention(q, k, v, sink):
    B, T = q.shape[0], q.shape[1]
    nb = T // BLOCK
    qb = q.reshape(B, nb, BLOCK, N_KV_HEADS, Q_PER_KV, HEAD_DIM) * (HEAD_DIM ** -0.5)

    def band(a):
        ap = jnp.pad(a, ((0, 0), (BLOCK, BLOCK), (0, 0), (0, 0))).reshape(B, nb + 2, BLOCK, N_KV_HEADS, HEAD_DIM)
        return jnp.concatenate([ap[:, :-2], ap[:, 1:-1], ap[:, 2:]], axis=2)

    kb, vb = band(k), band(v)
    km, vm = k[:, FRONT_PAD:BLOCK], v[:, FRONT_PAD:BLOCK]
    qi = jnp.arange(nb)[:, None, None] * BLOCK + jnp.arange(BLOCK)[None, :, None]
    ki = jnp.arange(nb)[:, None, None] * BLOCK - BLOCK + jnp.arange(3 * BLOCK)[None, None, :]
    band_mask = (jnp.abs(qi - ki) <= WINDOW) & (ki >= BLOCK) & (ki < T)
    s_band = jnp.einsum('bntkgd,bnrkd->bnkgtr', qb, kb).astype(jnp.float32)
    s_band = jnp.where(band_mask[None, :, None, None], s_band, NEG)
    s_meta = jnp.einsum('bntkgd,bmkd->bnkgtm', qb, km).astype(jnp.float32)
    s_sink = jnp.broadcast_to(sink.astype(jnp.float32).reshape(1, 1, N_KV_HEADS, Q_PER_KV, 1, 1), s_meta.shape[:-1] + (1,))
    p = jax.nn.softmax(jnp.concatenate([s_meta, s_band, s_sink], axis=-1), axis=-1).astype(v.dtype)
    out = (jnp.einsum('bnkgtm,bmkd->bntkgd', p[..., :N_META], vm)
           + jnp.einsum('bnkgtr,bnrkd->bntkgd', p[..., N_META:N_META + 3 * BLOCK], vb))
    return out.reshape(B, T, A_Q)


def mlstm_chunkwise(q, k, v, log_i, log_f):
    B, H, T, dk = q.shape
    dv = v.shape[-1]
    nc = T // BLOCK
    q = q.reshape(B, H, nc, BLOCK, dk)
    k = k.reshape(B, H, nc, BLOCK, dk)
    v = v.reshape(B, H, nc, BLOCK, dv)
    li = log_i.reshape(B, H, nc, BLOCK)
    b = jnp.cumsum(log_f.reshape(B, H, nc, BLOCK), axis=-1)
    b_last = b[..., -1]
    a = b_last[..., None] - b + li
    a_max = jnp.max(a, axis=-1)
    w = jnp.exp(a - a_max[..., None])
    dC = jnp.einsum('bhcsk,bhcsv->bhckv', w[..., None] * k, v)
    dn = jnp.einsum('bhcs,bhcsk->bhck', w, k)

    def step(carry, inp):
        C, n, m = carry
        dC_c, dn_c, amax_c, bl_c = inp
        m_new = jnp.maximum(bl_c + m, amax_c)
        sp = jnp.exp(bl_c + m - m_new)
        sc = jnp.exp(amax_c - m_new)
        C_new = sp[..., None, None] * C + sc[..., None, None] * dC_c
        n_new = sp[..., None] * n + sc[..., None] * dn_c
        return (C_new, n_new, m_new), (C, n, m)

    init = (jnp.zeros((B, H, dk, dv), jnp.float32), jnp.zeros((B, H, dk), jnp.float32), jnp.zeros((B, H), jnp.float32))
    xs = (jnp.moveaxis(dC, 2, 0), jnp.moveaxis(dn, 2, 0), jnp.moveaxis(a_max, 2, 0), jnp.moveaxis(b_last, 2, 0))
    _, (C_prev, n_prev, m_prev) = lax.scan(step, init, xs)
    C_prev = jnp.moveaxis(C_prev, 0, 2)
    n_prev = jnp.moveaxis(n_prev, 0, 2)
    m_prev = jnp.moveaxis(m_prev, 0, 2)

    g = b + m_prev[..., None]
    dmat = b[..., :, None] - b[..., None, :] + li[..., None, :]
    lower = jnp.tril(jnp.ones((BLOCK, BLOCK), dtype=bool))
    dmat = jnp.where(lower, dmat, NEG)
    m = jnp.maximum(g, jnp.max(dmat, axis=-1))
    decay = jnp.exp(dmat - m[..., None])
    inter = jnp.exp(g - m)
    s = jnp.einsum('bhctk,bhcsk->bhcts', q, k) * decay
    num = inter[..., None] * jnp.einsum('bhctk,bhckv->bhctv', q, C_prev) + jnp.einsum('bhcts,bhcsv->bhctv', s, v)
    den = inter * jnp.einsum('bhctk,bhck->bhct', q, n_prev) + jnp.sum(s, axis=-1)
    h = num / jnp.maximum(jnp.abs(den), jnp.exp(-m))[..., None]
    return h.reshape(B, H, T, dv)


def mlstm_mixer(mq, mk, mv, mo, mg, gate_bias, norm_g):
    B, T = mq.shape[0], mq.shape[1]
    f32 = jnp.float32
    q = mq.reshape(B, T, M_HEADS, M_QK_DIM).transpose(0, 2, 1, 3).astype(f32) * (M_QK_DIM ** -0.5)
    k = mk.reshape(B, T, M_HEADS, M_QK_DIM).transpose(0, 2, 1, 3).astype(f32)
    v = mv.reshape(B, T, M_HEADS, M_V_DIM).transpose(0, 2, 1, 3).astype(f32)
    gates = (mg.astype(f32).reshape(B, T, 4, M_HEADS) + gate_bias.astype(f32)).transpose(2, 0, 3, 1)
    is_pad = jnp.arange(T) < FRONT_PAD
    log_i = jnp.where(is_pad, NEG, gates[0:2])
    log_f = jnp.where(is_pad, 0.0, jax.nn.log_sigmoid(gates[2:4]))
    h_fwd = mlstm_chunkwise(q, k, v, log_i[0], log_f[0])
    flip = lambda a: jnp.flip(a, axis=2)
    h_bwd = flip(mlstm_chunkwise(flip(q), flip(k), flip(v), flip(log_i[1]), flip(log_f[1])))
    h = h_fwd + h_bwd
    mu = jnp.mean(h, axis=-1, keepdims=True)
    var = jnp.mean(jnp.square(h - mu), axis=-1, keepdims=True)
    h = ((h - mu) * lax.rsqrt(var + M_NORM_EPS)).transpose(0, 2, 1, 3).reshape(B, T, M_V) * norm_g.astype(f32)
    return (h * jax.nn.sigmoid(mo.astype(f32))).astype(mo.dtype)


def hier_moe(x, w_rg, b_rg, w_re, b_re, w_g, w_u, w_d):
    N = x.shape[0]
    f32 = jnp.float32
    lg = (x @ w_rg).astype(f32) + b_rg.astype(f32)
    pg = jax.nn.softmax(lg, axis=-1)
    grp = jnp.argmax(lg, axis=-1)
    le = ((x @ w_re).astype(f32) + b_re.astype(f32)).reshape(N, N_GROUPS, EXPERTS_PER_GROUP)
    le_sel = jnp.take_along_axis(le, grp[:, None, None], axis=1)[:, 0]
    top_l, top_i = lax.top_k(le_sel, TOP_K)
    w_tok = jax.nn.softmax(top_l, axis=-1) * jnp.take_along_axis(pg, grp[:, None], axis=1)
    eid = grp[:, None] * EXPERTS_PER_GROUP + top_i
    NA = N * TOP_K
    e_flat = eid.reshape(-1)
    tok_flat = jnp.repeat(jnp.arange(N, dtype=jnp.int32), TOP_K)
    w_flat = w_tok.reshape(-1)
    order = jnp.argsort(e_flat)
    e_s, tok_s, w_s = e_flat[order], tok_flat[order], w_flat[order]
    counts = jnp.bincount(e_flat, length=N_EXPERTS)
    starts = jnp.cumsum(counts) - counts
    padded = (counts + MOE_BLOCK - 1) // MOE_BLOCK * MOE_BLOCK
    pad_ends = jnp.cumsum(padded)
    pad_starts = pad_ends - padded
    dest = pad_starts[e_s] + jnp.arange(NA) - starts[e_s]
    n_blocks = -(-NA // MOE_BLOCK) + N_EXPERTS
    R = n_blocks * MOE_BLOCK
    row_tok = jnp.zeros((R,), jnp.int32).at[dest].set(tok_s)
    row_w = jnp.zeros((R,), f32).at[dest].set(w_s)
    blk_e = jnp.minimum(jnp.searchsorted(pad_ends, jnp.arange(n_blocks) * MOE_BLOCK, side='right'), N_EXPERTS - 1)
    xr = x[row_tok].reshape(n_blocks, MOE_BLOCK, x.shape[-1])

    def expert_block(args):
        xb, e = args
        hdn = jax.nn.silu(xb @ w_g[e]) * (xb @ w_u[e])
        return hdn @ w_d[e]

    yr = lax.map(expert_block, (xr, blk_e)).reshape(R, x.shape[-1]) * row_w[:, None].astype(x.dtype)
    return jnp.zeros_like(x).at[row_tok].add(yr)


def encoder_layer(x, w_in, attn_sink, m_gate_bias, m_norm_g, w_br_attn, w_br_mlstm, w_out, ln1_g, ln1_b,
                  w_rg, b_rg, w_re, b_re, w_g, w_u, w_d, ln2_g, ln2_b):
    B, L, _ = x.shape
    T = L + FRONT_PAD
    u = jnp.pad(x, ((0, 0), (FRONT_PAD, 0), (0, 0)))
    proj = u @ w_in
    aq, ak, av, mq, mk, mv, mo, mg, bg = jnp.split(proj, SPLITS, axis=-1)
    pos = jnp.arange(T) - FRONT_PAD
    aq = rope(aq.reshape(B, T, N_Q_HEADS, HEAD_DIM), pos)
    ak = rope(ak.reshape(B, T, N_KV_HEADS, HEAD_DIM), pos)
    av = av.reshape(B, T, N_KV_HEADS, HEAD_DIM)
    attn = windowed_attention(aq, ak, av, attn_sink)[:, FRONT_PAD:]
    mem = mlstm_mixer(mq, mk, mv, mo, mg, m_gate_bias, m_norm_g)[:, FRONT_PAD:]
    ga, gb = jnp.split(jax.nn.sigmoid(bg[:, FRONT_PAD:]), 2, axis=-1)
    merged = ga * (attn @ w_br_attn) + gb * (mem @ w_br_mlstm)
    x = layer_norm(ALPHA * x + merged @ w_out, ln1_g, ln1_b)
    ff = hier_moe(x.reshape(B * L, D_MODEL), w_rg, b_rg, w_re, b_re, w_g, w_u, w_d).reshape(B, L, D_MODEL)
    return layer_norm(ALPHA * x + ff, ln2_g, ln2_b)


def encode(x, meta_tokens, ln_emb_g, ln_emb_b, layer_weights):
    B = x.shape[0]
    meta = jnp.broadcast_to(meta_tokens.astype(x.dtype)[None], (B, N_META, D_MODEL))
    h = layer_norm(jnp.concatenate([meta, x], axis=1), ln_emb_g, ln_emb_b)
    for layer in range(DEPTH):
        h = encoder_layer(h, *[w[layer] for w in layer_weights])
    return h[:, N_META:]


def setup_inputs(seed: int = 0) -> dict:
    key = jax.random.key(seed)
    ks = jax.random.split(key, 24)
    f32 = jnp.float32
    nrm = lambda k, shape, s: jax.random.normal(k, shape, f32) * s
    col_scale = jnp.ones((N_IN,), f32)
    col_scale = col_scale.at[A_Q + A_KV:A_Q + 2 * A_KV].set(BETA)
    col_scale = col_scale.at[SPLITS[4]:SPLITS[5]].set(BETA)
    f_bias = jnp.broadcast_to(jnp.linspace(3.0, 6.0, M_HEADS, dtype=f32), (DEPTH, 2, M_HEADS))
    m_gate_bias = jnp.concatenate([nrm(ks[7], (DEPTH, 2, M_HEADS), 0.1), f_bias + nrm(ks[8], (DEPTH, 2, M_HEADS), 0.1)], axis=1)
    return {
        "x_prompt": nrm(ks[0], (BATCH, SEQ, D_MODEL), 1.0),
        "x_sample": nrm(ks[1], (DEC_BATCH, DEC_SEQ, D_MODEL), 1.0),
        "meta_tokens": nrm(ks[2], (N_META, D_MODEL), 1.0),
        "ln_emb_g": 1.0 + nrm(ks[3], (D_MODEL,), 0.02),
        "ln_emb_b": nrm(ks[4], (D_MODEL,), 0.02),
        "w_in": nrm(ks[5], (DEPTH, D_MODEL, N_IN), D_MODEL ** -0.5) * col_scale,
        "attn_sink": nrm(ks[6], (DEPTH, N_Q_HEADS), 1.0),
        "m_gate_bias": m_gate_bias,
        "m_norm_g": 1.0 + nrm(ks[9], (DEPTH, M_V), 0.02),
        "w_br_attn": nrm(ks[10], (DEPTH, A_Q, D_MODEL), A_Q ** -0.5),
        "w_br_mlstm": nrm(ks[11], (DEPTH, M_V, D_MODEL), M_V ** -0.5),
        "w_out": nrm(ks[12], (DEPTH, D_MODEL, D_MODEL), BETA * D_MODEL ** -0.5),
        "ln1_g": 1.0 + nrm(ks[13], (DEPTH, D_MODEL), 0.02),
        "ln1_b": nrm(ks[14], (DEPTH, D_MODEL), 0.02),
        "w_router_group": nrm(ks[15], (DEPTH, D_MODEL, N_GROUPS), D_MODEL ** -0.5),
        "b_router_group": nrm(ks[16], (DEPTH, N_GROUPS), 0.01),
        "w_router_expert": nrm(ks[17], (DEPTH, D_MODEL, N_EXPERTS), D_MODEL ** -0.5),
        "b_router_expert": nrm(ks[18], (DEPTH, N_EXPERTS), 0.01),
        "w_expert_gate": nrm(ks[19], (DEPTH, N_EXPERTS, D_MODEL, D_EXPERT), D_MODEL ** -0.5),
        "w_expert_up": nrm(ks[20], (DEPTH, N_EXPERTS, D_MODEL, D_EXPERT), D_MODEL ** -0.5),
        "w_expert_down": nrm(ks[21], (DEPTH, N_EXPERTS, D_EXPERT, D_MODEL), BETA * D_EXPERT ** -0.5),
        "ln2_g": 1.0 + nrm(ks[22], (DEPTH, D_MODEL), 0.02),
        "ln2_b": nrm(ks[23], (DEPTH, D_MODEL), 0.02),
    }


def reference(x_prompt, x_sample, meta_tokens, ln_emb_g, ln_emb_b, w_in, attn_sink, m_gate_bias, m_norm_g,
              w_br_attn, w_br_mlstm, w_out, ln1_g, ln1_b, w_router_group, b_router_group, w_router_expert,
              b_router_expert, w_expert_gate, w_expert_up, w_expert_down, ln2_g, ln2_b):
    layer_weights = (w_in, attn_sink, m_gate_bias, m_norm_g, w_br_attn, w_br_mlstm, w_out, ln1_g, ln1_b,
                     w_router_group, b_router_group, w_router_expert, b_router_expert,
                     w_expert_gate, w_expert_up, w_expert_down, ln2_g, ln2_b)
    y_prompt = encode(x_prompt, meta_tokens, ln_emb_g, ln_emb_b, layer_weights)
    y_sample = encode(x_sample, meta_tokens, ln_emb_g, ln_emb_b, layer_weights)
    return (y_prompt, y_sample)
```

```python
import functools

import jax
import jax.numpy as jnp
from jax import lax
from jax.experimental import pallas as pl
from jax.experimental.pallas import tpu as pltpu

F32 = jnp.float32
BF16 = jnp.bfloat16
I32 = jnp.int32

D_MODEL = 2048
DEPTH = 1
N_META = 16
BLOCK = 128
FRONT_PAD = BLOCK - N_META
N_Q_HEADS = 16
N_KV_HEADS = 4
HEAD_DIM = 128
Q_PER_KV = 4
WINDOW = 128
ROPE_THETA = 10000.0
M_HEADS = 8
M_QK_DIM = 128
M_V_DIM = 256
M_NORM_EPS = 1e-6
N_GROUPS = 4
EXPERTS_PER_GROUP = 8
N_EXPERTS = 32
D_EXPERT = 1024
ALPHA = (2 * DEPTH) ** 0.25
LN_EPS = 1e-5
NEG = -1e30

C_AQ, C_MV, C_MO, C_GA, C_GB, C_MQ, C_MK, C_AK, C_AV = 0, 2048, 4096, 6144, 8192, 10240, 11264, 12288, 12800
N_MAIN = 13312
PROJ_TN = 1024
LANES = 128
SLOT_BLOCK = 256
VMEM_LIMIT = 56 * 1024 * 1024


def _pick_tile(n, pref):
    t = min(pref, n)
    t -= t % BLOCK
    while n % t:
        t -= BLOCK
    return t


def _cparams(sem, vmem=VMEM_LIMIT):
    return pltpu.CompilerParams(dimension_semantics=sem, vmem_limit_bytes=vmem)


def _ln_rows(x, g, b, eps):
    mu = jnp.mean(x, axis=-1, keepdims=True)
    xc = x - mu
    var = jnp.mean(xc * xc, axis=-1, keepdims=True)
    return xc * lax.rsqrt(var + eps) * g + b


def _rope_heads(acc, cos, sin, nheads, scale):
    outs = []
    for h in range(nheads):
        xh = acc[:, h * HEAD_DIM:(h + 1) * HEAD_DIM]
        o = xh * cos + pltpu.roll(xh, HEAD_DIM // 2, axis=1) * sin
        outs.append(o * scale)
    return outs


def _proj_kernel(x_ref, g_ref, b_ref, cos_ref, sin_ref, w_ref, wg_ref, out_ref, gate_ref, u_scr, *, zero_front):
    j = pl.program_id(1)

    @pl.when(j == 0)
    def _():
        u = _ln_rows(x_ref[...], g_ref[...], b_ref[...], LN_EPS)
        if zero_front:
            row = lax.broadcasted_iota(I32, u.shape, 0)
            u = jnp.where(row >= FRONT_PAD, u, 0.0)
        ub = u.astype(BF16)
        u_scr[...] = ub
        gate_ref[...] = jnp.dot(ub, wg_ref[...], preferred_element_type=F32)

    acc = jnp.dot(u_scr[...], w_ref[...], preferred_element_type=F32)

    @pl.when(j < C_MV // PROJ_TN)
    def _():
        heads = _rope_heads(acc, cos_ref[...], sin_ref[...], PROJ_TN // HEAD_DIM, HEAD_DIM ** -0.5)
        out_ref[...] = jnp.concatenate(heads, axis=1).astype(BF16)

    @pl.when(j == C_AK // PROJ_TN)
    def _():
        heads = _rope_heads(acc, cos_ref[...], sin_ref[...], N_KV_HEADS, 1.0)
        out_ref[...] = jnp.concatenate(heads + [acc[:, N_KV_HEADS * HEAD_DIM:]], axis=1).astype(BF16)

    @pl.when(j == C_MQ // PROJ_TN)
    def _():
        out_ref[...] = (acc * (M_QK_DIM ** -0.5)).astype(BF16)

    @pl.when((j >= C_MV // PROJ_TN) & (j != C_MQ // PROJ_TN) & (j != C_AK // PROJ_TN))
    def _():
        out_ref[...] = acc.astype(BF16)


def _proj_call(x, g, b, cos, sin, w_main, w_gate, *, tm, zero_front, name):
    rows = x.shape[0]
    pos_tiles = cos.shape[0] // tm
    grid = (rows // tm, N_MAIN // PROJ_TN)
    return pl.pallas_call(
        functools.partial(_proj_kernel, zero_front=zero_front),
        grid=grid,
        in_specs=[
            pl.BlockSpec((tm, D_MODEL), lambda i, j: (i, 0)),
            pl.BlockSpec((1, D_MODEL), lambda i, j: (0, 0)),
            pl.BlockSpec((1, D_MODEL), lambda i, j: (0, 0)),
            pl.BlockSpec((tm, HEAD_DIM), lambda i, j: (i % pos_tiles, 0)),
            pl.BlockSpec((tm, HEAD_DIM), lambda i, j: (i % pos_tiles, 0)),
            pl.BlockSpec((D_MODEL, PROJ_TN), lambda i, j: (0, j)),
            pl.BlockSpec((D_MODEL, LANES), lambda i, j: (0, 0)),
        ],
        out_specs=[
            pl.BlockSpec((tm, PROJ_TN), lambda i, j: (i, j)),
            pl.BlockSpec((tm, LANES), lambda i, j: (i, 0)),
        ],
        out_shape=[jax.ShapeDtypeStruct((rows, N_MAIN), BF16), jax.ShapeDtypeStruct((rows, LANES), F32)],
        scratch_shapes=[pltpu.VMEM((tm, D_MODEL), BF16)],
        compiler_params=_cparams(("arbitrary", "arbitrary")),
        name=name,
    )(x, g, b, cos, sin, w_main, w_gate)


def _attn_kernel(sink_ref, q_ref, kp_ref, kc_ref, kn_ref, vp_ref, vc_ref, vn_ref, km_ref, vm_ref, o_ref, *, nb):
    i = pl.program_id(1)
    prow = lax.broadcasted_iota(I32, (BLOCK, 4 * BLOCK), 0)
    col = lax.broadcasted_iota(I32, (BLOCK, 4 * BLOCK), 1)
    lo = jnp.where(i > 0, 0, BLOCK)
    hi = jnp.where(i < nb - 1, 3 * BLOCK, 2 * BLOCK)
    band = (jnp.abs(col - BLOCK - prow) <= WINDOW) & (col >= lo) & (col < hi)
    mask = band | (col >= 3 * BLOCK + FRONT_PAD)
    mask = jnp.concatenate([mask] * Q_PER_KV, axis=0)
    for kv in range(N_KV_HEADS):
        h0 = kv * Q_PER_KV
        ks = slice(kv * HEAD_DIM, (kv + 1) * HEAD_DIM)
        q = jnp.concatenate([q_ref[:, (h0 + g) * HEAD_DIM:(h0 + g + 1) * HEAD_DIM] for g in range(Q_PER_KV)], axis=0)
        k = jnp.concatenate([kp_ref[:, ks], kc_ref[:, ks], kn_ref[:, ks], km_ref[:, ks]], axis=0)
        v = jnp.concatenate([vp_ref[:, ks], vc_ref[:, ks], vn_ref[:, ks], vm_ref[:, ks]], axis=0)
        s = lax.dot_general(q, k, (((1,), (1,)), ((), ())), preferred_element_type=F32)
        s = jnp.where(mask, s, NEG)
        sink = jnp.concatenate([jnp.full((BLOCK, 1), sink_ref[h0 + g], F32) for g in range(Q_PER_KV)], axis=0)
        m = jnp.maximum(jnp.max(s, axis=1, keepdims=True), sink)
        p = jnp.exp(s - m)
        denom = jnp.sum(p, axis=1, keepdims=True) + jnp.exp(sink - m)
        o = jnp.dot(p.astype(BF16), v, preferred_element_type=F32) / denom
        for g in range(Q_PER_KV):
            o_ref[:, (h0 + g) * HEAD_DIM:(h0 + g + 1) * HEAD_DIM] = o[g * BLOCK:(g + 1) * BLOCK].astype(BF16)


def _attn_call(sink, proj, proj_meta, *, batch, nb, name):
    rows = proj.shape[0]
    kw = N_KV_HEADS * HEAD_DIM
    kcol, vcol = C_AK // kw, C_AV // kw

    def band_spec(off, colblk):
        def imap(b, i, sink_ref):
            return (b * nb + jnp.clip(i + off, 0, nb - 1), colblk)
        return pl.BlockSpec((BLOCK, kw), imap)

    grid_spec = pltpu.PrefetchScalarGridSpec(
        num_scalar_prefetch=1,
        grid=(batch, nb),
        in_specs=[
            pl.BlockSpec((BLOCK, N_Q_HEADS * HEAD_DIM), lambda b, i, s: (b * nb + i, 0)),
            band_spec(-1, kcol), band_spec(0, kcol), band_spec(1, kcol),
            band_spec(-1, vcol), band_spec(0, vcol), band_spec(1, vcol),
            pl.BlockSpec((BLOCK, kw), lambda b, i, s: (0, kcol)),
            pl.BlockSpec((BLOCK, kw), lambda b, i, s: (0, vcol)),
        ],
        out_specs=pl.BlockSpec((BLOCK, N_Q_HEADS * HEAD_DIM), lambda b, i, s: (b * nb + i, 0)),
    )
    return pl.pallas_call(
        functools.partial(_attn_kernel, nb=nb),
        grid_spec=grid_spec,
        out_shape=jax.ShapeDtypeStruct((rows, N_Q_HEADS * HEAD_DIM), BF16),
        compiler_params=_cparams(("arbitrary", "arbitrary")),
        name=name,
    )(sink, proj, proj, proj, proj, proj, proj, proj, proj_meta, proj_meta)


def _dot_exact(a, b, dims):
    return lax.dot_general(a, b, (dims, ((), ())), preferred_element_type=F32, precision=lax.Precision.HIGHEST)


def _gate_terms(g_col, g_row, tri, rev, pad_front):
    d = M_HEADS if rev else 0
    li_c = g_col[:, d:d + M_HEADS]
    lf_c = jax.nn.log_sigmoid(g_col[:, 2 * M_HEADS + d:3 * M_HEADS + d])
    li_r = g_row[d:d + M_HEADS, :]
    lf_r = jax.nn.log_sigmoid(g_row[2 * M_HEADS + d:3 * M_HEADS + d, :])
    if pad_front:
        rc = lax.broadcasted_iota(I32, li_c.shape, 0) >= FRONT_PAD
        rr = lax.broadcasted_iota(I32, li_r.shape, 1) >= FRONT_PAD
        li_c, lf_c = jnp.where(rc, li_c, NEG), jnp.where(rc, lf_c, 0.0)
        li_r, lf_r = jnp.where(rr, li_r, NEG), jnp.where(rr, lf_r, 0.0)
    b_c = _dot_exact(tri, lf_c, ((1,), (0,)))
    b_r = _dot_exact(lf_r, tri, ((1,), (1,)))
    tot_r = jnp.sum(lf_r, axis=1, keepdims=True)
    return li_c, li_r, b_c, b_r, tot_r


def _state_update(h, k, v, li_c, li_r, b_c, b_r, tot_r, c_scr, n_scr, m_scr):
    bc, br = b_c[:, h:h + 1], b_r[h:h + 1, :]
    tot = tot_r[h:h + 1, :]
    m_prev = m_scr[h:h + 1, 0:1]
    a_r = tot - br + li_r[h:h + 1, :]
    a_c = tot - bc + li_c[:, h:h + 1]
    a_max = jnp.max(a_r, axis=1, keepdims=True)
    wk = jnp.exp(a_c - a_max) * k.astype(F32)
    d_c = lax.dot_general(wk.astype(BF16), v, (((0,), (0,)), ((), ())), preferred_element_type=F32)
    d_n = jnp.sum(wk, axis=0, keepdims=True)
    m_new = jnp.maximum(tot + m_prev, a_max)
    sp = jnp.exp(tot + m_prev - m_new)
    sc = jnp.exp(a_max - m_new)
    c_scr[h] = sp * c_scr[h] + sc * d_c
    n_scr[h:h + 1, :] = sp * n_scr[h:h + 1, :] + sc * d_n
    m_scr[h:h + 1, :] = jnp.broadcast_to(m_new, (1, LANES))


def _mlstm_kernel(*refs, rev, final):
    (q_ref, k_ref, v_ref, g_ref, gt_ref, bias_c_ref, bias_r_ref, tri_ref), refs = refs[:8], refs[8:]
    if not rev:
        (km_ref, vm_ref, gm_ref, gmt_ref), refs = refs[:4], refs[4:]
    if final:
        (hf_ref, mo_ref, ng_ref), refs = refs[:3], refs[3:]
    h_ref, c_scr, n_scr, m_scr = refs
    tri = tri_ref[...]

    @pl.when(pl.program_id(1) == 0)
    def _():
        c_scr[...] = jnp.zeros_like(c_scr)
        n_scr[...] = jnp.zeros_like(n_scr)
        m_scr[...] = jnp.zeros_like(m_scr)
        if not rev:
            terms = _gate_terms(gm_ref[...] + bias_c_ref[...], gmt_ref[...] + bias_r_ref[...], tri, rev, True)
            for h in range(M_HEADS):
                _state_update(h, km_ref[:, h * M_QK_DIM:(h + 1) * M_QK_DIM], vm_ref[:, h * M_V_DIM:(h + 1) * M_V_DIM],
                              *terms, c_scr, n_scr, m_scr)

    li_c, li_r, b_c, b_r, tot_r = _gate_terms(g_ref[...] + bias_c_ref[...], gt_ref[...] + bias_r_ref[...], tri, rev, False)
    causal = tri > 0.5
    for h in range(M_HEADS):
        q = q_ref[:, h * M_QK_DIM:(h + 1) * M_QK_DIM]
        k = k_ref[:, h * M_QK_DIM:(h + 1) * M_QK_DIM]
        v = v_ref[:, h * M_V_DIM:(h + 1) * M_V_DIM]
        bc, br = b_c[:, h:h + 1], b_r[h:h + 1, :]
        m_prev = m_scr[h:h + 1, 0:1]
        dmat = jnp.where(causal, bc - br + li_r[h:h + 1, :], NEG)
        g = bc + m_prev
        m = jnp.maximum(g, jnp.max(dmat, axis=1, keepdims=True))
        decay = jnp.exp(dmat - m)
        inter = jnp.exp(g - m)
        s = lax.dot_general(q, k, (((1,), (1,)), ((), ())), preferred_element_type=F32) * decay
        num = inter * jnp.dot(q, c_scr[h].astype(BF16), preferred_element_type=F32)
        num = num + jnp.dot(s.astype(BF16), v, preferred_element_type=F32)
        qn = jnp.sum(q.astype(F32) * n_scr[h:h + 1, :], axis=1, keepdims=True)
        den = inter * qn + jnp.sum(s, axis=1, keepdims=True)
        hv = num / jnp.maximum(jnp.abs(den), jnp.exp(-m))
        cols = slice(h * M_V_DIM, (h + 1) * M_V_DIM)
        if final:
            hv = hv + hf_ref[:, cols].astype(F32)
            mu = jnp.mean(hv, axis=1, keepdims=True)
            hc = hv - mu
            var = jnp.mean(hc * hc, axis=1, keepdims=True)
            hv = hc * lax.rsqrt(var + M_NORM_EPS) * ng_ref[:, cols] * jax.nn.sigmoid(mo_ref[:, cols].astype(F32))
        h_ref[:, cols] = hv.astype(h_ref.dtype)
        _state_update(h, k, v, li_c, li_r, b_c, b_r, tot_r, c_scr, n_scr, m_scr)


def _mlstm_call(proj, gates, gates_t, bias_c, bias_r, tri, meta, final_in, *, batch, nc, rev, name):
    rows = proj.shape[0]
    final = final_in is not None
    qw, vw = M_HEADS * M_QK_DIM, M_HEADS * M_V_DIM

    def chunk(b, i):
        return b * nc + (nc - 1 - i if rev else i)

    in_specs = [
        pl.BlockSpec((BLOCK, qw), lambda b, i: (chunk(b, i), C_MQ // qw)),
        pl.BlockSpec((BLOCK, qw), lambda b, i: (chunk(b, i), C_MK // qw)),
        pl.BlockSpec((BLOCK, vw), lambda b, i: (chunk(b, i), C_MV // vw)),
        pl.BlockSpec((BLOCK, LANES), lambda b, i: (chunk(b, i), 0)),
        pl.BlockSpec((4 * M_HEADS, BLOCK), lambda b, i: (0, chunk(b, i))),
        pl.BlockSpec((1, LANES), lambda b, i: (0, 0)),
        pl.BlockSpec((4 * M_HEADS, 1), lambda b, i: (0, 0)),
        pl.BlockSpec((BLOCK, BLOCK), lambda b, i: (0, 0)),
    ]
    args = [proj, proj, proj, gates, gates_t, bias_c, bias_r, tri]
    if not rev:
        proj_m, gates_m, gates_mt = meta
        in_specs += [
            pl.BlockSpec((BLOCK, qw), lambda b, i: (0, C_MK // qw)),
            pl.BlockSpec((BLOCK, vw), lambda b, i: (0, C_MV // vw)),
            pl.BlockSpec((BLOCK, LANES), lambda b, i: (0, 0)),
            pl.BlockSpec((4 * M_HEADS, BLOCK), lambda b, i: (0, 0)),
        ]
        args += [proj_m, proj_m, gates_m, gates_mt]
    if final:
        h_fwd, norm_g = final_in
        in_specs += [
            pl.BlockSpec((BLOCK, vw), lambda b, i: (chunk(b, i), 0)),
            pl.BlockSpec((BLOCK, vw), lambda b, i: (chunk(b, i), C_MO // vw)),
            pl.BlockSpec((1, vw), lambda b, i: (0, 0)),
        ]
        args += [h_fwd, proj, norm_g]
    return pl.pallas_call(
        functools.partial(_mlstm_kernel, rev=rev, final=final),
        grid=(batch, nc),
        in_specs=in_specs,
        out_specs=pl.BlockSpec((BLOCK, vw), lambda b, i: (chunk(b, i), 0)),
        out_shape=jax.ShapeDtypeStruct((rows, vw), BF16),
        scratch_shapes=[pltpu.VMEM((M_HEADS, M_QK_DIM, M_V_DIM), F32), pltpu.VMEM((M_HEADS, LANES), F32),
                        pltpu.VMEM((M_HEADS, LANES), F32)],
        compiler_params=_cparams(("arbitrary", "arbitrary")),
        name=name,
    )(*args)


def _merge_kernel(a_ref, m_ref, ga_ref, gb_ref, wa_ref, wb_ref, o_ref):
    ya = jnp.dot(a_ref[...], wa_ref[...], preferred_element_type=F32)
    yb = jnp.dot(m_ref[...], wb_ref[...], preferred_element_type=F32)
    o_ref[...] = (jax.nn.sigmoid(ga_ref[...].astype(F32)) * ya + jax.nn.sigmoid(gb_ref[...].astype(F32)) * yb).astype(BF16)


def _merge_call(attn, mem, proj, wa, wb, *, tm, name):
    rows = attn.shape[0]
    row_spec = pl.BlockSpec((tm, D_MODEL), lambda i: (i, 0))
    w_spec = pl.BlockSpec((D_MODEL, D_MODEL), lambda i: (0, 0), pipeline_mode=pl.Buffered(1))
    return pl.pallas_call(
        _merge_kernel,
        grid=(rows // tm,),
        in_specs=[row_spec, row_spec,
                  pl.BlockSpec((tm, D_MODEL), lambda i: (i, C_GA // D_MODEL)),
                  pl.BlockSpec((tm, D_MODEL), lambda i: (i, C_GB // D_MODEL)),
                  w_spec, w_spec],
        out_specs=row_spec,
        out_shape=jax.ShapeDtypeStruct((rows, D_MODEL), BF16),
        compiler_params=_cparams(("arbitrary",)),
        name=name,
    )(attn, mem, proj, proj, wa, wb)


def _post_kernel(mg_ref, x_ref, eg_ref, eb_ref, wo_ref, g1_ref, b1_ref, wr_hi_ref, wr_lo_ref, rb_ref, low_ref, cnt_in_ref,
                 x1_ref, ri_ref, rw_ref, cnt_ref):
    i = pl.program_id(0)
    x0 = _ln_rows(x_ref[...], eg_ref[...], eb_ref[...], LN_EPS)
    y = jnp.dot(mg_ref[...], wo_ref[...], preferred_element_type=F32)
    x1 = _ln_rows(ALPHA * x0 + y, g1_ref[...], b1_ref[...], LN_EPS)
    x1_ref[...] = x1

    hi = x1.astype(BF16)
    lo = (x1 - hi.astype(F32)).astype(BF16)
    logits = (jnp.dot(hi, wr_hi_ref[...], preferred_element_type=F32)
              + jnp.dot(lo, wr_hi_ref[...], preferred_element_type=F32)
              + jnp.dot(hi, wr_lo_ref[...], preferred_element_type=F32)) + rb_ref[...]
    lane = lax.broadcasted_iota(I32, logits.shape, 1)
    big = jnp.int32(LANES)

    def first_max(valid):
        vmax = jnp.max(jnp.where(valid, logits, -jnp.inf), axis=1, keepdims=True)
        idx = jnp.min(jnp.where(valid & (logits == vmax), lane, big), axis=1, keepdims=True)
        return vmax, idx

    is_grp = lane < N_GROUPS
    gmax, grp = first_max(is_grp)
    p_grp = 1.0 / jnp.sum(jnp.where(is_grp, jnp.exp(logits - gmax), 0.0), axis=1, keepdims=True)
    e_lo = N_GROUPS + grp * EXPERTS_PER_GROUP
    in_grp = (lane >= e_lo) & (lane < e_lo + EXPERTS_PER_GROUP)
    l1, i1 = first_max(in_grp)
    l2, i2 = first_max(in_grp & (lane != i1))
    t2 = jnp.exp(l2 - l1)
    w1 = p_grp / (1.0 + t2)
    w2 = p_grp * t2 / (1.0 + t2)
    e1 = i1 - N_GROUPS
    e2 = i2 - N_GROUPS

    @pl.when(i == 0)
    def _():
        cnt_ref[...] = cnt_in_ref[...]

    oh1 = (lane == e1).astype(F32)
    oh2 = (lane == e2).astype(F32)
    cnt = oh1 + oh2
    before = jnp.dot(low_ref[...], cnt.astype(BF16), preferred_element_type=F32) + cnt_ref[...]
    r1 = jnp.sum(oh1 * before, axis=1, keepdims=True).astype(I32)
    r2 = jnp.sum(oh2 * before, axis=1, keepdims=True).astype(I32)
    cnt_ref[...] = cnt_ref[...] + jnp.sum(cnt, axis=0, keepdims=True)

    ri_ref[...] = jnp.where(lane == 0, e1, jnp.where(lane == 1, e2, jnp.where(lane == 2, r1, jnp.where(lane == 3, r2, 0))))
    rw_ref[...] = jnp.where(lane == 0, w1, jnp.where(lane == 1, w2, 0.0))


def _post_call(merged, x, eg, eb, wo, g1, b1, wr_hi, wr_lo, rb, low, cnt_in, *, tm, name):
    rows = merged.shape[0]
    row_spec = pl.BlockSpec((tm, D_MODEL), lambda i: (i, 0))
    vec_spec = pl.BlockSpec((1, D_MODEL), lambda i: (0, 0))
    lane_spec = pl.BlockSpec((1, LANES), lambda i: (0, 0))
    rt_spec = pl.BlockSpec((D_MODEL, LANES), lambda i: (0, 0))
    return pl.pallas_call(
        _post_kernel,
        grid=(rows // tm,),
        in_specs=[row_spec, row_spec, vec_spec, vec_spec,
                  pl.BlockSpec((D_MODEL, D_MODEL), lambda i: (0, 0), pipeline_mode=pl.Buffered(1)),
                  vec_spec, vec_spec, rt_spec, rt_spec, lane_spec,
                  pl.BlockSpec((tm, tm), lambda i: (0, 0)), lane_spec],
        out_specs=[row_spec, pl.BlockSpec((tm, LANES), lambda i: (i, 0)), pl.BlockSpec((tm, LANES), lambda i: (i, 0)),
                   lane_spec],
        out_shape=[jax.ShapeDtypeStruct((rows, D_MODEL), F32), jax.ShapeDtypeStruct((rows, LANES), I32),
                   jax.ShapeDtypeStruct((rows, LANES), F32), jax.ShapeDtypeStruct((1, LANES), F32)],
        compiler_params=_cparams(("arbitrary",)),
        name=name,
    )(merged, x, eg, eb, wo, g1, b1, wr_hi, wr_lo, rb, low, cnt_in)


def _row_copy(src, s, dst, d, sem):
    return pltpu.make_async_copy(src.at[pl.ds(s, 1), :], dst.at[pl.ds(d, 1), :], sem)


def _dispatch_kernel(dest_ref, x_ref, xr_in_ref, xr_ref, sem, *, tile):
    del xr_in_ref
    base = pl.program_id(0) * tile

    def body(r, carry):
        _row_copy(x_ref, base + r, xr_ref, dest_ref[0, 0, 2 * r], sem).start()
        _row_copy(x_ref, base + r, xr_ref, dest_ref[0, 0, 2 * r + 1], sem).start()
        return carry

    lax.fori_loop(0, tile, body, 0)
    for _ in range(2):
        pltpu.make_async_copy(x_ref.at[pl.ds(0, tile), :], xr_ref.at[pl.ds(0, tile), :], sem).wait()


def _dispatch_call(dest, x1, xr, *, tile, name):
    rows = x1.shape[0]
    return pl.pallas_call(
        functools.partial(_dispatch_kernel, tile=tile),
        grid=(rows // tile,),
        in_specs=[pl.BlockSpec((1, 1, 2 * tile), lambda i: (i, 0, 0), memory_space=pltpu.SMEM),
                  pl.BlockSpec(memory_space=pl.ANY), pl.BlockSpec(memory_space=pl.ANY)],
        out_specs=pl.BlockSpec(memory_space=pl.ANY),
        out_shape=jax.ShapeDtypeStruct(xr.shape, xr.dtype),
        scratch_shapes=[pltpu.SemaphoreType.DMA(())],
        input_output_aliases={2: 0},
        compiler_params=pltpu.CompilerParams(dimension_semantics=("arbitrary",), has_side_effects=True),
        name=name,
    )(dest.reshape(rows // tile, 1, 2 * tile), x1, xr)


def _expert_kernel(blk_e_ref, n_act_ref, x_ref, wg_ref, wu_ref, wd_ref, y_ref):
    del blk_e_ref
    active = pl.program_id(0) < n_act_ref[0]

    @pl.when(active)
    def _():
        xb = x_ref[...].astype(BF16)
        gate = jnp.dot(xb, wg_ref[0], preferred_element_type=F32)
        up = jnp.dot(xb, wu_ref[0], preferred_element_type=F32)
        hdn = (gate * jax.nn.sigmoid(gate) * up).astype(BF16)
        y_ref[...] = jnp.dot(hdn, wd_ref[0], preferred_element_type=F32)

    @pl.when(jnp.logical_not(active))
    def _():
        y_ref[...] = jnp.zeros_like(y_ref)


def _expert_call(blk_e, n_act, xr, wg, wu, wd, *, name):
    slots = xr.shape[0]
    n_blocks = slots // SLOT_BLOCK

    def x_map(i, be, na):
        return (jnp.minimum(i, jnp.maximum(na[0] - 1, 0)), 0)

    grid_spec = pltpu.PrefetchScalarGridSpec(
        num_scalar_prefetch=2,
        grid=(n_blocks,),
        in_specs=[
            pl.BlockSpec((SLOT_BLOCK, D_MODEL), x_map),
            pl.BlockSpec((1, D_MODEL, D_EXPERT), lambda i, be, na: (be[i], 0, 0)),
            pl.BlockSpec((1, D_MODEL, D_EXPERT), lambda i, be, na: (be[i], 0, 0)),
            pl.BlockSpec((1, D_EXPERT, D_MODEL), lambda i, be, na: (be[i], 0, 0)),
        ],
        out_specs=pl.BlockSpec((SLOT_BLOCK, D_MODEL), lambda i, be, na: (i, 0)),
    )
    return pl.pallas_call(
        _expert_kernel,
        grid_spec=grid_spec,
        out_shape=jax.ShapeDtypeStruct((slots, D_MODEL), F32),
        compiler_params=_cparams(("arbitrary",)),
        name=name,
    )(blk_e, n_act, xr, wg, wu, wd)


def _combine_kernel(dest_ref, x1_ref, rw_ref, g2_ref, b2_ref, yr_ref, o_ref, ybuf, sem, *, tile):
    def body(r, carry):
        _row_copy(yr_ref, dest_ref[0, 0, 2 * r], ybuf.at[0], r, sem).start()
        _row_copy(yr_ref, dest_ref[0, 0, 2 * r + 1], ybuf.at[1], r, sem).start()
        return carry

    lax.fori_loop(0, tile, body, 0)
    pltpu.make_async_copy(yr_ref.at[pl.ds(0, tile), :], ybuf.at[0], sem).wait()
    pltpu.make_async_copy(yr_ref.at[pl.ds(0, tile), :], ybuf.at[1], sem).wait()
    rw = rw_ref[...]
    ff = rw[:, 0:1] * ybuf[0] + rw[:, 1:2] * ybuf[1]
    o_ref[...] = _ln_rows(ALPHA * x1_ref[...] + ff, g2_ref[...], b2_ref[...], LN_EPS)


def _combine_call(dest, x1, rw, g2, b2, yr, *, tile, name):
    rows = x1.shape[0]
    row_spec = pl.BlockSpec((tile, D_MODEL), lambda i: (i, 0))
    vec_spec = pl.BlockSpec((1, D_MODEL), lambda i: (0, 0))
    return pl.pallas_call(
        functools.partial(_combine_kernel, tile=tile),
        grid=(rows // tile,),
        in_specs=[pl.BlockSpec((1, 1, 2 * tile), lambda i: (i, 0, 0), memory_space=pltpu.SMEM),
                  row_spec, pl.BlockSpec((tile, LANES), lambda i: (i, 0)), vec_spec, vec_spec,
                  pl.BlockSpec(memory_space=pl.ANY)],
        out_specs=row_spec,
        out_shape=jax.ShapeDtypeStruct((rows, D_MODEL), F32),
        scratch_shapes=[pltpu.VMEM((2, tile, D_MODEL), F32), pltpu.SemaphoreType.DMA(())],
        compiler_params=_cparams(("arbitrary",)),
        name=name,
    )(dest.reshape(rows // tile, 1, 2 * tile), x1, rw, g2, b2, yr)


def _rope_tables(pos):
    half = HEAD_DIM // 2
    inv = ROPE_THETA ** (-jnp.arange(half, dtype=F32) / half)
    ang = pos.astype(F32)[:, None] * inv[None, :]
    cos, sin = jnp.cos(ang), jnp.sin(ang)
    return jnp.concatenate([cos, cos], axis=1), jnp.concatenate([-sin, sin], axis=1)


def kernel(x_prompt, x_sample, meta_tokens, ln_emb_g, ln_emb_b, w_in, attn_sink, m_gate_bias, m_norm_g, w_br_attn, w_br_mlstm, w_out, ln1_g, ln1_b, w_router_group, b_router_group, w_router_expert, b_router_expert, w_expert_gate, w_expert_up, w_expert_down, ln2_g, ln2_b):
    assert w_in.shape[0] == DEPTH == 1
    row = lambda v: v.reshape(1, -1).astype(F32)
    eg, eb = row(ln_emb_g), row(ln_emb_b)
    g1, b1, g2, b2 = row(ln1_g[0]), row(ln1_b[0]), row(ln2_g[0]), row(ln2_b[0])

    w = w_in[0]
    sec = {}
    off = 0
    for nm, width in (("aq", 2048), ("ak", 512), ("av", 512), ("mq", 1024), ("mk", 1024), ("mv", 2048), ("mo", 2048),
                      ("mg", 32), ("bg", 4096)):
        sec[nm] = w[:, off:off + width]
        off += width
    w_main = jnp.concatenate([sec[n] for n in ("aq", "mv", "mo", "bg", "mq", "mk", "ak", "av")], axis=1).astype(BF16)
    w_gate = jnp.pad(sec["mg"], ((0, 0), (0, LANES - 4 * M_HEADS))).astype(BF16)
    wa, wb, wo = w_br_attn[0].astype(BF16), w_br_mlstm[0].astype(BF16), w_out[0].astype(BF16)
    wr = jnp.pad(jnp.concatenate([w_router_group[0], w_router_expert[0]], axis=1),
                 ((0, 0), (0, LANES - N_GROUPS - N_EXPERTS)))
    wr_hi = wr.astype(BF16)
    wr_lo = (wr - wr_hi.astype(F32)).astype(BF16)
    rb = jnp.pad(jnp.concatenate([b_router_group[0], b_router_expert[0]]), (0, LANES - N_GROUPS - N_EXPERTS)).reshape(1, LANES)
    weg, weu, wed = w_expert_gate[0].astype(BF16), w_expert_up[0].astype(BF16), w_expert_down[0].astype(BF16)
    bias_c = jnp.pad(m_gate_bias[0].reshape(1, 4 * M_HEADS), ((0, 0), (0, LANES - 4 * M_HEADS))).astype(F32)
    bias_r = m_gate_bias[0].reshape(4 * M_HEADS, 1).astype(F32)
    norm_g = row(m_norm_g[0])
    sink = attn_sink[0].astype(F32)
    t_idx = jnp.arange(BLOCK)
    tri_f = (t_idx[None, :] <= t_idx[:, None]).astype(F32)
    tri_b = (t_idx[None, :] >= t_idx[:, None]).astype(F32)

    xm = jnp.pad(meta_tokens.astype(F32), ((FRONT_PAD, 0), (0, 0)))
    cos_m, sin_m = _rope_tables(jnp.arange(BLOCK) - FRONT_PAD)
    proj_m, gates_m = _proj_call(xm, eg, eb, cos_m, sin_m, w_main, w_gate, tm=BLOCK, zero_front=True, name="proj_meta")
    gates_mt = gates_m[:, :4 * M_HEADS].T

    groups = []
    cnt = jnp.zeros((1, LANES), F32)
    for gi, xg in enumerate((x_prompt, x_sample)):
        batch, seq, _ = xg.shape
        assert seq % BLOCK == 0
        nb = seq // BLOCK
        rows = batch * seq
        x2 = xg.reshape(rows, D_MODEL)
        tm = _pick_tile(seq, 1024)
        cos, sin = _rope_tables(jnp.arange(seq) + N_META)
        proj, gates = _proj_call(x2, eg, eb, cos, sin, w_main, w_gate, tm=tm, zero_front=False, name=f"proj_g{gi}")
        gates_t = gates[:, :4 * M_HEADS].T
        attn = _attn_call(sink, proj, proj_m, batch=batch, nb=nb, name=f"attn_g{gi}")
        h_fwd = _mlstm_call(proj, gates, gates_t, bias_c, bias_r, tri_f, (proj_m, gates_m, gates_mt), None,
                            batch=batch, nc=nb, rev=False, name=f"mlstm_fwd_g{gi}")
        mem = _mlstm_call(proj, gates, gates_t, bias_c, bias_r, tri_b, None, (h_fwd, norm_g),
                          batch=batch, nc=nb, rev=True, name=f"mlstm_bwd_g{gi}")
        tp = _pick_tile(rows, 512)
        merged = _merge_call(attn, mem, proj, wa, wb, tm=tp, name=f"merge_g{gi}")
        low = (jnp.arange(tp)[None, :] < jnp.arange(tp)[:, None]).astype(BF16)
        x1, ri, rw, cnt = _post_call(merged, x2, eg, eb, wo, g1, b1, wr_hi, wr_lo, rb, low, cnt, tm=tp, name=f"post_g{gi}")
        groups.append((xg.shape, x1, ri, rw))

    counts = cnt[0, :N_EXPERTS].astype(I32)
    padded = (counts + SLOT_BLOCK - 1) // SLOT_BLOCK * SLOT_BLOCK
    pad_ends = jnp.cumsum(padded)
    pad_starts = pad_ends - padded
    total_rows = sum(g[1].shape[0] for g in groups)
    n_blocks = -(-2 * total_rows // SLOT_BLOCK) + N_EXPERTS
    blk_e = jnp.minimum(jnp.searchsorted(pad_ends, jnp.arange(n_blocks) * SLOT_BLOCK, side="right"), N_EXPERTS - 1).astype(I32)
    n_act = (pad_ends[-1:] // SLOT_BLOCK).astype(I32)

    xr = jnp.zeros((n_blocks * SLOT_BLOCK, D_MODEL), F32)
    dests = []
    for gi, (_, x1, ri, _) in enumerate(groups):
        dest = pad_starts[ri[:, 0:2]] + ri[:, 2:4]
        dests.append(dest)
        xr = _dispatch_call(dest, x1, xr, tile=_pick_tile(x1.shape[0], 1024), name=f"dispatch_g{gi}")
    yr = _expert_call(blk_e, n_act, xr, weg, weu, wed, name="experts")
    outs = []
    for gi, (shape, x1, _, rw) in enumerate(groups):
        out = _combine_call(dests[gi], x1, rw, g2, b2, yr, tile=_pick_tile(x1.shape[0], 256), name=f"combine_g{gi}")
        outs.append(out.reshape(shape))
    return tuple(outs)
```

```python
import functools

import jax
import jax.numpy as jnp
from jax import lax
from jax.experimental import pallas as pl
from jax.experimental.pallas import tpu as pltpu

F32 = jnp.float32
BF16 = jnp.bfloat16
I32 = jnp.int32

D_MODEL = 2048
DEPTH = 1
N_META = 16
BLOCK = 128
FRONT_PAD = BLOCK - N_META
N_Q_HEADS = 16
N_KV_HEADS = 4
HEAD_DIM = 128
Q_PER_KV = 4
WINDOW = 128
ROPE_THETA = 10000.0
M_HEADS = 8
M_QK_DIM = 128
M_V_DIM = 256
M_NORM_EPS = 1e-6
N_GROUPS = 4
EXPERTS_PER_GROUP = 8
N_EXPERTS = 32
D_EXPERT = 1024
ALPHA = (2 * DEPTH) ** 0.25
LN_EPS = 1e-5
NEG = -1e30

C_AQ, C_MV, C_MO, C_GA, C_GB, C_MQ, C_MK, C_AK, C_AV = 0, 2048, 4096, 6144, 8192, 10240, 11264, 12288, 12800
N_MAIN = 13312
PROJ_TN = 1024
LANES = 128
SLOT_BLOCK = 256
VMEM_LIMIT = 56 * 1024 * 1024


def _pick_tile(n, pref):
    t = min(pref, n)
    t -= t % BLOCK
    while n % t:
        t -= BLOCK
    return t


def _cparams(sem, vmem=VMEM_LIMIT):
    return pltpu.CompilerParams(dimension_semantics=sem, vmem_limit_bytes=vmem)


def _ln_rows(x, g, b, eps):
    mu = jnp.mean(x, axis=-1, keepdims=True)
    xc = x - mu
    var = jnp.mean(xc * xc, axis=-1, keepdims=True)
    return xc * lax.rsqrt(var + eps) * g + b


def _rope_heads(acc, cos, sin, nheads, scale):
    outs = []
    for h in range(nheads):
        xh = acc[:, h * HEAD_DIM:(h + 1) * HEAD_DIM]
        o = xh * cos + pltpu.roll(xh, HEAD_DIM // 2, axis=1) * sin
        outs.append(o * scale)
    return outs


def _proj_kernel(x_ref, g_ref, b_ref, cos_ref, sin_ref, w_ref, wg_ref, out_ref, gate_ref, u_scr, *, zero_front):
    j = pl.program_id(1)

    @pl.when(j == 0)
    def _():
        u = _ln_rows(x_ref[...], g_ref[...], b_ref[...], LN_EPS)
        if zero_front:
            row = lax.broadcasted_iota(I32, u.shape, 0)
            u = jnp.where(row >= FRONT_PAD, u, 0.0)
        ub = u.astype(BF16)
        u_scr[...] = ub
        gate_ref[...] = jnp.dot(ub, wg_ref[...], preferred_element_type=F32)

    acc = jnp.dot(u_scr[...], w_ref[...], preferred_element_type=F32)

    @pl.when(j < C_MV // PROJ_TN)
    def _():
        heads = _rope_heads(acc, cos_ref[...], sin_ref[...], PROJ_TN // HEAD_DIM, HEAD_DIM ** -0.5)
        out_ref[...] = jnp.concatenate(heads, axis=1).astype(BF16)

    @pl.when(j == C_AK // PROJ_TN)
    def _():
        heads = _rope_heads(acc, cos_ref[...], sin_ref[...], N_KV_HEADS, 1.0)
        out_ref[...] = jnp.concatenate(heads + [acc[:, N_KV_HEADS * HEAD_DIM:]], axis=1).astype(BF16)

    @pl.when(j == C_MQ // PROJ_TN)
    def _():
        out_ref[...] = (acc * (M_QK_DIM ** -0.5)).astype(BF16)

    @pl.when((j >= C_MV // PROJ_TN) & (j != C_MQ // PROJ_TN) & (j != C_AK // PROJ_TN))
    def _():
        out_ref[...] = acc.astype(BF16)


def _proj_call(x, g, b, cos, sin, w_main, w_gate, *, tm, zero_front, name):
    rows = x.shape[0]
    pos_tiles = cos.shape[0] // tm
    grid = (rows // tm, N_MAIN // PROJ_TN)
    return pl.pallas_call(
        functools.partial(_proj_kernel, zero_front=zero_front),
        grid=grid,
        in_specs=[
            pl.BlockSpec((tm, D_MODEL), lambda i, j: (i, 0)),
            pl.BlockSpec((1, D_MODEL), lambda i, j: (0, 0)),
            pl.BlockSpec((1, D_MODEL), lambda i, j: (0, 0)),
            pl.BlockSpec((tm, HEAD_DIM), lambda i, j: (i % pos_tiles, 0)),
            pl.BlockSpec((tm, HEAD_DIM), lambda i, j: (i % pos_tiles, 0)),
            pl.BlockSpec((D_MODEL, PROJ_TN), lambda i, j: (0, j)),
            pl.BlockSpec((D_MODEL, LANES), lambda i, j: (0, 0)),
        ],
        out_specs=[
            pl.BlockSpec((tm, PROJ_TN), lambda i, j: (i, j)),
            pl.BlockSpec((tm, LANES), lambda i, j: (i, 0)),
        ],
        out_shape=[jax.ShapeDtypeStruct((rows, N_MAIN), BF16), jax.ShapeDtypeStruct((rows, LANES), F32)],
        scratch_shapes=[pltpu.VMEM((tm, D_MODEL), BF16)],
        compiler_params=_cparams(("arbitrary", "arbitrary")),
        name=name,
    )(x, g, b, cos, sin, w_main, w_gate)


ATTN_ROWS = 32


def _attn_kernel(sink_ref, q_ref, kp_ref, kc_ref, kn_ref, vp_ref, vc_ref, vn_ref, km_ref, vm_ref, o_ref,
                 bias_scr, s_scr, p_scr, inv_scr, *, nb):
    i = pl.program_id(1)
    prow = lax.broadcasted_iota(I32, (BLOCK, 4 * BLOCK), 0)
    col = lax.broadcasted_iota(I32, (BLOCK, 4 * BLOCK), 1)
    lo = jnp.where(i > 0, 0, BLOCK)
    hi = jnp.where(i < nb - 1, 3 * BLOCK, 2 * BLOCK)
    band = (jnp.abs(col - BLOCK - prow) <= WINDOW) & (col >= lo) & (col < hi)
    bias_scr[...] = jnp.where(band | (col >= 3 * BLOCK + FRONT_PAD), 0.0, NEG)
    for kv in range(N_KV_HEADS):
        h0 = kv * Q_PER_KV
        ks = slice(kv * HEAD_DIM, (kv + 1) * HEAD_DIM)
        q = jnp.concatenate([q_ref[:, (h0 + g) * HEAD_DIM:(h0 + g + 1) * HEAD_DIM] for g in range(Q_PER_KV)], axis=0)
        k = jnp.concatenate([kp_ref[:, ks], kc_ref[:, ks], kn_ref[:, ks], km_ref[:, ks]], axis=0)
        v = jnp.concatenate([vp_ref[:, ks], vc_ref[:, ks], vn_ref[:, ks], vm_ref[:, ks]], axis=0)
        s_scr[...] = lax.dot_general(q, k, (((1,), (1,)), ((), ())), preferred_element_type=F32)
        for c in range(Q_PER_KV * BLOCK // ATTN_ROWS):
            rows = slice(c * ATTN_ROWS, (c + 1) * ATTN_ROWS)
            p0 = (c * ATTN_ROWS) % BLOCK
            sink = sink_ref[h0 + (c * ATTN_ROWS) // BLOCK]
            s = s_scr[rows, :] + bias_scr[p0:p0 + ATTN_ROWS, :]
            m = jnp.maximum(jnp.max(s, axis=1, keepdims=True), sink)
            p = jnp.exp(s - m)
            inv_scr[rows, :] = 1.0 / (jnp.sum(p, axis=1, keepdims=True) + jnp.exp(sink - m))
            p_scr[rows, :] = p.astype(BF16)
        o = jnp.dot(p_scr[...], v, preferred_element_type=F32) * inv_scr[...]
        for g in range(Q_PER_KV):
            o_ref[:, (h0 + g) * HEAD_DIM:(h0 + g + 1) * HEAD_DIM] = o[g * BLOCK:(g + 1) * BLOCK].astype(BF16)


def _attn_call(sink, proj, proj_meta, *, batch, nb, name):
    rows = proj.shape[0]
    kw = N_KV_HEADS * HEAD_DIM
    kcol, vcol = C_AK // kw, C_AV // kw

    def band_spec(off, colblk):
        def imap(b, i, sink_ref):
            return (b * nb + jnp.clip(i + off, 0, nb - 1), colblk)
        return pl.BlockSpec((BLOCK, kw), imap)

    grid_spec = pltpu.PrefetchScalarGridSpec(
        num_scalar_prefetch=1,
        grid=(batch, nb),
        in_specs=[
            pl.BlockSpec((BLOCK, N_Q_HEADS * HEAD_DIM), lambda b, i, s: (b * nb + i, 0)),
            band_spec(-1, kcol), band_spec(0, kcol), band_spec(1, kcol),
            band_spec(-1, vcol), band_spec(0, vcol), band_spec(1, vcol),
            pl.BlockSpec((BLOCK, kw), lambda b, i, s: (0, kcol)),
            pl.BlockSpec((BLOCK, kw), lambda b, i, s: (0, vcol)),
        ],
        out_specs=pl.BlockSpec((BLOCK, N_Q_HEADS * HEAD_DIM), lambda b, i, s: (b * nb + i, 0)),
        scratch_shapes=[pltpu.VMEM((BLOCK, 4 * BLOCK), F32), pltpu.VMEM((Q_PER_KV * BLOCK, 4 * BLOCK), F32),
                        pltpu.VMEM((Q_PER_KV * BLOCK, 4 * BLOCK), BF16), pltpu.VMEM((Q_PER_KV * BLOCK, 1), F32)],
    )
    return pl.pallas_call(
        functools.partial(_attn_kernel, nb=nb),
        grid_spec=grid_spec,
        out_shape=jax.ShapeDtypeStruct((rows, N_Q_HEADS * HEAD_DIM), BF16),
        compiler_params=_cparams(("arbitrary", "arbitrary")),
        name=name,
    )(sink, proj, proj, proj, proj, proj, proj, proj, proj_meta, proj_meta)


def _dot_exact(a, b, dims):
    return lax.dot_general(a, b, (dims, ((), ())), preferred_element_type=F32, precision=lax.Precision.HIGHEST)


def _gate_terms(g_col, g_row, tri, rev, pad_front):
    d = M_HEADS if rev else 0
    li_c = g_col[:, d:d + M_HEADS]
    lf_c = jax.nn.log_sigmoid(g_col[:, 2 * M_HEADS + d:3 * M_HEADS + d])
    li_r = g_row[d:d + M_HEADS, :]
    lf_r = jax.nn.log_sigmoid(g_row[2 * M_HEADS + d:3 * M_HEADS + d, :])
    if pad_front:
        rc = lax.broadcasted_iota(I32, li_c.shape, 0) >= FRONT_PAD
        rr = lax.broadcasted_iota(I32, li_r.shape, 1) >= FRONT_PAD
        li_c, lf_c = jnp.where(rc, li_c, NEG), jnp.where(rc, lf_c, 0.0)
        li_r, lf_r = jnp.where(rr, li_r, NEG), jnp.where(rr, lf_r, 0.0)
    b_c = _dot_exact(tri, lf_c, ((1,), (0,)))
    b_r = _dot_exact(lf_r, tri, ((1,), (1,)))
    tot_r = jnp.sum(lf_r, axis=1, keepdims=True)
    return li_c, li_r, b_c, b_r, tot_r


def _state_update(h, k, v, li_c, li_r, b_c, b_r, tot_r, c_scr, n_scr, m_scr):
    bc, br = b_c[:, h:h + 1], b_r[h:h + 1, :]
    tot = tot_r[h:h + 1, :]
    m_prev = m_scr[h:h + 1, 0:1]
    a_r = tot - br + li_r[h:h + 1, :]
    a_c = tot - bc + li_c[:, h:h + 1]
    a_max = jnp.max(a_r, axis=1, keepdims=True)
    m_new = jnp.maximum(tot + m_prev, a_max)
    wk = jnp.exp(a_c - m_new) * k.astype(F32)
    d_c = lax.dot_general(wk.astype(BF16), v, (((0,), (0,)), ((), ())), preferred_element_type=F32)
    d_n = jnp.sum(wk, axis=0, keepdims=True)
    sp = jnp.exp(tot + m_prev - m_new)
    c_scr[h] = sp * c_scr[h] + d_c
    n_scr[h:h + 1, :] = sp * n_scr[h:h + 1, :] + d_n
    m_scr[h:h + 1, :] = jnp.broadcast_to(m_new, (1, LANES))


def _mlstm_kernel(*refs, rev, final):
    (q_ref, k_ref, v_ref, g_ref, gt_ref, bias_c_ref, bias_r_ref, tri_ref), refs = refs[:8], refs[8:]
    if not rev:
        (km_ref, vm_ref, gm_ref, gmt_ref), refs = refs[:4], refs[4:]
    if final:
        (hf_ref, mo_ref, ng_ref), refs = refs[:3], refs[3:]
    h_ref, c_scr, n_scr, m_scr = refs
    tri = tri_ref[...]

    @pl.when(pl.program_id(1) == 0)
    def _():
        c_scr[...] = jnp.zeros_like(c_scr)
        n_scr[...] = jnp.zeros_like(n_scr)
        m_scr[...] = jnp.zeros_like(m_scr)
        if not rev:
            terms = _gate_terms(gm_ref[...] + bias_c_ref[...], gmt_ref[...] + bias_r_ref[...], tri, rev, True)
            for h in range(M_HEADS):
                _state_update(h, km_ref[:, h * M_QK_DIM:(h + 1) * M_QK_DIM], vm_ref[:, h * M_V_DIM:(h + 1) * M_V_DIM],
                              *terms, c_scr, n_scr, m_scr)

    li_c, li_r, b_c, b_r, tot_r = _gate_terms(g_ref[...] + bias_c_ref[...], gt_ref[...] + bias_r_ref[...], tri, rev, False)
    causal = tri > 0.5
    for h in range(M_HEADS):
        q = q_ref[:, h * M_QK_DIM:(h + 1) * M_QK_DIM]
        k = k_ref[:, h * M_QK_DIM:(h + 1) * M_QK_DIM]
        v = v_ref[:, h * M_V_DIM:(h + 1) * M_V_DIM]
        bc, br = b_c[:, h:h + 1], b_r[h:h + 1, :]
        m_prev = m_scr[h:h + 1, 0:1]
        dmat = jnp.where(causal, bc - br + li_r[h:h + 1, :], NEG)
        g = bc + m_prev
        m = jnp.maximum(g, jnp.max(dmat, axis=1, keepdims=True))
        decay = jnp.exp(dmat - m)
        inter = jnp.exp(g - m)
        s = lax.dot_general(q, k, (((1,), (1,)), ((), ())), preferred_element_type=F32) * decay
        qf = q.astype(F32)
        lhs = jnp.concatenate([(qf * inter).astype(BF16), s.astype(BF16)], axis=1)
        rhs = jnp.concatenate([c_scr[h].astype(BF16), v], axis=0)
        num = jnp.dot(lhs, rhs, preferred_element_type=F32)
        qn = jnp.sum(qf * n_scr[h:h + 1, :], axis=1, keepdims=True)
        den = inter * qn + jnp.sum(s, axis=1, keepdims=True)
        hv = num * (1.0 / jnp.maximum(jnp.abs(den), jnp.exp(-m)))
        cols = slice(h * M_V_DIM, (h + 1) * M_V_DIM)
        if final:
            hv = hv + hf_ref[:, cols].astype(F32)
            mu = jnp.mean(hv, axis=1, keepdims=True)
            hc = hv - mu
            var = jnp.mean(hc * hc, axis=1, keepdims=True)
            hv = hc * lax.rsqrt(var + M_NORM_EPS) * ng_ref[:, cols] * jax.nn.sigmoid(mo_ref[:, cols].astype(F32))
        h_ref[:, cols] = hv.astype(h_ref.dtype)
        _state_update(h, k, v, li_c, li_r, b_c, b_r, tot_r, c_scr, n_scr, m_scr)


def _mlstm_call(proj, gates, gates_t, bias_c, bias_r, tri, meta, final_in, *, batch, nc, rev, name):
    rows = proj.shape[0]
    final = final_in is not None
    qw, vw = M_HEADS * M_QK_DIM, M_HEADS * M_V_DIM

    def chunk(b, i):
        return b * nc + (nc - 1 - i if rev else i)

    in_specs = [
        pl.BlockSpec((BLOCK, qw), lambda b, i: (chunk(b, i), C_MQ // qw)),
        pl.BlockSpec((BLOCK, qw), lambda b, i: (chunk(b, i), C_MK // qw)),
        pl.BlockSpec((BLOCK, vw), lambda b, i: (chunk(b, i), C_MV // vw)),
        pl.BlockSpec((BLOCK, LANES), lambda b, i: (chunk(b, i), 0)),
        pl.BlockSpec((4 * M_HEADS, BLOCK), lambda b, i: (0, chunk(b, i))),
        pl.BlockSpec((1, LANES), lambda b, i: (0, 0)),
        pl.BlockSpec((4 * M_HEADS, 1), lambda b, i: (0, 0)),
        pl.BlockSpec((BLOCK, BLOCK), lambda b, i: (0, 0)),
    ]
    args = [proj, proj, proj, gates, gates_t, bias_c, bias_r, tri]
    if not rev:
        proj_m, gates_m, gates_mt = meta
        in_specs += [
            pl.BlockSpec((BLOCK, qw), lambda b, i: (0, C_MK // qw)),
            pl.BlockSpec((BLOCK, vw), lambda b, i: (0, C_MV // vw)),
            pl.BlockSpec((BLOCK, LANES), lambda b, i: (0, 0)),
            pl.BlockSpec((4 * M_HEADS, BLOCK), lambda b, i: (0, 0)),
        ]
        args += [proj_m, proj_m, gates_m, gates_mt]
    if final:
        h_fwd, norm_g = final_in
        in_specs += [
            pl.BlockSpec((BLOCK, vw), lambda b, i: (chunk(b, i), 0)),
            pl.BlockSpec((BLOCK, vw), lambda b, i: (chunk(b, i), C_MO // vw)),
            pl.BlockSpec((1, vw), lambda b, i: (0, 0)),
        ]
        args += [h_fwd, proj, norm_g]
    return pl.pallas_call(
        functools.partial(_mlstm_kernel, rev=rev, final=final),
        grid=(batch, nc),
        in_specs=in_specs,
        out_specs=pl.BlockSpec((BLOCK, vw), lambda b, i: (chunk(b, i), 0)),
        out_shape=jax.ShapeDtypeStruct((rows, vw), BF16),
        scratch_shapes=[pltpu.VMEM((M_HEADS, M_QK_DIM, M_V_DIM), F32), pltpu.VMEM((M_HEADS, LANES), F32),
                        pltpu.VMEM((M_HEADS, LANES), F32)],
        compiler_params=_cparams(("arbitrary", "arbitrary")),
        name=name,
    )(*args)


def _merge_kernel(a_ref, m_ref, ga_ref, gb_ref, wa_ref, wb_ref, o_ref):
    ya = jnp.dot(a_ref[...], wa_ref[...], preferred_element_type=F32)
    yb = jnp.dot(m_ref[...], wb_ref[...], preferred_element_type=F32)
    o_ref[...] = (jax.nn.sigmoid(ga_ref[...].astype(F32)) * ya + jax.nn.sigmoid(gb_ref[...].astype(F32)) * yb).astype(BF16)


def _merge_call(attn, mem, proj, wa, wb, *, tm, name):
    rows = attn.shape[0]
    row_spec = pl.BlockSpec((tm, D_MODEL), lambda i: (i, 0))
    w_spec = pl.BlockSpec((D_MODEL, D_MODEL), lambda i: (0, 0), pipeline_mode=pl.Buffered(1))
    return pl.pallas_call(
        _merge_kernel,
        grid=(rows // tm,),
        in_specs=[row_spec, row_spec,
                  pl.BlockSpec((tm, D_MODEL), lambda i: (i, C_GA // D_MODEL)),
                  pl.BlockSpec((tm, D_MODEL), lambda i: (i, C_GB // D_MODEL)),
                  w_spec, w_spec],
        out_specs=row_spec,
        out_shape=jax.ShapeDtypeStruct((rows, D_MODEL), BF16),
        compiler_params=_cparams(("arbitrary",)),
        name=name,
    )(attn, mem, proj, proj, wa, wb)


def _post_kernel(mg_ref, x_ref, eg_ref, eb_ref, wo_ref, g1_ref, b1_ref, wr_hi_ref, wr_lo_ref, rb_ref, low_ref, cnt_in_ref,
                 x1_joint_ref, x1_ref, ri_ref, rw_ref, cnt_ref):
    del x1_joint_ref
    i = pl.program_id(0)
    x0 = _ln_rows(x_ref[...], eg_ref[...], eb_ref[...], LN_EPS)
    y = jnp.dot(mg_ref[...], wo_ref[...], preferred_element_type=F32)
    x1 = _ln_rows(ALPHA * x0 + y, g1_ref[...], b1_ref[...], LN_EPS)
    x1_ref[...] = x1

    hi = x1.astype(BF16)
    lo = (x1 - hi.astype(F32)).astype(BF16)
    logits = (jnp.dot(hi, wr_hi_ref[...], preferred_element_type=F32)
              + jnp.dot(lo, wr_hi_ref[...], preferred_element_type=F32)
              + jnp.dot(hi, wr_lo_ref[...], preferred_element_type=F32)) + rb_ref[...]
    lane = lax.broadcasted_iota(I32, logits.shape, 1)
    big = jnp.int32(LANES)

    def first_max(valid):
        vmax = jnp.max(jnp.where(valid, logits, -jnp.inf), axis=1, keepdims=True)
        idx = jnp.min(jnp.where(valid & (logits == vmax), lane, big), axis=1, keepdims=True)
        return vmax, idx

    is_grp = lane < N_GROUPS
    gmax, grp = first_max(is_grp)
    p_grp = 1.0 / jnp.sum(jnp.where(is_grp, jnp.exp(logits - gmax), 0.0), axis=1, keepdims=True)
    e_lo = N_GROUPS + grp * EXPERTS_PER_GROUP
    in_grp = (lane >= e_lo) & (lane < e_lo + EXPERTS_PER_GROUP)
    l1, i1 = first_max(in_grp)
    l2, i2 = first_max(in_grp & (lane != i1))
    t2 = jnp.exp(l2 - l1)
    w1 = p_grp / (1.0 + t2)
    w2 = p_grp * t2 / (1.0 + t2)
    e1 = i1 - N_GROUPS
    e2 = i2 - N_GROUPS

    @pl.when(i == 0)
    def _():
        cnt_ref[...] = cnt_in_ref[...]

    oh1 = (lane == e1).astype(F32)
    oh2 = (lane == e2).astype(F32)
    cnt = oh1 + oh2
    before = jnp.dot(low_ref[...], cnt.astype(BF16), preferred_element_type=F32) + cnt_ref[...]
    r1 = jnp.sum(oh1 * before, axis=1, keepdims=True).astype(I32)
    r2 = jnp.sum(oh2 * before, axis=1, keepdims=True).astype(I32)
    cnt_ref[...] = cnt_ref[...] + jnp.sum(cnt, axis=0, keepdims=True)

    ri_ref[...] = jnp.where(lane == 0, e1, jnp.where(lane == 1, e2, jnp.where(lane == 2, r1, jnp.where(lane == 3, r2, 0))))
    rw_ref[...] = jnp.where(lane == 0, w1, jnp.where(lane == 1, w2, 0.0))


def _post_call(merged, x, eg, eb, wo, g1, b1, wr_hi, wr_lo, rb, low, cnt_in, x1_joint, *, tm, row_off, total_rows, name):
    rows = merged.shape[0]
    off = row_off // tm
    row_spec = pl.BlockSpec((tm, D_MODEL), lambda i: (i, 0))
    vec_spec = pl.BlockSpec((1, D_MODEL), lambda i: (0, 0))
    lane_spec = pl.BlockSpec((1, LANES), lambda i: (0, 0))
    rt_spec = pl.BlockSpec((D_MODEL, LANES), lambda i: (0, 0))
    assert row_off % tm == 0 and x1_joint.shape == (total_rows, D_MODEL)
    return pl.pallas_call(
        _post_kernel,
        grid=(rows // tm,),
        in_specs=[row_spec, row_spec, vec_spec, vec_spec,
                  pl.BlockSpec((D_MODEL, D_MODEL), lambda i: (0, 0), pipeline_mode=pl.Buffered(1)),
                  vec_spec, vec_spec, rt_spec, rt_spec, lane_spec,
                  pl.BlockSpec((tm, tm), lambda i: (0, 0)), lane_spec, pl.BlockSpec(memory_space=pl.ANY)],
        out_specs=[pl.BlockSpec((tm, D_MODEL), lambda i: (i + off, 0)),
                   pl.BlockSpec((tm, LANES), lambda i: (i, 0)), pl.BlockSpec((tm, LANES), lambda i: (i, 0)),
                   lane_spec],
        out_shape=[jax.ShapeDtypeStruct((total_rows, D_MODEL), F32), jax.ShapeDtypeStruct((rows, LANES), I32),
                   jax.ShapeDtypeStruct((rows, LANES), F32), jax.ShapeDtypeStruct((1, LANES), F32)],
        input_output_aliases={12: 0},
        compiler_params=_cparams(("arbitrary",)),
        name=name,
    )(merged, x, eg, eb, wo, g1, b1, wr_hi, wr_lo, rb, low, cnt_in, x1_joint)


def _row_copy(src, s, dst, d, sem):
    return pltpu.make_async_copy(src.at[pl.ds(s, 1), :], dst.at[pl.ds(d, 1), :], sem)


def _expert_kernel(blk_e_ref, n_act_ref, tok_ref, x_ref, wg_ref, wu_ref, wd_ref, y_ref, xbuf, sem):
    del blk_e_ref
    i = pl.program_id(0)
    n_act = n_act_ref[0]

    def gather(blk):
        slot = blk % 2

        def body(r, carry):
            _row_copy(x_ref, tok_ref[blk * SLOT_BLOCK + r], xbuf.at[slot], r, sem.at[slot]).start()
            return carry

        lax.fori_loop(0, SLOT_BLOCK, body, 0, unroll=8)

    @pl.when(i == 0)
    def _():
        gather(i)

    @pl.when(i + 1 < n_act)
    def _():
        gather(i + 1)

    @pl.when(i < n_act)
    def _():
        slot = i % 2
        pltpu.make_async_copy(x_ref.at[pl.ds(0, SLOT_BLOCK), :], xbuf.at[slot], sem.at[slot]).wait()
        xb = xbuf[slot].astype(BF16)
        gate = jnp.dot(xb, wg_ref[0], preferred_element_type=F32)
        up = jnp.dot(xb, wu_ref[0], preferred_element_type=F32)
        hdn = (gate * jax.nn.sigmoid(gate) * up).astype(BF16)
        y_ref[...] = jnp.dot(hdn, wd_ref[0], preferred_element_type=F32)

    @pl.when(i >= n_act)
    def _():
        y_ref[...] = jnp.zeros_like(y_ref)


def _expert_call(blk_e, n_act, row_tok, x1, wg, wu, wd, *, name):
    n_blocks = blk_e.shape[0]
    grid_spec = pltpu.PrefetchScalarGridSpec(
        num_scalar_prefetch=3,
        grid=(n_blocks,),
        in_specs=[
            pl.BlockSpec(memory_space=pl.ANY),
            pl.BlockSpec((1, D_MODEL, D_EXPERT), lambda i, be, na, tok: (be[i], 0, 0)),
            pl.BlockSpec((1, D_MODEL, D_EXPERT), lambda i, be, na, tok: (be[i], 0, 0)),
            pl.BlockSpec((1, D_EXPERT, D_MODEL), lambda i, be, na, tok: (be[i], 0, 0)),
        ],
        out_specs=pl.BlockSpec((SLOT_BLOCK, D_MODEL), lambda i, be, na, tok: (i, 0)),
        scratch_shapes=[pltpu.VMEM((2, SLOT_BLOCK, D_MODEL), F32), pltpu.SemaphoreType.DMA((2,))],
    )
    return pl.pallas_call(
        _expert_kernel,
        grid_spec=grid_spec,
        out_shape=jax.ShapeDtypeStruct((n_blocks * SLOT_BLOCK, D_MODEL), F32),
        compiler_params=_cparams(("arbitrary",)),
        name=name,
    )(blk_e, n_act, row_tok, x1, wg, wu, wd)


def _combine_kernel(dest_ref, x1_ref, rw_ref, g2_ref, b2_ref, yr_ref, o_ref, ybuf, sem, *, tile):
    i = pl.program_id(0)

    def gather(t):
        slot = t % 2

        def body(r, carry):
            for k in range(2):
                _row_copy(yr_ref, dest_ref[2 * (t * tile + r) + k], ybuf.at[slot, k], r, sem.at[slot]).start()
            return carry

        lax.fori_loop(0, tile, body, 0, unroll=4)

    @pl.when(i == 0)
    def _():
        gather(i)

    @pl.when(i + 1 < pl.num_programs(0))
    def _():
        gather(i + 1)

    slot = i % 2
    for k in range(2):
        pltpu.make_async_copy(yr_ref.at[pl.ds(0, tile), :], ybuf.at[slot, k], sem.at[slot]).wait()
    rw = rw_ref[...]
    ff = rw[:, 0:1] * ybuf[slot, 0] + rw[:, 1:2] * ybuf[slot, 1]
    o_ref[...] = _ln_rows(ALPHA * x1_ref[...] + ff, g2_ref[...], b2_ref[...], LN_EPS)


def _combine_call(dest, x1, rw, g2, b2, yr, *, row_off, tile, name):
    rows = rw.shape[0]
    off = row_off // tile
    vec_spec = pl.BlockSpec((1, D_MODEL), lambda i, d: (0, 0))
    grid_spec = pltpu.PrefetchScalarGridSpec(
        num_scalar_prefetch=1,
        grid=(rows // tile,),
        in_specs=[pl.BlockSpec((tile, D_MODEL), lambda i, d: (i + off, 0)),
                  pl.BlockSpec((tile, LANES), lambda i, d: (i, 0)), vec_spec, vec_spec,
                  pl.BlockSpec(memory_space=pl.ANY)],
        out_specs=pl.BlockSpec((tile, D_MODEL), lambda i, d: (i, 0)),
        scratch_shapes=[pltpu.VMEM((2, 2, tile, D_MODEL), F32), pltpu.SemaphoreType.DMA((2,))],
    )
    return pl.pallas_call(
        functools.partial(_combine_kernel, tile=tile),
        grid_spec=grid_spec,
        out_shape=jax.ShapeDtypeStruct((rows, D_MODEL), F32),
        compiler_params=_cparams(("arbitrary",)),
        name=name,
    )(dest.reshape(-1), x1, rw, g2, b2, yr)


def _rope_tables(pos):
    half = HEAD_DIM // 2
    inv = ROPE_THETA ** (-jnp.arange(half, dtype=F32) / half)
    ang = pos.astype(F32)[:, None] * inv[None, :]
    cos, sin = jnp.cos(ang), jnp.sin(ang)
    return jnp.concatenate([cos, cos], axis=1), jnp.concatenate([-sin, sin], axis=1)


def kernel(x_prompt, x_sample, meta_tokens, ln_emb_g, ln_emb_b, w_in, attn_sink, m_gate_bias, m_norm_g, w_br_attn, w_br_mlstm, w_out, ln1_g, ln1_b, w_router_group, b_router_group, w_router_expert, b_router_expert, w_expert_gate, w_expert_up, w_expert_down, ln2_g, ln2_b):
    assert w_in.shape[0] == DEPTH == 1
    row = lambda v: v.reshape(1, -1).astype(F32)
    eg, eb = row(ln_emb_g), row(ln_emb_b)
    g1, b1, g2, b2 = row(ln1_g[0]), row(ln1_b[0]), row(ln2_g[0]), row(ln2_b[0])

    w = w_in[0]
    sec = {}
    off = 0
    for nm, width in (("aq", 2048), ("ak", 512), ("av", 512), ("mq", 1024), ("mk", 1024), ("mv", 2048), ("mo", 2048),
                      ("mg", 32), ("bg", 4096)):
        sec[nm] = w[:, off:off + width]
        off += width
    w_main = jnp.concatenate([sec[n] for n in ("aq", "mv", "mo", "bg", "mq", "mk", "ak", "av")], axis=1).astype(BF16)
    w_gate = jnp.pad(sec["mg"], ((0, 0), (0, LANES - 4 * M_HEADS))).astype(BF16)
    wa, wb, wo = w_br_attn[0].astype(BF16), w_br_mlstm[0].astype(BF16), w_out[0].astype(BF16)
    wr = jnp.pad(jnp.concatenate([w_router_group[0], w_router_expert[0]], axis=1),
                 ((0, 0), (0, LANES - N_GROUPS - N_EXPERTS)))
    wr_hi = wr.astype(BF16)
    wr_lo = (wr - wr_hi.astype(F32)).astype(BF16)
    rb = jnp.pad(jnp.concatenate([b_router_group[0], b_router_expert[0]]), (0, LANES - N_GROUPS - N_EXPERTS)).reshape(1, LANES)
    weg, weu, wed = w_expert_gate[0].astype(BF16), w_expert_up[0].astype(BF16), w_expert_down[0].astype(BF16)
    bias_c = jnp.pad(m_gate_bias[0].reshape(1, 4 * M_HEADS), ((0, 0), (0, LANES - 4 * M_HEADS))).astype(F32)
    bias_r = m_gate_bias[0].reshape(4 * M_HEADS, 1).astype(F32)
    norm_g = row(m_norm_g[0])
    sink = attn_sink[0].astype(F32)
    t_idx = jnp.arange(BLOCK)
    tri_f = (t_idx[None, :] <= t_idx[:, None]).astype(F32)
    tri_b = (t_idx[None, :] >= t_idx[:, None]).astype(F32)

    xm = jnp.pad(meta_tokens.astype(F32), ((FRONT_PAD, 0), (0, 0)))
    cos_m, sin_m = _rope_tables(jnp.arange(BLOCK) - FRONT_PAD)
    proj_m, gates_m = _proj_call(xm, eg, eb, cos_m, sin_m, w_main, w_gate, tm=BLOCK, zero_front=True, name="proj_meta")
    gates_mt = gates_m[:, :4 * M_HEADS].T

    groups = []
    cnt = jnp.zeros((1, LANES), F32)
    total_rows = x_prompt.shape[0] * x_prompt.shape[1] + x_sample.shape[0] * x_sample.shape[1]
    x1 = jnp.zeros((total_rows, D_MODEL), F32)
    row_off = 0
    for gi, xg in enumerate((x_prompt, x_sample)):
        batch, seq, _ = xg.shape
        assert seq % BLOCK == 0
        nb = seq // BLOCK
        rows = batch * seq
        x2 = xg.reshape(rows, D_MODEL)
        tm = _pick_tile(seq, 1024)
        cos, sin = _rope_tables(jnp.arange(seq) + N_META)
        proj, gates = _proj_call(x2, eg, eb, cos, sin, w_main, w_gate, tm=tm, zero_front=False, name=f"proj_g{gi}")
        gates_t = gates[:, :4 * M_HEADS].T
        attn = _attn_call(sink, proj, proj_m, batch=batch, nb=nb, name=f"attn_g{gi}")
        h_fwd = _mlstm_call(proj, gates, gates_t, bias_c, bias_r, tri_f, (proj_m, gates_m, gates_mt), None,
                            batch=batch, nc=nb, rev=False, name=f"mlstm_fwd_g{gi}")
        mem = _mlstm_call(proj, gates, gates_t, bias_c, bias_r, tri_b, None, (h_fwd, norm_g),
                          batch=batch, nc=nb, rev=True, name=f"mlstm_bwd_g{gi}")
        tp = _pick_tile(rows, 512)
        merged = _merge_call(attn, mem, proj, wa, wb, tm=tp, name=f"merge_g{gi}")
        low = (jnp.arange(tp)[None, :] < jnp.arange(tp)[:, None]).astype(BF16)
        x1, ri, rw, cnt = _post_call(merged, x2, eg, eb, wo, g1, b1, wr_hi, wr_lo, rb, low, cnt, x1, tm=tp,
                                     row_off=row_off, total_rows=total_rows, name=f"post_g{gi}")
        groups.append((xg.shape, row_off, ri[:, :4], rw))
        row_off += rows

    counts = cnt[0, :N_EXPERTS].astype(I32)
    padded = (counts + SLOT_BLOCK - 1) // SLOT_BLOCK * SLOT_BLOCK
    pad_ends = jnp.cumsum(padded)
    pad_starts = pad_ends - padded
    n_blocks = -(-2 * total_rows // SLOT_BLOCK) + N_EXPERTS
    blk_start = jnp.arange(n_blocks, dtype=I32) * SLOT_BLOCK
    blk_e = jnp.minimum(jnp.sum((pad_ends[None, :] <= blk_start[:, None]).astype(I32), axis=1), N_EXPERTS - 1)
    n_act = (pad_ends[-1:] // SLOT_BLOCK).astype(I32)
    ri_all = jnp.concatenate([g[2] for g in groups], axis=0)
    dest = pad_starts[ri_all[:, 0:2]] + ri_all[:, 2:4]
    tok = jnp.broadcast_to(jnp.arange(total_rows, dtype=I32)[:, None], dest.shape)
    row_tok = jnp.zeros((n_blocks * SLOT_BLOCK,), I32).at[dest.reshape(-1)].set(tok.reshape(-1), unique_indices=True)

    yr = _expert_call(blk_e, n_act, row_tok, x1, weg, weu, wed, name="experts")
    outs = []
    for gi, (shape, off, _, rw) in enumerate(groups):
        rows = rw.shape[0]
        out = _combine_call(dest[off:off + rows], x1, rw, g2, b2, yr, row_off=off, tile=_pick_tile(rows, 256),
                            name=f"combine_g{gi}")
        outs.append(out.reshape(shape))
    return tuple(outs)
```

```python
import functools

import jax
import jax.numpy as jnp
from jax import lax
from jax.experimental import pallas as pl
from jax.experimental.pallas import tpu as pltpu

F32 = jnp.float32
BF16 = jnp.bfloat16
I32 = jnp.int32

D_MODEL = 2048
DEPTH = 1
N_META = 16
BLOCK = 128
FRONT_PAD = BLOCK - N_META
N_Q_HEADS = 16
N_KV_HEADS = 4
HEAD_DIM = 128
Q_PER_KV = 4
WINDOW = 128
ROPE_THETA = 10000.0
M_HEADS = 8
M_QK_DIM = 128
M_V_DIM = 256
M_NORM_EPS = 1e-6
N_GROUPS = 4
EXPERTS_PER_GROUP = 8
N_EXPERTS = 32
D_EXPERT = 1024
ALPHA = (2 * DEPTH) ** 0.25
LN_EPS = 1e-5
NEG = -1e30

C_AQ, C_GA, C_GB, C_MK, C_AK, C_AV = 0, 2048, 4096, 6144, 7168, 7680
N_TOK = 8192
F_MV, F_MO, F_MQ = 0, 2048, 4096
N_FEAT = 5120
PROJ_TN = 1024
LANES = 128
SLOT_BLOCK = 256
VMEM_LIMIT = 56 * 1024 * 1024


def _pick_tile(n, pref):
    t = min(pref, n)
    t -= t % BLOCK
    while n % t:
        t -= BLOCK
    return t


def _cparams(sem, vmem=VMEM_LIMIT):
    return pltpu.CompilerParams(dimension_semantics=sem, vmem_limit_bytes=vmem)


def _ln_rows(x, g, b, eps):
    mu = jnp.mean(x, axis=-1, keepdims=True)
    xc = x - mu
    var = jnp.mean(xc * xc, axis=-1, keepdims=True)
    return xc * lax.rsqrt(var + eps) * g + b


def _rope_heads(acc, cos, sin, nheads, scale):
    outs = []
    for h in range(nheads):
        xh = acc[:, h * HEAD_DIM:(h + 1) * HEAD_DIM]
        o = xh * cos + pltpu.roll(xh, HEAD_DIM // 2, axis=1) * sin
        outs.append(o * scale)
    return outs


N_TOK_TILES = N_TOK // PROJ_TN
N_FEAT_TILES = N_FEAT // PROJ_TN
_NT = (((1,), (1,)), ((), ()))


def _proj_kernel(x_ref, g_ref, b_ref, cos_ref, sin_ref, w_ref, wt_ref, wg_ref, out_ref, outt_ref, gate_ref, u_scr, *,
                 zero_front):
    j = pl.program_id(1)

    @pl.when(j == 0)
    def _():
        u = _ln_rows(x_ref[...], g_ref[...], b_ref[...], LN_EPS)
        if zero_front:
            row = lax.broadcasted_iota(I32, u.shape, 0)
            u = jnp.where(row >= FRONT_PAD, u, 0.0)
        ub = u.astype(BF16)
        u_scr[...] = ub
        gate_ref[...] = lax.dot_general(wg_ref[...], ub, _NT, preferred_element_type=F32)

    @pl.when(j < N_TOK_TILES)
    def _():
        acc = jnp.dot(u_scr[...], w_ref[...], preferred_element_type=F32)

        @pl.when(j < C_GA // PROJ_TN)
        def _():
            heads = _rope_heads(acc, cos_ref[...], sin_ref[...], PROJ_TN // HEAD_DIM, HEAD_DIM ** -0.5)
            out_ref[...] = jnp.concatenate(heads, axis=1).astype(BF16)

        @pl.when(j == C_AK // PROJ_TN)
        def _():
            heads = _rope_heads(acc, cos_ref[...], sin_ref[...], N_KV_HEADS, 1.0)
            out_ref[...] = jnp.concatenate(heads + [acc[:, N_KV_HEADS * HEAD_DIM:]], axis=1).astype(BF16)

        @pl.when((j >= C_GA // PROJ_TN) & (j != C_AK // PROJ_TN))
        def _():
            out_ref[...] = acc.astype(BF16)

    @pl.when(j >= N_TOK_TILES)
    def _():
        acc = lax.dot_general(wt_ref[...], u_scr[...], _NT, preferred_element_type=F32)
        scale = jnp.where(j == N_TOK_TILES + F_MQ // PROJ_TN, M_QK_DIM ** -0.5, 1.0)
        outt_ref[...] = (acc * scale).astype(BF16)


def _proj_call(x, g, b, cos, sin, w_tok, w_feat_t, w_gate_t, *, tm, zero_front, name):
    rows = x.shape[0]
    pos_tiles = cos.shape[0] // tm
    grid = (rows // tm, N_TOK_TILES + N_FEAT_TILES)
    tok_tile = lambda j: jnp.minimum(j, N_TOK_TILES - 1)
    feat_tile = lambda j: jnp.clip(j - N_TOK_TILES, 0, N_FEAT_TILES - 1)
    return pl.pallas_call(
        functools.partial(_proj_kernel, zero_front=zero_front),
        grid=grid,
        in_specs=[
            pl.BlockSpec((tm, D_MODEL), lambda i, j: (i, 0)),
            pl.BlockSpec((1, D_MODEL), lambda i, j: (0, 0)),
            pl.BlockSpec((1, D_MODEL), lambda i, j: (0, 0)),
            pl.BlockSpec((tm, HEAD_DIM), lambda i, j: (i % pos_tiles, 0)),
            pl.BlockSpec((tm, HEAD_DIM), lambda i, j: (i % pos_tiles, 0)),
            pl.BlockSpec((D_MODEL, PROJ_TN), lambda i, j: (0, tok_tile(j))),
            pl.BlockSpec((PROJ_TN, D_MODEL), lambda i, j: (feat_tile(j), 0)),
            pl.BlockSpec((LANES, D_MODEL), lambda i, j: (0, 0)),
        ],
        out_specs=[
            pl.BlockSpec((tm, PROJ_TN), lambda i, j: (i, tok_tile(j))),
            pl.BlockSpec((PROJ_TN, tm), lambda i, j: (feat_tile(j), i)),
            pl.BlockSpec((LANES, tm), lambda i, j: (0, i)),
        ],
        out_shape=[jax.ShapeDtypeStruct((rows, N_TOK), BF16), jax.ShapeDtypeStruct((N_FEAT, rows), BF16),
                   jax.ShapeDtypeStruct((LANES, rows), F32)],
        scratch_shapes=[pltpu.VMEM((tm, D_MODEL), BF16)],
        compiler_params=_cparams(("arbitrary", "arbitrary")),
        name=name,
    )(x, g, b, cos, sin, w_tok, w_feat_t, w_gate_t)


ATTN_ROWS = 32


def _attn_kernel(sink_ref, q_ref, kp_ref, kc_ref, kn_ref, vp_ref, vc_ref, vn_ref, km_ref, vm_ref, o_ref,
                 bias_scr, s_scr, p_scr, inv_scr, *, nb):
    i = pl.program_id(1)
    prow = lax.broadcasted_iota(I32, (BLOCK, 4 * BLOCK), 0)
    col = lax.broadcasted_iota(I32, (BLOCK, 4 * BLOCK), 1)
    lo = jnp.where(i > 0, 0, BLOCK)
    hi = jnp.where(i < nb - 1, 3 * BLOCK, 2 * BLOCK)
    band = (jnp.abs(col - BLOCK - prow) <= WINDOW) & (col >= lo) & (col < hi)
    bias_scr[...] = jnp.where(band | (col >= 3 * BLOCK + FRONT_PAD), 0.0, NEG)
    for kv in range(N_KV_HEADS):
        h0 = kv * Q_PER_KV
        ks = slice(kv * HEAD_DIM, (kv + 1) * HEAD_DIM)
        q = jnp.concatenate([q_ref[:, (h0 + g) * HEAD_DIM:(h0 + g + 1) * HEAD_DIM] for g in range(Q_PER_KV)], axis=0)
        k = jnp.concatenate([kp_ref[:, ks], kc_ref[:, ks], kn_ref[:, ks], km_ref[:, ks]], axis=0)
        v = jnp.concatenate([vp_ref[:, ks], vc_ref[:, ks], vn_ref[:, ks], vm_ref[:, ks]], axis=0)
        s_scr[...] = lax.dot_general(q, k, (((1,), (1,)), ((), ())), preferred_element_type=F32)
        for c in range(Q_PER_KV * BLOCK // ATTN_ROWS):
            rows = slice(c * ATTN_ROWS, (c + 1) * ATTN_ROWS)
            p0 = (c * ATTN_ROWS) % BLOCK
            sink = sink_ref[h0 + (c * ATTN_ROWS) // BLOCK]
            s = s_scr[rows, :] + bias_scr[p0:p0 + ATTN_ROWS, :]
            m = jnp.maximum(jnp.max(s, axis=1, keepdims=True), sink)
            p = jnp.exp(s - m)
            inv_scr[rows, :] = 1.0 / (jnp.sum(p, axis=1, keepdims=True) + jnp.exp(sink - m))
            p_scr[rows, :] = p.astype(BF16)
        o = jnp.dot(p_scr[...], v, preferred_element_type=F32) * inv_scr[...]
        for g in range(Q_PER_KV):
            o_ref[:, (h0 + g) * HEAD_DIM:(h0 + g + 1) * HEAD_DIM] = o[g * BLOCK:(g + 1) * BLOCK].astype(BF16)


def _attn_call(sink, proj, proj_meta, *, batch, nb, name):
    rows = proj.shape[0]
    kw = N_KV_HEADS * HEAD_DIM
    kcol, vcol = C_AK // kw, C_AV // kw

    def band_spec(off, colblk):
        def imap(b, i, sink_ref):
            return (b * nb + jnp.clip(i + off, 0, nb - 1), colblk)
        return pl.BlockSpec((BLOCK, kw), imap)

    grid_spec = pltpu.PrefetchScalarGridSpec(
        num_scalar_prefetch=1,
        grid=(batch, nb),
        in_specs=[
            pl.BlockSpec((BLOCK, N_Q_HEADS * HEAD_DIM), lambda b, i, s: (b * nb + i, 0)),
            band_spec(-1, kcol), band_spec(0, kcol), band_spec(1, kcol),
            band_spec(-1, vcol), band_spec(0, vcol), band_spec(1, vcol),
            pl.BlockSpec((BLOCK, kw), lambda b, i, s: (0, kcol)),
            pl.BlockSpec((BLOCK, kw), lambda b, i, s: (0, vcol)),
        ],
        out_specs=pl.BlockSpec((BLOCK, N_Q_HEADS * HEAD_DIM), lambda b, i, s: (b * nb + i, 0)),
        scratch_shapes=[pltpu.VMEM((BLOCK, 4 * BLOCK), F32), pltpu.VMEM((Q_PER_KV * BLOCK, 4 * BLOCK), F32),
                        pltpu.VMEM((Q_PER_KV * BLOCK, 4 * BLOCK), BF16), pltpu.VMEM((Q_PER_KV * BLOCK, 1), F32)],
    )
    return pl.pallas_call(
        functools.partial(_attn_kernel, nb=nb),
        grid_spec=grid_spec,
        out_shape=jax.ShapeDtypeStruct((rows, N_Q_HEADS * HEAD_DIM), BF16),
        compiler_params=_cparams(("arbitrary", "arbitrary")),
        name=name,
    )(sink, proj, proj, proj, proj, proj, proj, proj, proj_meta, proj_meta)


def _scan_lanes(x, op, fill, rev):
    lane = lax.broadcasted_iota(I32, x.shape, 1)
    sh = 1
    while sh < BLOCK:
        if rev:
            y = jnp.where(lane < BLOCK - sh, pltpu.roll(x, BLOCK - sh, axis=1), fill)
        else:
            y = jnp.where(lane >= sh, pltpu.roll(x, sh, axis=1), fill)
        x = op(x, y)
        sh *= 2
    return x


def _rows_to_columns(x, eye):
    hi = x.astype(BF16)
    r1 = x - hi.astype(F32)
    mid = r1.astype(BF16)
    lo = (r1 - mid.astype(F32)).astype(BF16)
    move = lambda a: lax.dot_general(eye, a, _NT, preferred_element_type=F32)
    return move(hi) + move(mid) + move(lo)


def _chunk_terms(gt, bias, m_prev, rev, pad_front):
    d = M_HEADS if rev else 0
    li = gt[d:d + M_HEADS, :] + bias[d:d + M_HEADS, :]
    lf = jax.nn.log_sigmoid(gt[2 * M_HEADS + d:3 * M_HEADS + d, :] + bias[2 * M_HEADS + d:3 * M_HEADS + d, :])
    if pad_front:
        real = lax.broadcasted_iota(I32, li.shape, 1) >= FRONT_PAD
        li, lf = jnp.where(real, li, NEG), jnp.where(real, lf, 0.0)
    b = _scan_lanes(lf, jnp.add, 0.0, rev)
    tot = jnp.sum(lf, axis=1, keepdims=True)
    u = li - b
    big_m = jnp.maximum(m_prev, _scan_lanes(u, jnp.maximum, -jnp.inf, rev))
    inter = jnp.exp(m_prev - big_m)
    e_neg_m = jnp.exp(-(b + big_m))
    m_new = jnp.maximum(tot + m_prev, tot + jnp.max(u, axis=1, keepdims=True))
    w = jnp.exp(tot + u - m_new)
    sp = jnp.exp(tot + m_prev - m_new)
    return u, big_m, inter, e_neg_m, w, sp, m_new


def _state_update(h, k, vt, w, sp, ct_scr, n_scr):
    wh = w[h:h + 1, :]
    lhs = jnp.concatenate([(vt.astype(F32) * wh).astype(BF16), jnp.broadcast_to(wh, (16, BLOCK)).astype(BF16)], axis=0)
    upd = jnp.dot(lhs, k, preferred_element_type=F32)
    sph = sp[h:h + 1, :]
    ct_scr[h] = sph * ct_scr[h] + upd[:M_V_DIM]
    n_scr[h:h + 1, :] = sph * n_scr[h:h + 1, :] + upd[M_V_DIM:M_V_DIM + 1]


def _mlstm_kernel(*refs, rev, final):
    (qt_ref, k_ref, vt_ref, gt_ref, gtn_ref, bias_ref, eye_ref), refs = refs[:7], refs[7:]
    if not rev:
        (km_ref, vtm_ref, gtm_ref), refs = refs[:3], refs[3:]
    if final:
        (hft_ref, mot_ref, ngb_ref), refs = refs[:3], refs[3:]
    ht_ref, ct_scr, n_scr, t_scr, ucol_scr = refs
    bias = bias_ref[...]
    eye = eye_ref[...]

    def stash_terms(gt, m_prev):
        u, big_m, inter, e_neg_m, w, sp, m_new = _chunk_terms(gt, bias, m_prev, rev, False)
        ucol_scr[...] = _rows_to_columns(u, eye)
        for slot, val in enumerate((big_m, inter, e_neg_m, w, sp, m_new)):
            t_scr[slot] = jnp.broadcast_to(val, (M_HEADS, LANES))

    @pl.when(pl.program_id(1) == 0)
    def _():
        ct_scr[...] = jnp.zeros_like(ct_scr)
        n_scr[...] = jnp.zeros_like(n_scr)
        m0 = jnp.zeros((M_HEADS, 1), F32)
        if not rev:
            _, _, _, _, w, sp, m0 = _chunk_terms(gtm_ref[0:4 * M_HEADS, :], bias, m0, rev, True)
            for h in range(M_HEADS):
                _state_update(h, km_ref[:, h * M_QK_DIM:(h + 1) * M_QK_DIM], vtm_ref[h * M_V_DIM:(h + 1) * M_V_DIM, :],
                              w, sp, ct_scr, n_scr)
        stash_terms(gt_ref[0:4 * M_HEADS, :], m0)

    big_m, inter, e_neg_m, w = t_scr[0], t_scr[1], t_scr[2], t_scr[3]
    sp, m_new = t_scr[4][:, 0:1], t_scr[5][:, 0:1]
    u_col = ucol_scr[...]
    s_idx = lax.broadcasted_iota(I32, (BLOCK, BLOCK), 0)
    t_idx = lax.broadcasted_iota(I32, (BLOCK, BLOCK), 1)
    causal = (s_idx >= t_idx) if rev else (s_idx <= t_idx)
    for h in range(M_HEADS):
        qt = qt_ref[h * M_QK_DIM:(h + 1) * M_QK_DIM, :]
        k = k_ref[:, h * M_QK_DIM:(h + 1) * M_QK_DIM]
        vt = vt_ref[h * M_V_DIM:(h + 1) * M_V_DIM, :]
        decay = jnp.exp(jnp.where(causal, u_col[:, h:h + 1] - big_m[h:h + 1, :], NEG))
        n_rows = jnp.broadcast_to(n_scr[h:h + 1, :], (16, M_QK_DIM)).astype(BF16)
        st = jnp.dot(jnp.concatenate([k, n_rows], axis=0), qt, preferred_element_type=F32)
        sd = st[:BLOCK] * decay
        den = inter[h:h + 1, :] * st[BLOCK:BLOCK + 1] + jnp.sum(sd, axis=0, keepdims=True)
        rden = 1.0 / jnp.maximum(jnp.abs(den), e_neg_m[h:h + 1, :])
        lhs = jnp.concatenate([vt, ct_scr[h].astype(BF16)], axis=1)
        rhs = jnp.concatenate([sd.astype(BF16), (qt.astype(F32) * inter[h:h + 1, :]).astype(BF16)], axis=0)
        hv = jnp.dot(lhs, rhs, preferred_element_type=F32) * rden
        rows = slice(h * M_V_DIM, (h + 1) * M_V_DIM)
        if final:
            hv = hv + hft_ref[rows, :].astype(F32)
            mu = jnp.mean(hv, axis=0, keepdims=True)
            hc = hv - mu
            var = jnp.mean(hc * hc, axis=0, keepdims=True)
            hv = hc * lax.rsqrt(var + M_NORM_EPS) * ngb_ref[rows, :] * jax.nn.sigmoid(mot_ref[rows, :].astype(F32))
        ht_ref[rows, :] = hv.astype(ht_ref.dtype)
        _state_update(h, k, vt, w, sp, ct_scr, n_scr)
    stash_terms(gtn_ref[0:4 * M_HEADS, :], m_new)


def _mlstm_call(proj, proj_t, gates_t, bias, eye, meta, final_in, *, batch, nc, rev, name):
    rows = proj.shape[0]
    final = final_in is not None
    qw, vw = M_HEADS * M_QK_DIM, M_HEADS * M_V_DIM

    def chunk(b, i):
        return b * nc + (nc - 1 - i if rev else i)

    in_specs = [
        pl.BlockSpec((qw, BLOCK), lambda b, i: (F_MQ // qw, chunk(b, i))),
        pl.BlockSpec((BLOCK, qw), lambda b, i: (chunk(b, i), C_MK // qw)),
        pl.BlockSpec((vw, BLOCK), lambda b, i: (F_MV // vw, chunk(b, i))),
        pl.BlockSpec((LANES, BLOCK), lambda b, i: (0, chunk(b, i))),
        pl.BlockSpec((LANES, BLOCK), lambda b, i: (0, chunk(b, jnp.minimum(i + 1, nc - 1)))),
        pl.BlockSpec((4 * M_HEADS, 1), lambda b, i: (0, 0)),
        pl.BlockSpec((BLOCK, BLOCK), lambda b, i: (0, 0)),
    ]
    args = [proj_t, proj, proj_t, gates_t, gates_t, bias, eye]
    if not rev:
        proj_m, proj_tm, gates_tm = meta
        in_specs += [
            pl.BlockSpec((BLOCK, qw), lambda b, i: (0, C_MK // qw)),
            pl.BlockSpec((vw, BLOCK), lambda b, i: (F_MV // vw, 0)),
            pl.BlockSpec((LANES, BLOCK), lambda b, i: (0, 0)),
        ]
        args += [proj_m, proj_tm, gates_tm]
    if final:
        h_fwd_t, norm_g_b = final_in
        in_specs += [
            pl.BlockSpec((vw, BLOCK), lambda b, i: (0, chunk(b, i))),
            pl.BlockSpec((vw, BLOCK), lambda b, i: (F_MO // vw, chunk(b, i))),
            pl.BlockSpec((vw, BLOCK), lambda b, i: (0, 0)),
        ]
        args += [h_fwd_t, proj_t, norm_g_b]
    return pl.pallas_call(
        functools.partial(_mlstm_kernel, rev=rev, final=final),
        grid=(batch, nc),
        in_specs=in_specs,
        out_specs=pl.BlockSpec((vw, BLOCK), lambda b, i: (0, chunk(b, i))),
        out_shape=jax.ShapeDtypeStruct((vw, rows), BF16),
        scratch_shapes=[pltpu.VMEM((M_HEADS, M_V_DIM, M_QK_DIM), F32), pltpu.VMEM((M_HEADS, LANES), F32),
                        pltpu.VMEM((6, M_HEADS, LANES), F32), pltpu.VMEM((BLOCK, M_HEADS), F32)],
        compiler_params=_cparams(("arbitrary", "arbitrary")),
        name=name,
    )(*args)


def _merge_kernel(a_ref, mt_ref, ga_ref, gb_ref, wa_ref, wb_ref, o_ref):
    ya = jnp.dot(a_ref[...], wa_ref[...], preferred_element_type=F32)
    yb = lax.dot_general(mt_ref[...], wb_ref[...], (((0,), (0,)), ((), ())), preferred_element_type=F32)
    o_ref[...] = (jax.nn.sigmoid(ga_ref[...].astype(F32)) * ya + jax.nn.sigmoid(gb_ref[...].astype(F32)) * yb).astype(BF16)


def _merge_call(attn, mem_t, proj, wa, wb, *, tm, name):
    rows = attn.shape[0]
    row_spec = pl.BlockSpec((tm, D_MODEL), lambda i: (i, 0))
    w_spec = pl.BlockSpec((D_MODEL, D_MODEL), lambda i: (0, 0), pipeline_mode=pl.Buffered(1))
    return pl.pallas_call(
        _merge_kernel,
        grid=(rows // tm,),
        in_specs=[row_spec, pl.BlockSpec((D_MODEL, tm), lambda i: (0, i)),
                  pl.BlockSpec((tm, D_MODEL), lambda i: (i, C_GA // D_MODEL)),
                  pl.BlockSpec((tm, D_MODEL), lambda i: (i, C_GB // D_MODEL)),
                  w_spec, w_spec],
        out_specs=row_spec,
        out_shape=jax.ShapeDtypeStruct((rows, D_MODEL), BF16),
        compiler_params=_cparams(("arbitrary",)),
        name=name,
    )(attn, mem_t, proj, proj, wa, wb)


def _post_kernel(mg_ref, x_ref, eg_ref, eb_ref, wo_ref, g1_ref, b1_ref, wr_hi_ref, wr_lo_ref, rb_ref, low_ref, cnt_in_ref,
                 x1_joint_ref, x1_ref, ri_ref, rw_ref, cnt_ref):
    del x1_joint_ref
    i = pl.program_id(0)
    x0 = _ln_rows(x_ref[...], eg_ref[...], eb_ref[...], LN_EPS)
    y = jnp.dot(mg_ref[...], wo_ref[...], preferred_element_type=F32)
    x1 = _ln_rows(ALPHA * x0 + y, g1_ref[...], b1_ref[...], LN_EPS)
    x1_ref[...] = x1

    hi = x1.astype(BF16)
    lo = (x1 - hi.astype(F32)).astype(BF16)
    logits = (jnp.dot(hi, wr_hi_ref[...], preferred_element_type=F32)
              + jnp.dot(lo, wr_hi_ref[...], preferred_element_type=F32)
              + jnp.dot(hi, wr_lo_ref[...], preferred_element_type=F32)) + rb_ref[...]
    lane = lax.broadcasted_iota(I32, logits.shape, 1)
    big = jnp.int32(LANES)

    def first_max(valid):
        vmax = jnp.max(jnp.where(valid, logits, -jnp.inf), axis=1, keepdims=True)
        idx = jnp.min(jnp.where(valid & (logits == vmax), lane, big), axis=1, keepdims=True)
        return vmax, idx

    is_grp = lane < N_GROUPS
    gmax, grp = first_max(is_grp)
    p_grp = 1.0 / jnp.sum(jnp.where(is_grp, jnp.exp(logits - gmax), 0.0), axis=1, keepdims=True)
    e_lo = N_GROUPS + grp * EXPERTS_PER_GROUP
    in_grp = (lane >= e_lo) & (lane < e_lo + EXPERTS_PER_GROUP)
    l1, i1 = first_max(in_grp)
    l2, i2 = first_max(in_grp & (lane != i1))
    t2 = jnp.exp(l2 - l1)
    w1 = p_grp / (1.0 + t2)
    w2 = p_grp * t2 / (1.0 + t2)
    e1 = i1 - N_GROUPS
    e2 = i2 - N_GROUPS

    @pl.when(i == 0)
    def _():
        cnt_ref[...] = cnt_in_ref[...]

    oh1 = (lane == e1).astype(F32)
    oh2 = (lane == e2).astype(F32)
    cnt = oh1 + oh2
    before = jnp.dot(low_ref[...], cnt.astype(BF16), preferred_element_type=F32) + cnt_ref[...]
    r1 = jnp.sum(oh1 * before, axis=1, keepdims=True).astype(I32)
    r2 = jnp.sum(oh2 * before, axis=1, keepdims=True).astype(I32)
    cnt_ref[...] = cnt_ref[...] + jnp.sum(cnt, axis=0, keepdims=True)

    ri_ref[...] = jnp.where(lane == 0, e1, jnp.where(lane == 1, e2, jnp.where(lane == 2, r1, jnp.where(lane == 3, r2, 0))))
    rw_ref[...] = jnp.where(lane == 0, w1, jnp.where(lane == 1, w2, 0.0))


def _post_call(merged, x, eg, eb, wo, g1, b1, wr_hi, wr_lo, rb, low, cnt_in, x1_joint, *, tm, row_off, total_rows, name):
    rows = merged.shape[0]
    off = row_off // tm
    row_spec = pl.BlockSpec((tm, D_MODEL), lambda i: (i, 0))
    vec_spec = pl.BlockSpec((1, D_MODEL), lambda i: (0, 0))
    lane_spec = pl.BlockSpec((1, LANES), lambda i: (0, 0))
    rt_spec = pl.BlockSpec((D_MODEL, LANES), lambda i: (0, 0))
    assert row_off % tm == 0 and x1_joint.shape == (total_rows, D_MODEL)
    return pl.pallas_call(
        _post_kernel,
        grid=(rows // tm,),
        in_specs=[row_spec, row_spec, vec_spec, vec_spec,
                  pl.BlockSpec((D_MODEL, D_MODEL), lambda i: (0, 0), pipeline_mode=pl.Buffered(1)),
                  vec_spec, vec_spec, rt_spec, rt_spec, lane_spec,
                  pl.BlockSpec((tm, tm), lambda i: (0, 0)), lane_spec, pl.BlockSpec(memory_space=pl.ANY)],
        out_specs=[pl.BlockSpec((tm, D_MODEL), lambda i: (i + off, 0)),
                   pl.BlockSpec((tm, LANES), lambda i: (i, 0)), pl.BlockSpec((tm, LANES), lambda i: (i, 0)),
                   lane_spec],
        out_shape=[jax.ShapeDtypeStruct((total_rows, D_MODEL), F32), jax.ShapeDtypeStruct((rows, LANES), I32),
                   jax.ShapeDtypeStruct((rows, LANES), F32), jax.ShapeDtypeStruct((1, LANES), F32)],
        input_output_aliases={12: 0},
        compiler_params=_cparams(("arbitrary",)),
        name=name,
    )(merged, x, eg, eb, wo, g1, b1, wr_hi, wr_lo, rb, low, cnt_in, x1_joint)


def _row_copy(src, s, dst, d, sem):
    return pltpu.make_async_copy(src.at[pl.ds(s, 1), :], dst.at[pl.ds(d, 1), :], sem)


GATE_STAGE = 2


def _expert_kernel(blk_e_ref, n_act_ref, tok_ref, x_ref, wg_ref, wu_ref, wd_ref, y_ref, xbuf, sem):
    del blk_e_ref
    i = pl.program_id(0)
    n_act = n_act_ref[0]

    def start(blk, r):
        _row_copy(x_ref, tok_ref[blk * SLOT_BLOCK + r], xbuf.at[blk % 2], r, sem.at[blk % 2]).start()

    def wait(blk):
        pltpu.make_async_copy(x_ref.at[pl.ds(0, SLOT_BLOCK), :], xbuf.at[blk % 2], sem.at[blk % 2]).wait()

    @pl.when(i == 0)
    def _():
        lax.fori_loop(0, SLOT_BLOCK, lambda r, c: (start(i, r), c)[1], 0, unroll=8)

    @pl.when(i < n_act)
    def _():
        wait(i)
        xb = xbuf[i % 2].astype(BF16)
        xbuf[GATE_STAGE, :, 0:D_EXPERT] = jnp.dot(xb, wg_ref[0], preferred_element_type=F32)
        for r in range(SLOT_BLOCK):
            start(i + 1, r)
        up = jnp.dot(xb, wu_ref[0], preferred_element_type=F32)
        gate = xbuf[GATE_STAGE, :, 0:D_EXPERT]
        hdn = (gate * jax.nn.sigmoid(gate) * up).astype(BF16)
        y_ref[...] = jnp.dot(hdn, wd_ref[0], preferred_element_type=F32)

    @pl.when(i >= n_act)
    def _():
        @pl.when(i == n_act)
        def _():
            wait(i)
        y_ref[...] = jnp.zeros_like(y_ref)


def _expert_call(blk_e, n_act, row_tok, x1, wg, wu, wd, *, name):
    n_blocks = blk_e.shape[0]
    grid_spec = pltpu.PrefetchScalarGridSpec(
        num_scalar_prefetch=3,
        grid=(n_blocks,),
        in_specs=[
            pl.BlockSpec(memory_space=pl.ANY),
            pl.BlockSpec((1, D_MODEL, D_EXPERT), lambda i, be, na, tok: (be[i], 0, 0)),
            pl.BlockSpec((1, D_MODEL, D_EXPERT), lambda i, be, na, tok: (be[i], 0, 0)),
            pl.BlockSpec((1, D_EXPERT, D_MODEL), lambda i, be, na, tok: (be[i], 0, 0)),
        ],
        out_specs=pl.BlockSpec((SLOT_BLOCK, D_MODEL), lambda i, be, na, tok: (i, 0)),
        scratch_shapes=[pltpu.VMEM((3, SLOT_BLOCK, D_MODEL), F32), pltpu.SemaphoreType.DMA((2,))],
    )
    return pl.pallas_call(
        _expert_kernel,
        grid_spec=grid_spec,
        out_shape=jax.ShapeDtypeStruct((n_blocks * SLOT_BLOCK, D_MODEL), F32),
        compiler_params=_cparams(("arbitrary",)),
        name=name,
    )(blk_e, n_act, row_tok, x1, wg, wu, wd)


def _combine_kernel(dest_ref, x1_ref, rw_ref, g2_ref, b2_ref, yr_ref, o_ref, ybuf, sem, *, tile):
    i = pl.program_id(0)

    def gather(t):
        slot = t % 2

        def body(r, carry):
            for k in range(2):
                _row_copy(yr_ref, dest_ref[2 * (t * tile + r) + k], ybuf.at[slot, k], r, sem.at[slot]).start()
            return carry

        lax.fori_loop(0, tile, body, 0, unroll=4)

    @pl.when(i == 0)
    def _():
        gather(i)

    @pl.when(i + 1 < pl.num_programs(0))
    def _():
        gather(i + 1)

    slot = i % 2
    for k in range(2):
        pltpu.make_async_copy(yr_ref.at[pl.ds(0, tile), :], ybuf.at[slot, k], sem.at[slot]).wait()
    rw = rw_ref[...]
    ff = rw[:, 0:1] * ybuf[slot, 0] + rw[:, 1:2] * ybuf[slot, 1]
    o_ref[...] = _ln_rows(ALPHA * x1_ref[...] + ff, g2_ref[...], b2_ref[...], LN_EPS)


def _combine_call(dest, x1, rw, g2, b2, yr, *, row_off, tile, name):
    rows = rw.shape[0]
    off = row_off // tile
    vec_spec = pl.BlockSpec((1, D_MODEL), lambda i, d: (0, 0))
    grid_spec = pltpu.PrefetchScalarGridSpec(
        num_scalar_prefetch=1,
        grid=(rows // tile,),
        in_specs=[pl.BlockSpec((tile, D_MODEL), lambda i, d: (i + off, 0)),
                  pl.BlockSpec((tile, LANES), lambda i, d: (i, 0)), vec_spec, vec_spec,
                  pl.BlockSpec(memory_space=pl.ANY)],
        out_specs=pl.BlockSpec((tile, D_MODEL), lambda i, d: (i, 0)),
        scratch_shapes=[pltpu.VMEM((2, 2, tile, D_MODEL), F32), pltpu.SemaphoreType.DMA((2,))],
    )
    return pl.pallas_call(
        functools.partial(_combine_kernel, tile=tile),
        grid_spec=grid_spec,
        out_shape=jax.ShapeDtypeStruct((rows, D_MODEL), F32),
        compiler_params=_cparams(("arbitrary",)),
        name=name,
    )(dest.reshape(-1), x1, rw, g2, b2, yr)


def _rope_tables(pos):
    half = HEAD_DIM // 2
    inv = ROPE_THETA ** (-jnp.arange(half, dtype=F32) / half)
    ang = pos.astype(F32)[:, None] * inv[None, :]
    cos, sin = jnp.cos(ang), jnp.sin(ang)
    return jnp.concatenate([cos, cos], axis=1), jnp.concatenate([-sin, sin], axis=1)


def kernel(x_prompt, x_sample, meta_tokens, ln_emb_g, ln_emb_b, w_in, attn_sink, m_gate_bias, m_norm_g, w_br_attn, w_br_mlstm, w_out, ln1_g, ln1_b, w_router_group, b_router_group, w_router_expert, b_router_expert, w_expert_gate, w_expert_up, w_expert_down, ln2_g, ln2_b):
    assert w_in.shape[0] == DEPTH == 1
    row = lambda v: v.reshape(1, -1).astype(F32)
    eg, eb = row(ln_emb_g), row(ln_emb_b)
    g1, b1, g2, b2 = row(ln1_g[0]), row(ln1_b[0]), row(ln2_g[0]), row(ln2_b[0])

    w = w_in[0]
    sec = {}
    off = 0
    for nm, width in (("aq", 2048), ("ak", 512), ("av", 512), ("mq", 1024), ("mk", 1024), ("mv", 2048), ("mo", 2048),
                      ("mg", 32), ("bg", 4096)):
        sec[nm] = w[:, off:off + width]
        off += width
    w_tok = jnp.concatenate([sec[n] for n in ("aq", "bg", "mk", "ak", "av")], axis=1).astype(BF16)
    w_feat_t = jnp.concatenate([sec[n] for n in ("mv", "mo", "mq")], axis=1).T.astype(BF16)
    w_gate_t = jnp.pad(sec["mg"], ((0, 0), (0, LANES - 4 * M_HEADS))).T.astype(BF16)
    wa, wb, wo = w_br_attn[0].astype(BF16), w_br_mlstm[0].astype(BF16), w_out[0].astype(BF16)
    wr = jnp.pad(jnp.concatenate([w_router_group[0], w_router_expert[0]], axis=1),
                 ((0, 0), (0, LANES - N_GROUPS - N_EXPERTS)))
    wr_hi = wr.astype(BF16)
    wr_lo = (wr - wr_hi.astype(F32)).astype(BF16)
    rb = jnp.pad(jnp.concatenate([b_router_group[0], b_router_expert[0]]), (0, LANES - N_GROUPS - N_EXPERTS)).reshape(1, LANES)
    weg, weu, wed = w_expert_gate[0].astype(BF16), w_expert_up[0].astype(BF16), w_expert_down[0].astype(BF16)
    gate_bias = m_gate_bias[0].reshape(4 * M_HEADS, 1).astype(F32)
    norm_g_b = jnp.broadcast_to(m_norm_g[0].astype(F32)[:, None], (M_HEADS * M_V_DIM, BLOCK))
    sink = attn_sink[0].astype(F32)
    eye = jnp.eye(BLOCK, dtype=BF16)

    xm = jnp.pad(meta_tokens.astype(F32), ((FRONT_PAD, 0), (0, 0)))
    cos_m, sin_m = _rope_tables(jnp.arange(BLOCK) - FRONT_PAD)
    meta = _proj_call(xm, eg, eb, cos_m, sin_m, w_tok, w_feat_t, w_gate_t, tm=BLOCK, zero_front=True, name="proj_meta")
    proj_m = meta[0]

    groups = []
    cnt = jnp.zeros((1, LANES), F32)
    total_rows = x_prompt.shape[0] * x_prompt.shape[1] + x_sample.shape[0] * x_sample.shape[1]
    x1 = jnp.zeros((total_rows, D_MODEL), F32)
    row_off = 0
    for gi, xg in enumerate((x_prompt, x_sample)):
        batch, seq, _ = xg.shape
        assert seq % BLOCK == 0
        nb = seq // BLOCK
        rows = batch * seq
        x2 = xg.reshape(rows, D_MODEL)
        tm = _pick_tile(seq, 1024)
        cos, sin = _rope_tables(jnp.arange(seq) + N_META)
        proj, proj_t, gates_t = _proj_call(x2, eg, eb, cos, sin, w_tok, w_feat_t, w_gate_t, tm=tm, zero_front=False,
                                           name=f"proj_g{gi}")
        attn = _attn_call(sink, proj, proj_m, batch=batch, nb=nb, name=f"attn_g{gi}")
        h_fwd_t = _mlstm_call(proj, proj_t, gates_t, gate_bias, eye, meta, None,
                              batch=batch, nc=nb, rev=False, name=f"mlstm_fwd_g{gi}")
        mem_t = _mlstm_call(proj, proj_t, gates_t, gate_bias, eye, None, (h_fwd_t, norm_g_b),
                            batch=batch, nc=nb, rev=True, name=f"mlstm_bwd_g{gi}")
        tp = _pick_tile(rows, 512)
        merged = _merge_call(attn, mem_t, proj, wa, wb, tm=tp, name=f"merge_g{gi}")
        low = (jnp.arange(tp)[None, :] < jnp.arange(tp)[:, None]).astype(BF16)
        x1, ri, rw, cnt = _post_call(merged, x2, eg, eb, wo, g1, b1, wr_hi, wr_lo, rb, low, cnt, x1, tm=tp,
                                     row_off=row_off, total_rows=total_rows, name=f"post_g{gi}")
        groups.append((xg.shape, row_off, ri[:, :4], rw))
        row_off += rows

    counts = cnt[0, :N_EXPERTS].astype(I32)
    padded = (counts + SLOT_BLOCK - 1) // SLOT_BLOCK * SLOT_BLOCK
    pad_ends = jnp.cumsum(padded)
    pad_starts = pad_ends - padded
    n_blocks = -(-2 * total_rows // SLOT_BLOCK) + N_EXPERTS
    blk_start = jnp.arange(n_blocks, dtype=I32) * SLOT_BLOCK
    blk_e = jnp.minimum(jnp.sum((pad_ends[None, :] <= blk_start[:, None]).astype(I32), axis=1), N_EXPERTS - 1)
    n_act = (pad_ends[-1:] // SLOT_BLOCK).astype(I32)
    ri_all = jnp.concatenate([g[2] for g in groups], axis=0)
    dest = pad_starts[ri_all[:, 0:2]] + ri_all[:, 2:4]
    tok = jnp.broadcast_to(jnp.arange(total_rows, dtype=I32)[:, None], dest.shape)
    row_tok = jnp.zeros((n_blocks * SLOT_BLOCK,), I32).at[dest.reshape(-1)].set(tok.reshape(-1), unique_indices=True)

    yr = _expert_call(blk_e, n_act, row_tok, x1, weg, weu, wed, name="experts")
    outs = []
    for gi, (shape, off, _, rw) in enumerate(groups):
        rows = rw.shape[0]
        out = _combine_call(dest[off:off + rows], x1, rw, g2, b2, yr, row_off=off, tile=_pick_tile(rows, 256),
                            name=f"combine_g{gi}")
        outs.append(out.reshape(shape))
    return tuple(outs)
```

```python
import functools

import jax
import jax.numpy as jnp
from jax import lax
from jax.experimental import pallas as pl
from jax.experimental.pallas import tpu as pltpu

F32 = jnp.float32
BF16 = jnp.bfloat16
I32 = jnp.int32

D_MODEL = 2048
DEPTH = 1
N_META = 16
BLOCK = 128
FRONT_PAD = BLOCK - N_META
N_Q_HEADS = 16
N_KV_HEADS = 4
HEAD_DIM = 128
Q_PER_KV = 4
WINDOW = 128
ROPE_THETA = 10000.0
M_HEADS = 8
M_QK_DIM = 128
M_V_DIM = 256
M_NORM_EPS = 1e-6
N_GROUPS = 4
EXPERTS_PER_GROUP = 8
N_EXPERTS = 32
D_EXPERT = 1024
ALPHA = (2 * DEPTH) ** 0.25
LN_EPS = 1e-5
NEG = -1e30

C_GA, C_GB, C_MK, C_AK, C_AV = 0, 2048, 4096, 5120, 5632
N_TOK = 6144
F_MV, F_MO, F_AQ, F_MQ = 0, 2048, 4096, 6144
N_FEAT = 7168
PROJ_TN = 1024
LANES = 128
SLOT_BLOCK = 256
VMEM_LIMIT = 56 * 1024 * 1024


def _pick_tile(n, pref):
    t = min(pref, n)
    t -= t % BLOCK
    while n % t:
        t -= BLOCK
    return t


def _cparams(sem, vmem=VMEM_LIMIT):
    return pltpu.CompilerParams(dimension_semantics=sem, vmem_limit_bytes=vmem)


def _ln_rows(x, g, b, eps):
    mu = jnp.mean(x, axis=-1, keepdims=True)
    xc = x - mu
    var = jnp.mean(xc * xc, axis=-1, keepdims=True)
    return xc * lax.rsqrt(var + eps) * g + b


def _rope_heads(acc, cos, sin, nheads, scale):
    outs = []
    for h in range(nheads):
        xh = acc[:, h * HEAD_DIM:(h + 1) * HEAD_DIM]
        o = xh * cos + pltpu.roll(xh, HEAD_DIM // 2, axis=1) * sin
        outs.append(o * scale)
    return outs


N_TOK_TILES = N_TOK // PROJ_TN
N_FEAT_TILES = N_FEAT // PROJ_TN
_NT = (((1,), (1,)), ((), ()))


def _rope_heads_t(acc, cos_t, sin_t, scale):
    half = HEAD_DIM // 2
    outs = []
    for h in range(acc.shape[0] // HEAD_DIM):
        x1 = acc[h * HEAD_DIM:h * HEAD_DIM + half, :]
        x2 = acc[h * HEAD_DIM + half:(h + 1) * HEAD_DIM, :]
        outs += [(x1 * cos_t - x2 * sin_t) * scale, (x2 * cos_t + x1 * sin_t) * scale]
    return jnp.concatenate(outs, axis=0)


def _proj_kernel(x_ref, g_ref, b_ref, cos_ref, sin_ref, cost_ref, sint_ref, w_ref, wt_ref, wg_ref,
                 out_ref, outt_ref, gate_ref, u_scr, *, zero_front):
    j = pl.program_id(1)

    @pl.when(j == 0)
    def _():
        u = _ln_rows(x_ref[...], g_ref[...], b_ref[...], LN_EPS)
        if zero_front:
            row = lax.broadcasted_iota(I32, u.shape, 0)
            u = jnp.where(row >= FRONT_PAD, u, 0.0)
        ub = u.astype(BF16)
        u_scr[...] = ub
        gate_ref[...] = lax.dot_general(wg_ref[...], ub, _NT, preferred_element_type=F32)

    @pl.when(j < N_TOK_TILES)
    def _():
        acc = jnp.dot(u_scr[...], w_ref[...], preferred_element_type=F32)

        @pl.when(j == C_AK // PROJ_TN)
        def _():
            heads = _rope_heads(acc, cos_ref[...], sin_ref[...], N_KV_HEADS, 1.0)
            out_ref[...] = jnp.concatenate(heads + [acc[:, N_KV_HEADS * HEAD_DIM:]], axis=1).astype(BF16)

        @pl.when(j != C_AK // PROJ_TN)
        def _():
            out_ref[...] = acc.astype(BF16)

    @pl.when(j >= N_TOK_TILES)
    def _():
        acc = lax.dot_general(wt_ref[...], u_scr[...], _NT, preferred_element_type=F32)
        jt = j - N_TOK_TILES
        is_aq = (jt >= F_AQ // PROJ_TN) & (jt < F_MQ // PROJ_TN)

        @pl.when(is_aq)
        def _():
            outt_ref[...] = _rope_heads_t(acc, cost_ref[...], sint_ref[...], HEAD_DIM ** -0.5).astype(BF16)

        @pl.when(jnp.logical_not(is_aq))
        def _():
            scale = jnp.where(jt == F_MQ // PROJ_TN, M_QK_DIM ** -0.5, 1.0)
            outt_ref[...] = (acc * scale).astype(BF16)


def _proj_call(x, g, b, rope, w_tok, w_feat_t, w_gate_t, *, tm, zero_front, name):
    rows = x.shape[0]
    cos, sin, cos_t, sin_t = rope
    pos_tiles = cos.shape[0] // tm
    grid = (rows // tm, N_TOK_TILES + N_FEAT_TILES)
    tok_tile = lambda j: jnp.minimum(j, N_TOK_TILES - 1)
    feat_tile = lambda j: jnp.clip(j - N_TOK_TILES, 0, N_FEAT_TILES - 1)
    return pl.pallas_call(
        functools.partial(_proj_kernel, zero_front=zero_front),
        grid=grid,
        in_specs=[
            pl.BlockSpec((tm, D_MODEL), lambda i, j: (i, 0)),
            pl.BlockSpec((1, D_MODEL), lambda i, j: (0, 0)),
            pl.BlockSpec((1, D_MODEL), lambda i, j: (0, 0)),
            pl.BlockSpec((tm, HEAD_DIM), lambda i, j: (i % pos_tiles, 0)),
            pl.BlockSpec((tm, HEAD_DIM), lambda i, j: (i % pos_tiles, 0)),
            pl.BlockSpec((HEAD_DIM // 2, tm), lambda i, j: (0, i % pos_tiles)),
            pl.BlockSpec((HEAD_DIM // 2, tm), lambda i, j: (0, i % pos_tiles)),
            pl.BlockSpec((D_MODEL, PROJ_TN), lambda i, j: (0, tok_tile(j))),
            pl.BlockSpec((PROJ_TN, D_MODEL), lambda i, j: (feat_tile(j), 0)),
            pl.BlockSpec((LANES, D_MODEL), lambda i, j: (0, 0)),
        ],
        out_specs=[
            pl.BlockSpec((tm, PROJ_TN), lambda i, j: (i, tok_tile(j))),
            pl.BlockSpec((PROJ_TN, tm), lambda i, j: (feat_tile(j), i)),
            pl.BlockSpec((LANES, tm), lambda i, j: (0, i)),
        ],
        out_shape=[jax.ShapeDtypeStruct((rows, N_TOK), BF16), jax.ShapeDtypeStruct((N_FEAT, rows), BF16),
                   jax.ShapeDtypeStruct((LANES, rows), F32)],
        scratch_shapes=[pltpu.VMEM((tm, D_MODEL), BF16)],
        compiler_params=_cparams(("arbitrary", "arbitrary")),
        name=name,
    )(x, g, b, cos, sin, cos_t, sin_t, w_tok, w_feat_t, w_gate_t)


def _attn_kernel(sink_ref, qt_ref, kp_ref, kc_ref, kn_ref, vp_ref, vc_ref, vn_ref, km_ref, vm_ref, ot_ref,
                 bias_scr, s_scr, p_scr, *, nb):
    i = pl.program_id(1)
    key = lax.broadcasted_iota(I32, (4 * BLOCK, BLOCK), 0)
    qpos = lax.broadcasted_iota(I32, (4 * BLOCK, BLOCK), 1)
    lo = jnp.where(i > 0, 0, BLOCK)
    hi = jnp.where(i < nb - 1, 3 * BLOCK, 2 * BLOCK)
    band = (jnp.abs(key - BLOCK - qpos) <= WINDOW) & (key >= lo) & (key < hi)
    bias_scr[...] = jnp.where(band | (key >= 3 * BLOCK + FRONT_PAD), 0.0, NEG)
    key_blocks = [slice(c * BLOCK, (c + 1) * BLOCK) for c in range(4)]

    def scores(c, cols):
        s = s_scr[key_blocks[c], cols]
        return s if c == 1 else s + bias_scr[key_blocks[c], :]

    for kv in range(N_KV_HEADS):
        h0 = kv * Q_PER_KV
        ks = slice(kv * HEAD_DIM, (kv + 1) * HEAD_DIM)
        k = jnp.concatenate([kp_ref[:, ks], kc_ref[:, ks], kn_ref[:, ks], km_ref[:, ks]], axis=0)
        v = jnp.concatenate([vp_ref[:, ks], vc_ref[:, ks], vn_ref[:, ks], vm_ref[:, ks]], axis=0)
        qt = jnp.concatenate([qt_ref[(h0 + g) * HEAD_DIM:(h0 + g + 1) * HEAD_DIM, :] for g in range(Q_PER_KV)], axis=1)
        s_scr[...] = jnp.dot(k, qt, preferred_element_type=F32)
        inv = []
        for g in range(Q_PER_KV):
            cols = slice(g * BLOCK, (g + 1) * BLOCK)
            sink = sink_ref[h0 + g]
            m = jnp.full((1, BLOCK), sink, F32)
            for c in range(4):
                m = jnp.maximum(m, jnp.max(scores(c, cols), axis=0, keepdims=True))
            den = jnp.exp(sink - m)
            for c in range(4):
                p = jnp.exp(scores(c, cols) - m)
                den = den + jnp.sum(p, axis=0, keepdims=True)
                p_scr[key_blocks[c], cols] = p.astype(BF16)
            inv.append(1.0 / den)
        ot = lax.dot_general(v, p_scr[...], (((0,), (0,)), ((), ())), preferred_element_type=F32)
        for g in range(Q_PER_KV):
            ot_ref[(h0 + g) * HEAD_DIM:(h0 + g + 1) * HEAD_DIM, :] = (ot[:, g * BLOCK:(g + 1) * BLOCK] * inv[g]).astype(BF16)


def _attn_call(sink, proj, proj_t, proj_meta, *, batch, nb, name):
    rows = proj.shape[0]
    kw = N_KV_HEADS * HEAD_DIM
    qw = N_Q_HEADS * HEAD_DIM
    kcol, vcol = C_AK // kw, C_AV // kw

    def band_spec(off, colblk):
        def imap(b, i, sink_ref):
            return (b * nb + jnp.clip(i + off, 0, nb - 1), colblk)
        return pl.BlockSpec((BLOCK, kw), imap)

    grid_spec = pltpu.PrefetchScalarGridSpec(
        num_scalar_prefetch=1,
        grid=(batch, nb),
        in_specs=[
            pl.BlockSpec((qw, BLOCK), lambda b, i, s: (F_AQ // qw, b * nb + i)),
            band_spec(-1, kcol), band_spec(0, kcol), band_spec(1, kcol),
            band_spec(-1, vcol), band_spec(0, vcol), band_spec(1, vcol),
            pl.BlockSpec((BLOCK, kw), lambda b, i, s: (0, kcol)),
            pl.BlockSpec((BLOCK, kw), lambda b, i, s: (0, vcol)),
        ],
        out_specs=pl.BlockSpec((qw, BLOCK), lambda b, i, s: (0, b * nb + i)),
        scratch_shapes=[pltpu.VMEM((4 * BLOCK, BLOCK), F32), pltpu.VMEM((4 * BLOCK, Q_PER_KV * BLOCK), F32),
                        pltpu.VMEM((4 * BLOCK, Q_PER_KV * BLOCK), BF16)],
    )
    return pl.pallas_call(
        functools.partial(_attn_kernel, nb=nb),
        grid_spec=grid_spec,
        out_shape=jax.ShapeDtypeStruct((qw, rows), BF16),
        compiler_params=_cparams(("arbitrary", "arbitrary")),
        name=name,
    )(sink, proj_t, proj, proj, proj, proj, proj, proj, proj_meta, proj_meta)


MLSTM_HEAD_GROUP = 4


def _scan_lanes(x, op, fill, rev):
    lane = lax.broadcasted_iota(I32, x.shape, 1)
    sh = 1
    while sh < BLOCK:
        if rev:
            y = jnp.where(lane < BLOCK - sh, pltpu.roll(x, BLOCK - sh, axis=1), fill)
        else:
            y = jnp.where(lane >= sh, pltpu.roll(x, sh, axis=1), fill)
        x = op(x, y)
        sh *= 2
    return x


def _rows_to_columns(x, eye):
    hi = x.astype(BF16)
    r1 = x - hi.astype(F32)
    mid = r1.astype(BF16)
    lo = (r1 - mid.astype(F32)).astype(BF16)
    move = lambda a: lax.dot_general(eye, a, _NT, preferred_element_type=F32)
    return move(hi) + move(mid) + move(lo)


def _chunk_terms(gt, bias, m_prev, rev, pad_front):
    d = M_HEADS if rev else 0
    li = gt[d:d + M_HEADS, :] + bias[d:d + M_HEADS, :]
    lf = jax.nn.log_sigmoid(gt[2 * M_HEADS + d:3 * M_HEADS + d, :] + bias[2 * M_HEADS + d:3 * M_HEADS + d, :])
    if pad_front:
        real = lax.broadcasted_iota(I32, li.shape, 1) >= FRONT_PAD
        li, lf = jnp.where(real, li, NEG), jnp.where(real, lf, 0.0)
    b = _scan_lanes(lf, jnp.add, 0.0, rev)
    tot = jnp.sum(lf, axis=1, keepdims=True)
    u = li - b
    big_m = jnp.maximum(m_prev, _scan_lanes(u, jnp.maximum, -jnp.inf, rev))
    inter = jnp.exp(m_prev - big_m)
    e_neg_m = jnp.exp(-(b + big_m))
    m_new = jnp.maximum(tot + m_prev, tot + jnp.max(u, axis=1, keepdims=True))
    w = jnp.exp(tot + u - m_new)
    sp = jnp.exp(tot + m_prev - m_new)
    return u, big_m, inter, e_neg_m, w, sp, m_new


def _state_update(h, k, vt, w, sp, ct_scr, n_scr):
    wh = w[h:h + 1, :]
    lhs = jnp.concatenate([(vt.astype(F32) * wh).astype(BF16), jnp.broadcast_to(wh, (16, BLOCK)).astype(BF16)], axis=0)
    upd = jnp.dot(lhs, k, preferred_element_type=F32)
    sph = sp[h:h + 1, :]
    ct_scr[h] = sph * ct_scr[h] + upd[:M_V_DIM]
    n_scr[h:h + 1, :] = sph * n_scr[h:h + 1, :] + upd[M_V_DIM:M_V_DIM + 1]


def _mlstm_kernel(*refs, rev, final):
    (qt_ref, k_ref, vt_ref, gt_ref, gtn_ref, bias_ref, eye_ref), refs = refs[:7], refs[7:]
    if not rev:
        (km_ref, vtm_ref, gtm_ref), refs = refs[:3], refs[3:]
    if final:
        (hft_ref, mot_ref, ngb_ref), refs = refs[:3], refs[3:]
    ht_ref, ct_scr, n_scr, t_scr, ucol_scr = refs
    bias = bias_ref[...]
    eye = eye_ref[...]

    def stash_terms(gt, m_prev):
        u, big_m, inter, e_neg_m, w, sp, m_new = _chunk_terms(gt, bias, m_prev, rev, False)
        ucol_scr[...] = _rows_to_columns(u, eye)
        for slot, val in enumerate((big_m, inter, e_neg_m, w, sp, m_new)):
            t_scr[slot] = jnp.broadcast_to(val, (M_HEADS, LANES))

    @pl.when(pl.program_id(1) == 0)
    def _():
        ct_scr[...] = jnp.zeros_like(ct_scr)
        n_scr[...] = jnp.zeros_like(n_scr)
        m0 = jnp.zeros((M_HEADS, 1), F32)
        if not rev:
            _, _, _, _, w, sp, m0 = _chunk_terms(gtm_ref[0:4 * M_HEADS, :], bias, m0, rev, True)
            for h in range(M_HEADS):
                _state_update(h, km_ref[:, h * M_QK_DIM:(h + 1) * M_QK_DIM], vtm_ref[h * M_V_DIM:(h + 1) * M_V_DIM, :],
                              w, sp, ct_scr, n_scr)
        stash_terms(gt_ref[0:4 * M_HEADS, :], m0)

    big_m, inter, e_neg_m, w = t_scr[0], t_scr[1], t_scr[2], t_scr[3]
    sp, m_new = t_scr[4][:, 0:1], t_scr[5][:, 0:1]
    u_col = ucol_scr[...]
    s_idx = lax.broadcasted_iota(I32, (BLOCK, BLOCK), 0)
    t_idx = lax.broadcasted_iota(I32, (BLOCK, BLOCK), 1)
    causal = (s_idx >= t_idx) if rev else (s_idx <= t_idx)
    qt_of = lambda h: qt_ref[h * M_QK_DIM:(h + 1) * M_QK_DIM, :]
    k_of = lambda h: k_ref[:, h * M_QK_DIM:(h + 1) * M_QK_DIM]
    vt_of = lambda h: vt_ref[h * M_V_DIM:(h + 1) * M_V_DIM, :]
    for g0 in range(0, M_HEADS, MLSTM_HEAD_GROUP):
        hs = range(g0, g0 + MLSTM_HEAD_GROUP)
        st = {h: jnp.dot(jnp.concatenate([k_of(h), jnp.broadcast_to(n_scr[h:h + 1, :], (16, M_QK_DIM)).astype(BF16)], axis=0),
                         qt_of(h), preferred_element_type=F32) for h in hs}
        decay = {h: jnp.exp(jnp.where(causal, u_col[:, h:h + 1] - big_m[h:h + 1, :], NEG)) for h in hs}
        sd = {h: st[h][:BLOCK] * decay[h] for h in hs}
        rden = {}
        for h in hs:
            den = inter[h:h + 1, :] * st[h][BLOCK:BLOCK + 1] + jnp.sum(sd[h], axis=0, keepdims=True)
            rden[h] = 1.0 / jnp.maximum(jnp.abs(den), e_neg_m[h:h + 1, :])
        hv = {}
        for h in hs:
            lhs = jnp.concatenate([vt_of(h), ct_scr[h].astype(BF16)], axis=1)
            rhs = jnp.concatenate([sd[h].astype(BF16), (qt_of(h).astype(F32) * inter[h:h + 1, :]).astype(BF16)], axis=0)
            hv[h] = jnp.dot(lhs, rhs, preferred_element_type=F32) * rden[h]
        for h in hs:
            rows = slice(h * M_V_DIM, (h + 1) * M_V_DIM)
            x = hv[h]
            if final:
                x = x + hft_ref[rows, :].astype(F32)
                mu = jnp.mean(x, axis=0, keepdims=True)
                xc = x - mu
                var = jnp.mean(xc * xc, axis=0, keepdims=True)
                x = xc * lax.rsqrt(var + M_NORM_EPS) * ngb_ref[rows, :] * jax.nn.sigmoid(mot_ref[rows, :].astype(F32))
            ht_ref[rows, :] = x.astype(ht_ref.dtype)
        for h in hs:
            _state_update(h, k_of(h), vt_of(h), w, sp, ct_scr, n_scr)
    stash_terms(gtn_ref[0:4 * M_HEADS, :], m_new)


def _mlstm_call(proj, proj_t, gates_t, bias, eye, meta, final_in, *, batch, nc, rev, name):
    rows = proj.shape[0]
    final = final_in is not None
    qw, vw = M_HEADS * M_QK_DIM, M_HEADS * M_V_DIM

    def chunk(b, i):
        return b * nc + (nc - 1 - i if rev else i)

    in_specs = [
        pl.BlockSpec((qw, BLOCK), lambda b, i: (F_MQ // qw, chunk(b, i))),
        pl.BlockSpec((BLOCK, qw), lambda b, i: (chunk(b, i), C_MK // qw)),
        pl.BlockSpec((vw, BLOCK), lambda b, i: (F_MV // vw, chunk(b, i))),
        pl.BlockSpec((LANES, BLOCK), lambda b, i: (0, chunk(b, i))),
        pl.BlockSpec((LANES, BLOCK), lambda b, i: (0, chunk(b, jnp.minimum(i + 1, nc - 1)))),
        pl.BlockSpec((4 * M_HEADS, 1), lambda b, i: (0, 0)),
        pl.BlockSpec((BLOCK, BLOCK), lambda b, i: (0, 0)),
    ]
    args = [proj_t, proj, proj_t, gates_t, gates_t, bias, eye]
    if not rev:
        proj_m, proj_tm, gates_tm = meta
        in_specs += [
            pl.BlockSpec((BLOCK, qw), lambda b, i: (0, C_MK // qw)),
            pl.BlockSpec((vw, BLOCK), lambda b, i: (F_MV // vw, 0)),
            pl.BlockSpec((LANES, BLOCK), lambda b, i: (0, 0)),
        ]
        args += [proj_m, proj_tm, gates_tm]
    if final:
        h_fwd_t, norm_g_b = final_in
        in_specs += [
            pl.BlockSpec((vw, BLOCK), lambda b, i: (0, chunk(b, i))),
            pl.BlockSpec((vw, BLOCK), lambda b, i: (F_MO // vw, chunk(b, i))),
            pl.BlockSpec((vw, BLOCK), lambda b, i: (0, 0)),
        ]
        args += [h_fwd_t, proj_t, norm_g_b]
    return pl.pallas_call(
        functools.partial(_mlstm_kernel, rev=rev, final=final),
        grid=(batch, nc),
        in_specs=in_specs,
        out_specs=pl.BlockSpec((vw, BLOCK), lambda b, i: (0, chunk(b, i))),
        out_shape=jax.ShapeDtypeStruct((vw, rows), BF16),
        scratch_shapes=[pltpu.VMEM((M_HEADS, M_V_DIM, M_QK_DIM), F32), pltpu.VMEM((M_HEADS, LANES), F32),
                        pltpu.VMEM((6, M_HEADS, LANES), F32), pltpu.VMEM((BLOCK, M_HEADS), F32)],
        compiler_params=_cparams(("arbitrary", "arbitrary")),
        name=name,
    )(*args)


def _merge_kernel(at_ref, mt_ref, ga_ref, gb_ref, wa_ref, wb_ref, o_ref):
    tn = (((0,), (0,)), ((), ()))
    ya = lax.dot_general(at_ref[...], wa_ref[...], tn, preferred_element_type=F32)
    yb = lax.dot_general(mt_ref[...], wb_ref[...], tn, preferred_element_type=F32)
    o_ref[...] = (jax.nn.sigmoid(ga_ref[...].astype(F32)) * ya + jax.nn.sigmoid(gb_ref[...].astype(F32)) * yb).astype(BF16)


def _merge_call(attn_t, mem_t, proj, wa, wb, *, tm, name):
    rows = proj.shape[0]
    row_spec = pl.BlockSpec((tm, D_MODEL), lambda i: (i, 0))
    feat_spec = pl.BlockSpec((D_MODEL, tm), lambda i: (0, i))
    w_spec = pl.BlockSpec((D_MODEL, D_MODEL), lambda i: (0, 0), pipeline_mode=pl.Buffered(1))
    return pl.pallas_call(
        _merge_kernel,
        grid=(rows // tm,),
        in_specs=[feat_spec, feat_spec,
                  pl.BlockSpec((tm, D_MODEL), lambda i: (i, C_GA // D_MODEL)),
                  pl.BlockSpec((tm, D_MODEL), lambda i: (i, C_GB // D_MODEL)),
                  w_spec, w_spec],
        out_specs=row_spec,
        out_shape=jax.ShapeDtypeStruct((rows, D_MODEL), BF16),
        compiler_params=_cparams(("arbitrary",)),
        name=name,
    )(attn_t, mem_t, proj, proj, wa, wb)


def _post_kernel(mg_ref, x_ref, eg_ref, eb_ref, wo_ref, g1_ref, b1_ref, wr_hi_ref, wr_lo_ref, rb_ref, low_ref, cnt_in_ref,
                 x1_joint_ref, x1_ref, ri_ref, rw_ref, cnt_ref):
    del x1_joint_ref

    @pl.when(pl.program_id(0) == 0)
    def _():
        cnt_ref[...] = cnt_in_ref[...]

    x0 = _ln_rows(x_ref[...], eg_ref[...], eb_ref[...], LN_EPS)
    y = jnp.dot(mg_ref[...], wo_ref[...], preferred_element_type=F32)
    x1 = _ln_rows(ALPHA * x0 + y, g1_ref[...], b1_ref[...], LN_EPS)
    x1_ref[...] = x1

    hi = x1.astype(BF16)
    lo = (x1 - hi.astype(F32)).astype(BF16)
    logits = (jnp.dot(hi, wr_hi_ref[...], preferred_element_type=F32)
              + jnp.dot(lo, wr_hi_ref[...], preferred_element_type=F32)
              + jnp.dot(hi, wr_lo_ref[...], preferred_element_type=F32)) + rb_ref[...]
    lane = lax.broadcasted_iota(I32, logits.shape, 1)
    big = jnp.int32(LANES)

    def first_max(valid):
        vmax = jnp.max(jnp.where(valid, logits, -jnp.inf), axis=1, keepdims=True)
        idx = jnp.min(jnp.where(valid & (logits == vmax), lane, big), axis=1, keepdims=True)
        return vmax, idx

    is_grp = lane < N_GROUPS
    gmax, grp = first_max(is_grp)
    p_grp = 1.0 / jnp.sum(jnp.where(is_grp, jnp.exp(logits - gmax), 0.0), axis=1, keepdims=True)
    e_lo = N_GROUPS + grp * EXPERTS_PER_GROUP
    in_grp = (lane >= e_lo) & (lane < e_lo + EXPERTS_PER_GROUP)
    l1, i1 = first_max(in_grp)
    l2, i2 = first_max(in_grp & (lane != i1))
    t2 = jnp.exp(l2 - l1)
    w1 = p_grp / (1.0 + t2)
    w2 = p_grp * t2 / (1.0 + t2)
    e1 = i1 - N_GROUPS
    e2 = i2 - N_GROUPS

    oh1 = (lane == e1).astype(F32)
    oh2 = (lane == e2).astype(F32)
    cnt = oh1 + oh2
    before = jnp.dot(low_ref[...], cnt.astype(BF16), preferred_element_type=F32) + cnt_ref[...]
    r1 = jnp.sum(oh1 * before, axis=1, keepdims=True).astype(I32)
    r2 = jnp.sum(oh2 * before, axis=1, keepdims=True).astype(I32)
    cnt_ref[...] = cnt_ref[...] + jnp.sum(cnt, axis=0, keepdims=True)
    ri_ref[...] = jnp.where(lane == 0, e1, jnp.where(lane == 1, e2, jnp.where(lane == 2, r1, jnp.where(lane == 3, r2, 0))))
    rw_ref[...] = jnp.where(lane == 0, w1, jnp.where(lane == 1, w2, 0.0))


def _post_call(merged, x, eg, eb, wo, g1, b1, wr_hi, wr_lo, rb, low, cnt_in, x1_joint, *, tm, row_off, total_rows, name):
    rows = merged.shape[0]
    off = row_off // tm
    row_spec = pl.BlockSpec((tm, D_MODEL), lambda i: (i, 0))
    vec_spec = pl.BlockSpec((1, D_MODEL), lambda i: (0, 0))
    lane_spec = pl.BlockSpec((1, LANES), lambda i: (0, 0))
    rt_spec = pl.BlockSpec((D_MODEL, LANES), lambda i: (0, 0))
    assert row_off % tm == 0 and x1_joint.shape == (total_rows, D_MODEL)
    return pl.pallas_call(
        _post_kernel,
        grid=(rows // tm,),
        in_specs=[row_spec, row_spec, vec_spec, vec_spec,
                  pl.BlockSpec((D_MODEL, D_MODEL), lambda i: (0, 0), pipeline_mode=pl.Buffered(1)),
                  vec_spec, vec_spec, rt_spec, rt_spec, lane_spec,
                  pl.BlockSpec((tm, tm), lambda i: (0, 0)), lane_spec, pl.BlockSpec(memory_space=pl.ANY)],
        out_specs=[pl.BlockSpec((tm, D_MODEL), lambda i: (i + off, 0)),
                   pl.BlockSpec((tm, LANES), lambda i: (i, 0)), pl.BlockSpec((tm, LANES), lambda i: (i, 0)),
                   lane_spec],
        out_shape=[jax.ShapeDtypeStruct((total_rows, D_MODEL), F32), jax.ShapeDtypeStruct((rows, LANES), I32),
                   jax.ShapeDtypeStruct((rows, LANES), F32), jax.ShapeDtypeStruct((1, LANES), F32)],
        input_output_aliases={12: 0},
        compiler_params=_cparams(("arbitrary",)),
        name=name,
    )(merged, x, eg, eb, wo, g1, b1, wr_hi, wr_lo, rb, low, cnt_in, x1_joint)


def _row_copy(src, s, dst, d, sem):
    return pltpu.make_async_copy(src.at[pl.ds(s, 1), :], dst.at[pl.ds(d, 1), :], sem)


GATE_STAGE = 2


def _expert_kernel(blk_e_ref, n_act_ref, tok_ref, x_ref, wg_ref, wu_ref, wd_ref, y_ref, xbuf, sem):
    del blk_e_ref
    i = pl.program_id(0)
    n_act = n_act_ref[0]

    def start(blk, r):
        _row_copy(x_ref, tok_ref[blk * SLOT_BLOCK + r], xbuf.at[blk % 2], r, sem.at[blk % 2]).start()

    def wait(blk):
        pltpu.make_async_copy(x_ref.at[pl.ds(0, SLOT_BLOCK), :], xbuf.at[blk % 2], sem.at[blk % 2]).wait()

    @pl.when(i == 0)
    def _():
        lax.fori_loop(0, SLOT_BLOCK, lambda r, c: (start(i, r), c)[1], 0, unroll=8)

    @pl.when(i < n_act)
    def _():
        wait(i)
        xb = xbuf[i % 2].astype(BF16)
        xbuf[GATE_STAGE, :, 0:D_EXPERT] = jnp.dot(xb, wg_ref[0], preferred_element_type=F32)
        for r in range(SLOT_BLOCK):
            start(i + 1, r)
        up = jnp.dot(xb, wu_ref[0], preferred_element_type=F32)
        gate = xbuf[GATE_STAGE, :, 0:D_EXPERT]
        hdn = (gate * jax.nn.sigmoid(gate) * up).astype(BF16)
        y_ref[...] = jnp.dot(hdn, wd_ref[0], preferred_element_type=F32)

    @pl.when(i >= n_act)
    def _():
        @pl.when(i == n_act)
        def _():
            wait(i)
        y_ref[...] = jnp.zeros_like(y_ref)


def _expert_call(blk_e, n_act, row_tok, x1, wg, wu, wd, *, name):
    n_blocks = blk_e.shape[0]
    grid_spec = pltpu.PrefetchScalarGridSpec(
        num_scalar_prefetch=3,
        grid=(n_blocks,),
        in_specs=[
            pl.BlockSpec(memory_space=pl.ANY),
            pl.BlockSpec((1, D_MODEL, D_EXPERT), lambda i, be, na, tok: (be[i], 0, 0)),
            pl.BlockSpec((1, D_MODEL, D_EXPERT), lambda i, be, na, tok: (be[i], 0, 0)),
            pl.BlockSpec((1, D_EXPERT, D_MODEL), lambda i, be, na, tok: (be[i], 0, 0)),
        ],
        out_specs=pl.BlockSpec((SLOT_BLOCK, D_MODEL), lambda i, be, na, tok: (i, 0)),
        scratch_shapes=[pltpu.VMEM((3, SLOT_BLOCK, D_MODEL), F32), pltpu.SemaphoreType.DMA((2,))],
    )
    return pl.pallas_call(
        _expert_kernel,
        grid_spec=grid_spec,
        out_shape=jax.ShapeDtypeStruct((n_blocks * SLOT_BLOCK, D_MODEL), F32),
        compiler_params=_cparams(("arbitrary",)),
        name=name,
    )(blk_e, n_act, row_tok, x1, wg, wu, wd)


def _combine_kernel(dest_ref, x1_ref, rw_ref, g2_ref, b2_ref, yr_ref, o_ref, ybuf, sem, *, tile):
    i = pl.program_id(0)

    def gather(t):
        slot = t % 2

        def body(r, carry):
            for k in range(2):
                _row_copy(yr_ref, dest_ref[2 * (t * tile + r) + k], ybuf.at[slot, k], r, sem.at[slot]).start()
            return carry

        lax.fori_loop(0, tile, body, 0, unroll=4)

    @pl.when(i == 0)
    def _():
        gather(i)

    @pl.when(i + 1 < pl.num_programs(0))
    def _():
        gather(i + 1)

    slot = i % 2
    for k in range(2):
        pltpu.make_async_copy(yr_ref.at[pl.ds(0, tile), :], ybuf.at[slot, k], sem.at[slot]).wait()
    rw = rw_ref[...]
    ff = rw[:, 0:1] * ybuf[slot, 0] + rw[:, 1:2] * ybuf[slot, 1]
    o_ref[...] = _ln_rows(ALPHA * x1_ref[...] + ff, g2_ref[...], b2_ref[...], LN_EPS)


def _combine_call(dest, x1, rw, g2, b2, yr, *, row_off, tile, name):
    rows = rw.shape[0]
    off = row_off // tile
    vec_spec = pl.BlockSpec((1, D_MODEL), lambda i, d: (0, 0))
    grid_spec = pltpu.PrefetchScalarGridSpec(
        num_scalar_prefetch=1,
        grid=(rows // tile,),
        in_specs=[pl.BlockSpec((tile, D_MODEL), lambda i, d: (i + off, 0)),
                  pl.BlockSpec((tile, LANES), lambda i, d: (i, 0)), vec_spec, vec_spec,
                  pl.BlockSpec(memory_space=pl.ANY)],
        out_specs=pl.BlockSpec((tile, D_MODEL), lambda i, d: (i, 0)),
        scratch_shapes=[pltpu.VMEM((2, 2, tile, D_MODEL), F32), pltpu.SemaphoreType.DMA((2,))],
    )
    return pl.pallas_call(
        functools.partial(_combine_kernel, tile=tile),
        grid_spec=grid_spec,
        out_shape=jax.ShapeDtypeStruct((rows, D_MODEL), F32),
        compiler_params=_cparams(("arbitrary",)),
        name=name,
    )(dest.reshape(-1), x1, rw, g2, b2, yr)


def _rope_tables(pos):
    half = HEAD_DIM // 2
    inv = ROPE_THETA ** (-jnp.arange(half, dtype=F32) / half)
    ang = pos.astype(F32)[:, None] * inv[None, :]
    cos, sin = jnp.cos(ang), jnp.sin(ang)
    return jnp.concatenate([cos, cos], axis=1), jnp.concatenate([-sin, sin], axis=1), cos.T, sin.T


def kernel(x_prompt, x_sample, meta_tokens, ln_emb_g, ln_emb_b, w_in, attn_sink, m_gate_bias, m_norm_g, w_br_attn, w_br_mlstm, w_out, ln1_g, ln1_b, w_router_group, b_router_group, w_router_expert, b_router_expert, w_expert_gate, w_expert_up, w_expert_down, ln2_g, ln2_b):
    assert w_in.shape[0] == DEPTH == 1
    row = lambda v: v.reshape(1, -1).astype(F32)
    eg, eb = row(ln_emb_g), row(ln_emb_b)
    g1, b1, g2, b2 = row(ln1_g[0]), row(ln1_b[0]), row(ln2_g[0]), row(ln2_b[0])

    w = w_in[0]
    sec = {}
    off = 0
    for nm, width in (("aq", 2048), ("ak", 512), ("av", 512), ("mq", 1024), ("mk", 1024), ("mv", 2048), ("mo", 2048),
                      ("mg", 32), ("bg", 4096)):
        sec[nm] = w[:, off:off + width]
        off += width
    w_tok = jnp.concatenate([sec[n] for n in ("bg", "mk", "ak", "av")], axis=1).astype(BF16)
    w_feat_t = jnp.concatenate([sec[n] for n in ("mv", "mo", "aq", "mq")], axis=1).T.astype(BF16)
    w_gate_t = jnp.pad(sec["mg"], ((0, 0), (0, LANES - 4 * M_HEADS))).T.astype(BF16)
    wa, wb, wo = w_br_attn[0].astype(BF16), w_br_mlstm[0].astype(BF16), w_out[0].astype(BF16)
    wr = jnp.pad(jnp.concatenate([w_router_group[0], w_router_expert[0]], axis=1),
                 ((0, 0), (0, LANES - N_GROUPS - N_EXPERTS)))
    wr_hi = wr.astype(BF16)
    wr_lo = (wr - wr_hi.astype(F32)).astype(BF16)
    rb = jnp.pad(jnp.concatenate([b_router_group[0], b_router_expert[0]]), (0, LANES - N_GROUPS - N_EXPERTS)).reshape(1, LANES)
    weg, weu, wed = w_expert_gate[0].astype(BF16), w_expert_up[0].astype(BF16), w_expert_down[0].astype(BF16)
    gate_bias = m_gate_bias[0].reshape(4 * M_HEADS, 1).astype(F32)
    norm_g_b = jnp.broadcast_to(m_norm_g[0].astype(F32)[:, None], (M_HEADS * M_V_DIM, BLOCK))
    sink = attn_sink[0].astype(F32)
    eye = jnp.eye(BLOCK, dtype=BF16)

    xm = jnp.pad(meta_tokens.astype(F32), ((FRONT_PAD, 0), (0, 0)))
    meta = _proj_call(xm, eg, eb, _rope_tables(jnp.arange(BLOCK) - FRONT_PAD), w_tok, w_feat_t, w_gate_t, tm=BLOCK,
                      zero_front=True, name="proj_meta")
    proj_m = meta[0]

    groups = []
    cnt = jnp.zeros((1, LANES), F32)
    total_rows = x_prompt.shape[0] * x_prompt.shape[1] + x_sample.shape[0] * x_sample.shape[1]
    x1 = jnp.zeros((total_rows, D_MODEL), F32)
    row_off = 0
    for gi, xg in enumerate((x_prompt, x_sample)):
        batch, seq, _ = xg.shape
        assert seq % BLOCK == 0
        nb = seq // BLOCK
        rows = batch * seq
        x2 = xg.reshape(rows, D_MODEL)
        tm = _pick_tile(seq, 1024)
        proj, proj_t, gates_t = _proj_call(x2, eg, eb, _rope_tables(jnp.arange(seq) + N_META), w_tok, w_feat_t, w_gate_t,
                                           tm=tm, zero_front=False, name=f"proj_g{gi}")
        attn_t = _attn_call(sink, proj, proj_t, proj_m, batch=batch, nb=nb, name=f"attn_g{gi}")
        h_fwd_t = _mlstm_call(proj, proj_t, gates_t, gate_bias, eye, meta, None,
                              batch=batch, nc=nb, rev=False, name=f"mlstm_fwd_g{gi}")
        mem_t = _mlstm_call(proj, proj_t, gates_t, gate_bias, eye, None, (h_fwd_t, norm_g_b),
                            batch=batch, nc=nb, rev=True, name=f"mlstm_bwd_g{gi}")
        tp = _pick_tile(rows, 512)
        merged = _merge_call(attn_t, mem_t, proj, wa, wb, tm=tp, name=f"merge_g{gi}")
        low = (jnp.arange(tp)[None, :] < jnp.arange(tp)[:, None]).astype(BF16)
        x1, ri, rw, cnt = _post_call(merged, x2, eg, eb, wo, g1, b1, wr_hi, wr_lo, rb, low, cnt, x1, tm=tp,
                                     row_off=row_off, total_rows=total_rows, name=f"post_g{gi}")
        groups.append((xg.shape, row_off, ri[:, :4], rw))
        row_off += rows

    counts = cnt[0, :N_EXPERTS].astype(I32)
    padded = (counts + SLOT_BLOCK - 1) // SLOT_BLOCK * SLOT_BLOCK
    pad_ends = jnp.cumsum(padded)
    pad_starts = pad_ends - padded
    n_blocks = -(-2 * total_rows // SLOT_BLOCK) + N_EXPERTS
    blk_start = jnp.arange(n_blocks, dtype=I32) * SLOT_BLOCK
    blk_e = jnp.minimum(jnp.sum((pad_ends[None, :] <= blk_start[:, None]).astype(I32), axis=1), N_EXPERTS - 1)
    n_act = (pad_ends[-1:] // SLOT_BLOCK).astype(I32)
    ri_all = jnp.concatenate([g[2] for g in groups], axis=0)
    dest = pad_starts[ri_all[:, 0:2]] + ri_all[:, 2:4]
    tok = jnp.broadcast_to(jnp.arange(total_rows, dtype=I32)[:, None], dest.shape)
    row_tok = jnp.zeros((n_blocks * SLOT_BLOCK,), I32).at[dest.reshape(-1)].set(tok.reshape(-1), unique_indices=True)

    yr = _expert_call(blk_e, n_act, row_tok, x1, weg, weu, wed, name="experts")
    outs = []
    for gi, (shape, off, _, rw) in enumerate(groups):
        rows = rw.shape[0]
        out = _combine_call(dest[off:off + rows], x1, rw, g2, b2, yr, row_off=off, tile=_pick_tile(rows, 256),
                            name=f"combine_g{gi}")
        outs.append(out.reshape(shape))
    return tuple(outs)
```

```python
import functools

import jax
import jax.numpy as jnp
from jax import lax
from jax.experimental import pallas as pl
from jax.experimental.pallas import tpu as pltpu

F32 = jnp.float32
BF16 = jnp.bfloat16
I32 = jnp.int32

D_MODEL = 2048
DEPTH = 1
N_META = 16
BLOCK = 128
FRONT_PAD = BLOCK - N_META
N_Q_HEADS = 16
N_KV_HEADS = 4
HEAD_DIM = 128
Q_PER_KV = 4
WINDOW = 128
ROPE_THETA = 10000.0
M_HEADS = 8
M_QK_DIM = 128
M_V_DIM = 256
M_NORM_EPS = 1e-6
N_GROUPS = 4
EXPERTS_PER_GROUP = 8
N_EXPERTS = 32
D_EXPERT = 1024
ALPHA = (2 * DEPTH) ** 0.25
LN_EPS = 1e-5
NEG = -1e30

C_GA, C_GB, C_MK, C_AK, C_AV = 0, 2048, 4096, 5120, 5632
N_TOK = 6144
F_MV, F_MO, F_AQ, F_MQ = 0, 2048, 4096, 6144
N_FEAT = 7168
PROJ_TN = 1024
LANES = 128
SLOT_BLOCK = 256
VMEM_LIMIT = 56 * 1024 * 1024


def _pick_tile(n, pref):
    t = min(pref, n)
    t -= t % BLOCK
    while n % t:
        t -= BLOCK
    return t


def _cparams(sem, vmem=VMEM_LIMIT):
    return pltpu.CompilerParams(dimension_semantics=sem, vmem_limit_bytes=vmem)


def _ln_rows(x, g, b, eps):
    mu = jnp.mean(x, axis=-1, keepdims=True)
    xc = x - mu
    var = jnp.mean(xc * xc, axis=-1, keepdims=True)
    return xc * lax.rsqrt(var + eps) * g + b


def _rope_heads(acc, cos, sin, nheads, scale):
    outs = []
    for h in range(nheads):
        xh = acc[:, h * HEAD_DIM:(h + 1) * HEAD_DIM]
        o = xh * cos + pltpu.roll(xh, HEAD_DIM // 2, axis=1) * sin
        outs.append(o * scale)
    return outs


N_TOK_TILES = N_TOK // PROJ_TN
N_FEAT_TILES = N_FEAT // PROJ_TN
_NT = (((1,), (1,)), ((), ()))


def _rope_heads_t(acc, cos_t, sin_t, scale):
    half = HEAD_DIM // 2
    outs = []
    for h in range(acc.shape[0] // HEAD_DIM):
        x1 = acc[h * HEAD_DIM:h * HEAD_DIM + half, :]
        x2 = acc[h * HEAD_DIM + half:(h + 1) * HEAD_DIM, :]
        outs += [(x1 * cos_t - x2 * sin_t) * scale, (x2 * cos_t + x1 * sin_t) * scale]
    return jnp.concatenate(outs, axis=0)


def _proj_kernel(x_ref, g_ref, b_ref, cos_ref, sin_ref, cost_ref, sint_ref, w_ref, wt_ref, wg_ref,
                 out_ref, outt_ref, gate_ref, u_scr, *, zero_front):
    j = pl.program_id(1)

    @pl.when(j == 0)
    def _():
        u = _ln_rows(x_ref[...], g_ref[...], b_ref[...], LN_EPS)
        if zero_front:
            row = lax.broadcasted_iota(I32, u.shape, 0)
            u = jnp.where(row >= FRONT_PAD, u, 0.0)
        ub = u.astype(BF16)
        u_scr[...] = ub
        gate_ref[...] = lax.dot_general(wg_ref[...], ub, _NT, preferred_element_type=F32)

    @pl.when(j < N_TOK_TILES)
    def _():
        acc = jnp.dot(u_scr[...], w_ref[...], preferred_element_type=F32)

        @pl.when(j == C_AK // PROJ_TN)
        def _():
            heads = _rope_heads(acc, cos_ref[...], sin_ref[...], N_KV_HEADS, 1.0)
            out_ref[...] = jnp.concatenate(heads + [acc[:, N_KV_HEADS * HEAD_DIM:]], axis=1).astype(BF16)

        @pl.when(j != C_AK // PROJ_TN)
        def _():
            out_ref[...] = acc.astype(BF16)

    @pl.when(j >= N_TOK_TILES)
    def _():
        acc = lax.dot_general(wt_ref[...], u_scr[...], _NT, preferred_element_type=F32)
        jt = j - N_TOK_TILES
        is_aq = (jt >= F_AQ // PROJ_TN) & (jt < F_MQ // PROJ_TN)

        @pl.when(is_aq)
        def _():
            outt_ref[...] = _rope_heads_t(acc, cost_ref[...], sint_ref[...], HEAD_DIM ** -0.5).astype(BF16)

        @pl.when(jnp.logical_not(is_aq))
        def _():
            scale = jnp.where(jt == F_MQ // PROJ_TN, M_QK_DIM ** -0.5, 1.0)
            outt_ref[...] = (acc * scale).astype(BF16)


def _proj_call(x, g, b, rope, w_tok, w_feat_t, w_gate_t, *, tm, zero_front, name):
    rows = x.shape[0]
    cos, sin, cos_t, sin_t = rope
    seq = cos.shape[0]
    pos_tiles = seq // tm
    grid = (rows // tm, N_TOK_TILES + N_FEAT_TILES)
    tok_tile = lambda j: jnp.minimum(j, N_TOK_TILES - 1)
    feat_tile = lambda j: jnp.clip(j - N_TOK_TILES, 0, N_FEAT_TILES - 1)
    return pl.pallas_call(
        functools.partial(_proj_kernel, zero_front=zero_front),
        grid=grid,
        in_specs=[
            pl.BlockSpec((tm, D_MODEL), lambda i, j: (i, 0)),
            pl.BlockSpec((1, D_MODEL), lambda i, j: (0, 0)),
            pl.BlockSpec((1, D_MODEL), lambda i, j: (0, 0)),
            pl.BlockSpec((tm, HEAD_DIM), lambda i, j: (i % pos_tiles, 0)),
            pl.BlockSpec((tm, HEAD_DIM), lambda i, j: (i % pos_tiles, 0)),
            pl.BlockSpec((HEAD_DIM // 2, tm), lambda i, j: (0, i % pos_tiles)),
            pl.BlockSpec((HEAD_DIM // 2, tm), lambda i, j: (0, i % pos_tiles)),
            pl.BlockSpec((D_MODEL, PROJ_TN), lambda i, j: (0, tok_tile(j))),
            pl.BlockSpec((PROJ_TN, D_MODEL), lambda i, j: (feat_tile(j), 0)),
            pl.BlockSpec((LANES, D_MODEL), lambda i, j: (0, 0)),
        ],
        out_specs=[
            pl.BlockSpec((tm, PROJ_TN), lambda i, j: (i, tok_tile(j))),
            pl.BlockSpec((None, PROJ_TN, tm), lambda i, j: (i // pos_tiles, feat_tile(j), i % pos_tiles)),
            pl.BlockSpec((None, LANES, tm), lambda i, j: (i // pos_tiles, 0, i % pos_tiles)),
        ],
        out_shape=[jax.ShapeDtypeStruct((rows, N_TOK), BF16), jax.ShapeDtypeStruct((rows // seq, N_FEAT, seq), BF16),
                   jax.ShapeDtypeStruct((rows // seq, LANES, seq), F32)],
        scratch_shapes=[pltpu.VMEM((tm, D_MODEL), BF16)],
        compiler_params=_cparams(("arbitrary", "arbitrary")),
        name=name,
    )(x, g, b, cos, sin, cos_t, sin_t, w_tok, w_feat_t, w_gate_t)


def _attn_kernel(sink_ref, qt_ref, kp_ref, kc_ref, kn_ref, vp_ref, vc_ref, vn_ref, km_ref, vm_ref, ot_ref,
                 bias_scr, s_scr, p_scr, *, nb):
    i = pl.program_id(1)
    key = lax.broadcasted_iota(I32, (4 * BLOCK, BLOCK), 0)
    qpos = lax.broadcasted_iota(I32, (4 * BLOCK, BLOCK), 1)
    lo = jnp.where(i > 0, 0, BLOCK)
    hi = jnp.where(i < nb - 1, 3 * BLOCK, 2 * BLOCK)
    band = (jnp.abs(key - BLOCK - qpos) <= WINDOW) & (key >= lo) & (key < hi)
    bias_scr[...] = jnp.where(band | (key >= 3 * BLOCK + FRONT_PAD), 0.0, NEG)
    key_blocks = [slice(c * BLOCK, (c + 1) * BLOCK) for c in range(4)]

    def scores(c, cols):
        s = s_scr[key_blocks[c], cols]
        return s if c == 1 else s + bias_scr[key_blocks[c], :]

    for kv in range(N_KV_HEADS):
        h0 = kv * Q_PER_KV
        ks = slice(kv * HEAD_DIM, (kv + 1) * HEAD_DIM)
        k = jnp.concatenate([kp_ref[:, ks], kc_ref[:, ks], kn_ref[:, ks], km_ref[:, ks]], axis=0)
        v = jnp.concatenate([vp_ref[:, ks], vc_ref[:, ks], vn_ref[:, ks], vm_ref[:, ks]], axis=0)
        qt = jnp.concatenate([qt_ref[(h0 + g) * HEAD_DIM:(h0 + g + 1) * HEAD_DIM, :] for g in range(Q_PER_KV)], axis=1)
        s_scr[...] = jnp.dot(k, qt, preferred_element_type=F32)
        inv = []
        for g in range(Q_PER_KV):
            cols = slice(g * BLOCK, (g + 1) * BLOCK)
            sink = sink_ref[h0 + g]
            m = jnp.full((1, BLOCK), sink, F32)
            for c in range(4):
                m = jnp.maximum(m, jnp.max(scores(c, cols), axis=0, keepdims=True))
            den = jnp.exp(sink - m)
            for c in range(4):
                p = jnp.exp(scores(c, cols) - m)
                den = den + jnp.sum(p, axis=0, keepdims=True)
                p_scr[key_blocks[c], cols] = p.astype(BF16)
            inv.append(1.0 / den)
        ot = lax.dot_general(v, p_scr[...], (((0,), (0,)), ((), ())), preferred_element_type=F32)
        for g in range(Q_PER_KV):
            ot_ref[(h0 + g) * HEAD_DIM:(h0 + g + 1) * HEAD_DIM, :] = (ot[:, g * BLOCK:(g + 1) * BLOCK] * inv[g]).astype(BF16)


def _attn_call(sink, proj, proj_t, proj_meta, *, batch, nb, name):
    rows = proj.shape[0]
    kw = N_KV_HEADS * HEAD_DIM
    qw = N_Q_HEADS * HEAD_DIM
    kcol, vcol = C_AK // kw, C_AV // kw

    def band_spec(off, colblk):
        def imap(b, i, sink_ref):
            return (b * nb + jnp.clip(i + off, 0, nb - 1), colblk)
        return pl.BlockSpec((BLOCK, kw), imap)

    grid_spec = pltpu.PrefetchScalarGridSpec(
        num_scalar_prefetch=1,
        grid=(batch, nb),
        in_specs=[
            pl.BlockSpec((None, qw, BLOCK), lambda b, i, s: (b, F_AQ // qw, i)),
            band_spec(-1, kcol), band_spec(0, kcol), band_spec(1, kcol),
            band_spec(-1, vcol), band_spec(0, vcol), band_spec(1, vcol),
            pl.BlockSpec((BLOCK, kw), lambda b, i, s: (0, kcol)),
            pl.BlockSpec((BLOCK, kw), lambda b, i, s: (0, vcol)),
        ],
        out_specs=pl.BlockSpec((None, qw, BLOCK), lambda b, i, s: (b, 0, i)),
        scratch_shapes=[pltpu.VMEM((4 * BLOCK, BLOCK), F32), pltpu.VMEM((4 * BLOCK, Q_PER_KV * BLOCK), F32),
                        pltpu.VMEM((4 * BLOCK, Q_PER_KV * BLOCK), BF16)],
    )
    return pl.pallas_call(
        functools.partial(_attn_kernel, nb=nb),
        grid_spec=grid_spec,
        out_shape=jax.ShapeDtypeStruct((batch, qw, rows // batch), BF16),
        compiler_params=_cparams(("arbitrary", "arbitrary")),
        name=name,
    )(sink, proj_t, proj, proj, proj, proj, proj, proj, proj_meta, proj_meta)


MLSTM_HEAD_GROUP = 4


def _scan_lanes(x, op, fill, rev):
    lane = lax.broadcasted_iota(I32, x.shape, 1)
    sh = 1
    while sh < BLOCK:
        if rev:
            y = jnp.where(lane < BLOCK - sh, pltpu.roll(x, BLOCK - sh, axis=1), fill)
        else:
            y = jnp.where(lane >= sh, pltpu.roll(x, sh, axis=1), fill)
        x = op(x, y)
        sh *= 2
    return x


def _rows_to_columns(x, eye):
    hi = x.astype(BF16)
    r1 = x - hi.astype(F32)
    mid = r1.astype(BF16)
    lo = (r1 - mid.astype(F32)).astype(BF16)
    move = lambda a: lax.dot_general(eye, a, _NT, preferred_element_type=F32)
    return move(hi) + move(mid) + move(lo)


def _chunk_terms(gts, bias, m_prev, rev, pad_front):
    d = M_HEADS if rev else 0
    li = jnp.concatenate([gt[d:d + M_HEADS, :] + bias[d:d + M_HEADS, :] for gt in gts], axis=0)
    lf = jnp.concatenate([gt[2 * M_HEADS + d:3 * M_HEADS + d, :] + bias[2 * M_HEADS + d:3 * M_HEADS + d, :] for gt in gts],
                         axis=0)
    lf = jax.nn.log_sigmoid(lf)
    if pad_front:
        real = lax.broadcasted_iota(I32, li.shape, 1) >= FRONT_PAD
        li, lf = jnp.where(real, li, NEG), jnp.where(real, lf, 0.0)
    b = _scan_lanes(lf, jnp.add, 0.0, rev)
    tot = jnp.sum(lf, axis=1, keepdims=True)
    u = li - b
    big_m = jnp.maximum(m_prev, _scan_lanes(u, jnp.maximum, -jnp.inf, rev))
    inter = jnp.exp(m_prev - big_m)
    e_neg_m = jnp.exp(-(b + big_m))
    m_new = jnp.maximum(tot + m_prev, tot + jnp.max(u, axis=1, keepdims=True))
    w = jnp.exp(tot + u - m_new)
    sp = jnp.exp(tot + m_prev - m_new)
    return u, big_m, inter, e_neg_m, w, sp, m_new


def _state_update(h, k, vt, w, sp, ct_scr, n_scr):
    wh = w[h:h + 1, :]
    lhs = jnp.concatenate([(vt.astype(F32) * wh).astype(BF16), jnp.broadcast_to(wh, (16, BLOCK)).astype(BF16)], axis=0)
    upd = jnp.dot(lhs, k, preferred_element_type=F32)
    sph = sp[h:h + 1, :]
    ct_scr[h] = sph * ct_scr[h] + upd[:M_V_DIM]
    n_scr[h:h + 1, :] = sph * n_scr[h:h + 1, :] + upd[M_V_DIM:M_V_DIM + 1]


def _mlstm_kernel(*refs, rev, final):
    (qt_ref, k_ref, vt_ref, gt_ref, gtn_ref, bias_ref, eye_ref), refs = refs[:7], refs[7:]
    if not rev:
        (km_ref, vtm_ref, gtm_ref), refs = refs[:3], refs[3:]
    if final:
        (hft_ref, mot_ref, ngb_ref), refs = refs[:3], refs[3:]
    ht_ref, ct_scr, n_scr, t_scr, ucol_scr = refs
    bias = bias_ref[...]
    eye = eye_ref[...]
    n_seq = qt_ref.shape[0]
    n_heads = n_seq * M_HEADS

    def stash_terms(g_ref, m_prev):
        gts = [g_ref[b, 0:4 * M_HEADS, :] for b in range(n_seq)]
        u, big_m, inter, e_neg_m, w, sp, m_new = _chunk_terms(gts, bias, m_prev, rev, False)
        ucol_scr[...] = _rows_to_columns(u, eye)
        for slot, val in enumerate((big_m, inter, e_neg_m, w, sp, m_new)):
            t_scr[slot] = jnp.broadcast_to(val, (n_heads, LANES))

    @pl.when(pl.program_id(0) == 0)
    def _():
        ct_scr[...] = jnp.zeros_like(ct_scr)
        n_scr[...] = jnp.zeros_like(n_scr)
        m0 = jnp.zeros((n_heads, 1), F32)
        if not rev:
            _, _, _, _, w, sp, m0 = _chunk_terms([gtm_ref[0, 0:4 * M_HEADS, :]] * n_seq, bias, m0, rev, True)
            for hh in range(n_heads):
                h = hh % M_HEADS
                _state_update(hh, km_ref[:, h * M_QK_DIM:(h + 1) * M_QK_DIM],
                              vtm_ref[0, h * M_V_DIM:(h + 1) * M_V_DIM, :], w, sp, ct_scr, n_scr)
        stash_terms(gt_ref, m0)

    big_m, inter, e_neg_m, w = t_scr[0], t_scr[1], t_scr[2], t_scr[3]
    sp, m_new = t_scr[4][:, 0:1], t_scr[5][:, 0:1]
    u_col = ucol_scr[...]
    s_idx = lax.broadcasted_iota(I32, (BLOCK, BLOCK), 0)
    t_idx = lax.broadcasted_iota(I32, (BLOCK, BLOCK), 1)
    causal = (s_idx >= t_idx) if rev else (s_idx <= t_idx)
    qk_cols = lambda h: slice((h % M_HEADS) * M_QK_DIM, (h % M_HEADS + 1) * M_QK_DIM)
    v_rows = lambda h: slice((h % M_HEADS) * M_V_DIM, (h % M_HEADS + 1) * M_V_DIM)
    qt_of = lambda h: qt_ref[h // M_HEADS, qk_cols(h), :]
    k_of = lambda h: k_ref[h // M_HEADS, :, qk_cols(h)]
    vt_of = lambda h: vt_ref[h // M_HEADS, v_rows(h), :]
    for g0 in range(0, n_heads, MLSTM_HEAD_GROUP):
        hs = range(g0, g0 + MLSTM_HEAD_GROUP)
        st = {h: jnp.dot(jnp.concatenate([k_of(h), jnp.broadcast_to(n_scr[h:h + 1, :], (16, M_QK_DIM)).astype(BF16)], axis=0),
                         qt_of(h), preferred_element_type=F32) for h in hs}
        decay = {h: jnp.exp(jnp.where(causal, u_col[:, h:h + 1] - big_m[h:h + 1, :], NEG)) for h in hs}
        sd = {h: st[h][:BLOCK] * decay[h] for h in hs}
        rden = {}
        for h in hs:
            den = inter[h:h + 1, :] * st[h][BLOCK:BLOCK + 1] + jnp.sum(sd[h], axis=0, keepdims=True)
            rden[h] = 1.0 / jnp.maximum(jnp.abs(den), e_neg_m[h:h + 1, :])
        hv = {}
        for h in hs:
            lhs = jnp.concatenate([vt_of(h), ct_scr[h].astype(BF16)], axis=1)
            rhs = jnp.concatenate([sd[h].astype(BF16), (qt_of(h).astype(F32) * inter[h:h + 1, :]).astype(BF16)], axis=0)
            hv[h] = jnp.dot(lhs, rhs, preferred_element_type=F32) * rden[h]
        for h in hs:
            b, rows = h // M_HEADS, v_rows(h)
            x = hv[h]
            if final:
                x = x + hft_ref[b, rows, :].astype(F32)
                mu = jnp.mean(x, axis=0, keepdims=True)
                xc = x - mu
                var = jnp.mean(xc * xc, axis=0, keepdims=True)
                x = xc * lax.rsqrt(var + M_NORM_EPS) * ngb_ref[rows, :] * jax.nn.sigmoid(mot_ref[b, rows, :].astype(F32))
            ht_ref[b, rows, :] = x.astype(ht_ref.dtype)
        for h in hs:
            _state_update(h, k_of(h), vt_of(h), w, sp, ct_scr, n_scr)
    stash_terms(gtn_ref, m_new)


def _mlstm_call(proj, proj_t, gates_t, bias, eye, meta, final_in, *, batch, nc, rev, name):
    final = final_in is not None
    qw, vw = M_HEADS * M_QK_DIM, M_HEADS * M_V_DIM
    seq = nc * BLOCK
    proj3 = proj.reshape(batch, seq, N_TOK)
    chunk = (lambda i: nc - 1 - i) if rev else (lambda i: i)
    feat_spec = lambda width, off: pl.BlockSpec((batch, width, BLOCK), lambda i: (0, off // width, chunk(i)))

    in_specs = [
        feat_spec(qw, F_MQ),
        pl.BlockSpec((batch, BLOCK, qw), lambda i: (0, chunk(i), C_MK // qw)),
        feat_spec(vw, F_MV),
        pl.BlockSpec((batch, LANES, BLOCK), lambda i: (0, 0, chunk(i))),
        pl.BlockSpec((batch, LANES, BLOCK), lambda i: (0, 0, chunk(jnp.minimum(i + 1, nc - 1)))),
        pl.BlockSpec((4 * M_HEADS, 1), lambda i: (0, 0)),
        pl.BlockSpec((BLOCK, BLOCK), lambda i: (0, 0)),
    ]
    args = [proj_t, proj3, proj_t, gates_t, gates_t, bias, eye]
    if not rev:
        proj_m, proj_tm, gates_tm = meta
        in_specs += [
            pl.BlockSpec((BLOCK, qw), lambda i: (0, C_MK // qw)),
            pl.BlockSpec((1, vw, BLOCK), lambda i: (0, F_MV // vw, 0)),
            pl.BlockSpec((1, LANES, BLOCK), lambda i: (0, 0, 0)),
        ]
        args += [proj_m, proj_tm, gates_tm]
    if final:
        h_fwd_t, norm_g_b = final_in
        in_specs += [feat_spec(vw, 0), feat_spec(vw, F_MO), pl.BlockSpec((vw, BLOCK), lambda i: (0, 0))]
        args += [h_fwd_t, proj_t, norm_g_b]
    n_heads = batch * M_HEADS
    return pl.pallas_call(
        functools.partial(_mlstm_kernel, rev=rev, final=final),
        grid=(nc,),
        in_specs=in_specs,
        out_specs=feat_spec(vw, 0),
        out_shape=jax.ShapeDtypeStruct((batch, vw, seq), BF16),
        scratch_shapes=[pltpu.VMEM((n_heads, M_V_DIM, M_QK_DIM), F32), pltpu.VMEM((n_heads, LANES), F32),
                        pltpu.VMEM((6, n_heads, LANES), F32), pltpu.VMEM((BLOCK, n_heads), F32)],
        compiler_params=_cparams(("arbitrary",)),
        name=name,
    )(*args)


def _merge_kernel(at_ref, mt_ref, ga_ref, gb_ref, wa_ref, wb_ref, o_ref):
    tn = (((0,), (0,)), ((), ()))
    ya = lax.dot_general(at_ref[...], wa_ref[...], tn, preferred_element_type=F32)
    yb = lax.dot_general(mt_ref[...], wb_ref[...], tn, preferred_element_type=F32)
    o_ref[...] = (jax.nn.sigmoid(ga_ref[...].astype(F32)) * ya + jax.nn.sigmoid(gb_ref[...].astype(F32)) * yb).astype(BF16)


def _merge_call(attn_t, mem_t, proj, wa, wb, *, tm, name):
    rows = proj.shape[0]
    row_spec = pl.BlockSpec((tm, D_MODEL), lambda i: (i, 0))
    tiles_per_seq = attn_t.shape[2] // tm
    feat_spec = pl.BlockSpec((None, D_MODEL, tm), lambda i: (i // tiles_per_seq, 0, i % tiles_per_seq))
    w_spec = pl.BlockSpec((D_MODEL, D_MODEL), lambda i: (0, 0), pipeline_mode=pl.Buffered(1))
    return pl.pallas_call(
        _merge_kernel,
        grid=(rows // tm,),
        in_specs=[feat_spec, feat_spec,
                  pl.BlockSpec((tm, D_MODEL), lambda i: (i, C_GA // D_MODEL)),
                  pl.BlockSpec((tm, D_MODEL), lambda i: (i, C_GB // D_MODEL)),
                  w_spec, w_spec],
        out_specs=row_spec,
        out_shape=jax.ShapeDtypeStruct((rows, D_MODEL), BF16),
        compiler_params=_cparams(("arbitrary",)),
        name=name,
    )(attn_t, mem_t, proj, proj, wa, wb)


def _post_kernel(mg_ref, x_ref, eg_ref, eb_ref, wo_ref, g1_ref, b1_ref, wr_hi_ref, wr_lo_ref, rb_ref, low_ref, cnt_in_ref,
                 x1_joint_ref, x1_ref, ri_ref, rw_ref, cnt_ref):
    del x1_joint_ref

    @pl.when(pl.program_id(0) == 0)
    def _():
        cnt_ref[...] = cnt_in_ref[...]

    x0 = _ln_rows(x_ref[...], eg_ref[...], eb_ref[...], LN_EPS)
    y = jnp.dot(mg_ref[...], wo_ref[...], preferred_element_type=F32)
    x1 = _ln_rows(ALPHA * x0 + y, g1_ref[...], b1_ref[...], LN_EPS)
    x1_ref[...] = x1

    hi = x1.astype(BF16)
    lo = (x1 - hi.astype(F32)).astype(BF16)
    logits = (jnp.dot(hi, wr_hi_ref[...], preferred_element_type=F32)
              + jnp.dot(lo, wr_hi_ref[...], preferred_element_type=F32)
              + jnp.dot(hi, wr_lo_ref[...], preferred_element_type=F32)) + rb_ref[...]
    lane = lax.broadcasted_iota(I32, logits.shape, 1)
    big = jnp.int32(LANES)

    def first_max(valid):
        vmax = jnp.max(jnp.where(valid, logits, -jnp.inf), axis=1, keepdims=True)
        idx = jnp.min(jnp.where(valid & (logits == vmax), lane, big), axis=1, keepdims=True)
        return vmax, idx

    is_grp = lane < N_GROUPS
    gmax, grp = first_max(is_grp)
    p_grp = 1.0 / jnp.sum(jnp.where(is_grp, jnp.exp(logits - gmax), 0.0), axis=1, keepdims=True)
    e_lo = N_GROUPS + grp * EXPERTS_PER_GROUP
    in_grp = (lane >= e_lo) & (lane < e_lo + EXPERTS_PER_GROUP)
    l1, i1 = first_max(in_grp)
    l2, i2 = first_max(in_grp & (lane != i1))
    t2 = jnp.exp(l2 - l1)
    w1 = p_grp / (1.0 + t2)
    w2 = p_grp * t2 / (1.0 + t2)
    e1 = i1 - N_GROUPS
    e2 = i2 - N_GROUPS

    oh1 = (lane == e1).astype(F32)
    oh2 = (lane == e2).astype(F32)
    cnt = oh1 + oh2
    before = jnp.dot(low_ref[...], cnt.astype(BF16), preferred_element_type=F32) + cnt_ref[...]
    r1 = jnp.sum(oh1 * before, axis=1, keepdims=True).astype(I32)
    r2 = jnp.sum(oh2 * before, axis=1, keepdims=True).astype(I32)
    cnt_ref[...] = cnt_ref[...] + jnp.sum(cnt, axis=0, keepdims=True)
    ri_ref[...] = jnp.where(lane == 0, e1, jnp.where(lane == 1, e2, jnp.where(lane == 2, r1, jnp.where(lane == 3, r2, 0))))
    rw_ref[...] = jnp.where(lane == 0, w1, jnp.where(lane == 1, w2, 0.0))


def _post_call(merged, x, eg, eb, wo, g1, b1, wr_hi, wr_lo, rb, low, cnt_in, x1_joint, *, tm, row_off, total_rows, name):
    rows = merged.shape[0]
    off = row_off // tm
    row_spec = pl.BlockSpec((tm, D_MODEL), lambda i: (i, 0))
    vec_spec = pl.BlockSpec((1, D_MODEL), lambda i: (0, 0))
    lane_spec = pl.BlockSpec((1, LANES), lambda i: (0, 0))
    rt_spec = pl.BlockSpec((D_MODEL, LANES), lambda i: (0, 0))
    assert row_off % tm == 0 and x1_joint.shape == (total_rows, D_MODEL)
    return pl.pallas_call(
        _post_kernel,
        grid=(rows // tm,),
        in_specs=[row_spec, row_spec, vec_spec, vec_spec,
                  pl.BlockSpec((D_MODEL, D_MODEL), lambda i: (0, 0), pipeline_mode=pl.Buffered(1)),
                  vec_spec, vec_spec, rt_spec, rt_spec, lane_spec,
                  pl.BlockSpec((tm, tm), lambda i: (0, 0)), lane_spec, pl.BlockSpec(memory_space=pl.ANY)],
        out_specs=[pl.BlockSpec((tm, D_MODEL), lambda i: (i + off, 0)),
                   pl.BlockSpec((tm, LANES), lambda i: (i, 0)), pl.BlockSpec((tm, LANES), lambda i: (i, 0)),
                   lane_spec],
        out_shape=[jax.ShapeDtypeStruct((total_rows, D_MODEL), F32), jax.ShapeDtypeStruct((rows, LANES), I32),
                   jax.ShapeDtypeStruct((rows, LANES), F32), jax.ShapeDtypeStruct((1, LANES), F32)],
        input_output_aliases={12: 0},
        compiler_params=_cparams(("arbitrary",)),
        name=name,
    )(merged, x, eg, eb, wo, g1, b1, wr_hi, wr_lo, rb, low, cnt_in, x1_joint)


def _row_copy(src, s, dst, d, sem):
    return pltpu.make_async_copy(src.at[pl.ds(s, 1), :], dst.at[pl.ds(d, 1), :], sem)


GATE_STAGE = 2


def _expert_kernel(blk_e_ref, n_act_ref, tok_ref, x_ref, wg_ref, wu_ref, wd_ref, y_ref, xbuf, sem):
    del blk_e_ref
    i = pl.program_id(0)
    n_act = n_act_ref[0]

    def start(blk, r):
        _row_copy(x_ref, tok_ref[blk * SLOT_BLOCK + r], xbuf.at[blk % 2], r, sem.at[blk % 2]).start()

    def wait(blk):
        pltpu.make_async_copy(x_ref.at[pl.ds(0, SLOT_BLOCK), :], xbuf.at[blk % 2], sem.at[blk % 2]).wait()

    @pl.when(i == 0)
    def _():
        lax.fori_loop(0, SLOT_BLOCK, lambda r, c: (start(i, r), c)[1], 0, unroll=8)

    @pl.when(i < n_act)
    def _():
        wait(i)
        xb = xbuf[i % 2].astype(BF16)
        half = D_EXPERT // 2
        xbuf[GATE_STAGE, :, 0:half] = jnp.dot(xb, wg_ref[0, :, 0:half], preferred_element_type=F32)
        for r in range(SLOT_BLOCK):
            start(i + 1, r)
        gate_hi = jnp.dot(xb, wg_ref[0, :, half:], preferred_element_type=F32)
        up = jnp.dot(xb, wu_ref[0], preferred_element_type=F32)
        gate = jnp.concatenate([xbuf[GATE_STAGE, :, 0:half], gate_hi], axis=1)
        hdn = (gate * jax.nn.sigmoid(gate) * up).astype(BF16)
        y_ref[...] = jnp.dot(hdn, wd_ref[0], preferred_element_type=F32)

    @pl.when(i >= n_act)
    def _():
        @pl.when(i == n_act)
        def _():
            wait(i)
        y_ref[...] = jnp.zeros_like(y_ref)


def _expert_call(blk_e, n_act, row_tok, x1, wg, wu, wd, *, name):
    n_blocks = blk_e.shape[0]
    grid_spec = pltpu.PrefetchScalarGridSpec(
        num_scalar_prefetch=3,
        grid=(n_blocks,),
        in_specs=[
            pl.BlockSpec(memory_space=pl.ANY),
            pl.BlockSpec((1, D_MODEL, D_EXPERT), lambda i, be, na, tok: (be[i], 0, 0)),
            pl.BlockSpec((1, D_MODEL, D_EXPERT), lambda i, be, na, tok: (be[i], 0, 0)),
            pl.BlockSpec((1, D_EXPERT, D_MODEL), lambda i, be, na, tok: (be[i], 0, 0)),
        ],
        out_specs=pl.BlockSpec((SLOT_BLOCK, D_MODEL), lambda i, be, na, tok: (i, 0)),
        scratch_shapes=[pltpu.VMEM((3, SLOT_BLOCK, D_MODEL), F32), pltpu.SemaphoreType.DMA((2,))],
    )
    return pl.pallas_call(
        _expert_kernel,
        grid_spec=grid_spec,
        out_shape=jax.ShapeDtypeStruct((n_blocks * SLOT_BLOCK, D_MODEL), F32),
        compiler_params=_cparams(("arbitrary",)),
        name=name,
    )(blk_e, n_act, row_tok, x1, wg, wu, wd)


COMBINE_STAGE = 4


def _combine_kernel(dest_ref, x1_ref, rw_ref, g2_ref, b2_ref, yr_ref, o_ref, ybuf, sem, *, tile):
    i = pl.program_id(0)
    last = pl.num_programs(0) - 1

    def start(t, par, r, k):
        _row_copy(yr_ref, dest_ref[2 * (t * tile + r) + k], ybuf.at[2 * par + k], r, sem.at[par]).start()

    def wait(par):
        for k in range(2):
            pltpu.make_async_copy(yr_ref.at[pl.ds(0, tile), :], ybuf.at[2 * par + k], sem.at[par]).wait()

    @pl.when(i == 0)
    def _():
        lax.fori_loop(0, tile, lambda r, c: (start(i, 0, r, 0), start(i, 0, r, 1), c)[2], 0, unroll=4)

    par = i % 2
    wait(par)
    rw = rw_ref[...]
    s = ALPHA * x1_ref[...] + rw[:, 0:1] * ybuf[2 * par] + rw[:, 1:2] * ybuf[2 * par + 1]
    ybuf[COMBINE_STAGE, 0:8, :] = jnp.broadcast_to(b2_ref[...], (8, D_MODEL))
    nxt = jnp.minimum(i + 1, last)
    for r in range(tile):
        for k in range(2):
            start(nxt, 1 - par, r, k)
    o_ref[...] = _ln_rows(s, g2_ref[...], ybuf[COMBINE_STAGE, 0:1, :], LN_EPS)

    @pl.when(i == last)
    def _():
        wait(1 - par)


def _combine_call(dest, x1, rw, g2, b2, yr, *, row_off, tile, name):
    rows = rw.shape[0]
    off = row_off // tile
    vec_spec = pl.BlockSpec((1, D_MODEL), lambda i, d: (0, 0))
    grid_spec = pltpu.PrefetchScalarGridSpec(
        num_scalar_prefetch=1,
        grid=(rows // tile,),
        in_specs=[pl.BlockSpec((tile, D_MODEL), lambda i, d: (i + off, 0)),
                  pl.BlockSpec((tile, LANES), lambda i, d: (i, 0)), vec_spec, vec_spec,
                  pl.BlockSpec(memory_space=pl.ANY)],
        out_specs=pl.BlockSpec((tile, D_MODEL), lambda i, d: (i, 0)),
        scratch_shapes=[pltpu.VMEM((5, tile, D_MODEL), F32), pltpu.SemaphoreType.DMA((2,))],
    )
    return pl.pallas_call(
        functools.partial(_combine_kernel, tile=tile),
        grid_spec=grid_spec,
        out_shape=jax.ShapeDtypeStruct((rows, D_MODEL), F32),
        compiler_params=_cparams(("arbitrary",)),
        name=name,
    )(dest.reshape(-1), x1, rw, g2, b2, yr)


def _rope_tables(pos):
    half = HEAD_DIM // 2
    inv = ROPE_THETA ** (-jnp.arange(half, dtype=F32) / half)
    ang = pos.astype(F32)[:, None] * inv[None, :]
    cos, sin = jnp.cos(ang), jnp.sin(ang)
    return jnp.concatenate([cos, cos], axis=1), jnp.concatenate([-sin, sin], axis=1), cos.T, sin.T


def kernel(x_prompt, x_sample, meta_tokens, ln_emb_g, ln_emb_b, w_in, attn_sink, m_gate_bias, m_norm_g, w_br_attn, w_br_mlstm, w_out, ln1_g, ln1_b, w_router_group, b_router_group, w_router_expert, b_router_expert, w_expert_gate, w_expert_up, w_expert_down, ln2_g, ln2_b):
    assert w_in.shape[0] == DEPTH == 1
    row = lambda v: v.reshape(1, -1).astype(F32)
    eg, eb = row(ln_emb_g), row(ln_emb_b)
    g1, b1, g2, b2 = row(ln1_g[0]), row(ln1_b[0]), row(ln2_g[0]), row(ln2_b[0])

    w = w_in[0]
    sec = {}
    off = 0
    for nm, width in (("aq", 2048), ("ak", 512), ("av", 512), ("mq", 1024), ("mk", 1024), ("mv", 2048), ("mo", 2048),
                      ("mg", 32), ("bg", 4096)):
        sec[nm] = w[:, off:off + width]
        off += width
    w_tok = jnp.concatenate([sec[n] for n in ("bg", "mk", "ak", "av")], axis=1).astype(BF16)
    w_feat_t = jnp.concatenate([sec[n] for n in ("mv", "mo", "aq", "mq")], axis=1).T.astype(BF16)
    w_gate_t = jnp.pad(sec["mg"], ((0, 0), (0, LANES - 4 * M_HEADS))).T.astype(BF16)
    wa, wb, wo = w_br_attn[0].astype(BF16), w_br_mlstm[0].astype(BF16), w_out[0].astype(BF16)
    wr = jnp.pad(jnp.concatenate([w_router_group[0], w_router_expert[0]], axis=1),
                 ((0, 0), (0, LANES - N_GROUPS - N_EXPERTS)))
    wr_hi = wr.astype(BF16)
    wr_lo = (wr - wr_hi.astype(F32)).astype(BF16)
    rb = jnp.pad(jnp.concatenate([b_router_group[0], b_router_expert[0]]), (0, LANES - N_GROUPS - N_EXPERTS)).reshape(1, LANES)
    weg, weu, wed = w_expert_gate[0].astype(BF16), w_expert_up[0].astype(BF16), w_expert_down[0].astype(BF16)
    gate_bias = m_gate_bias[0].reshape(4 * M_HEADS, 1).astype(F32)
    norm_g_b = jnp.broadcast_to(m_norm_g[0].astype(F32)[:, None], (M_HEADS * M_V_DIM, BLOCK))
    sink = attn_sink[0].astype(F32)
    eye = jnp.eye(BLOCK, dtype=BF16)

    xm = jnp.pad(meta_tokens.astype(F32), ((FRONT_PAD, 0), (0, 0)))
    meta = _proj_call(xm, eg, eb, _rope_tables(jnp.arange(BLOCK) - FRONT_PAD), w_tok, w_feat_t, w_gate_t, tm=BLOCK,
                      zero_front=True, name="proj_meta")
    proj_m = meta[0]

    groups = []
    cnt = jnp.zeros((1, LANES), F32)
    total_rows = x_prompt.shape[0] * x_prompt.shape[1] + x_sample.shape[0] * x_sample.shape[1]
    x1 = jnp.zeros((total_rows, D_MODEL), F32)
    row_off = 0
    for gi, xg in enumerate((x_prompt, x_sample)):
        batch, seq, _ = xg.shape
        assert seq % BLOCK == 0
        nb = seq // BLOCK
        rows = batch * seq
        x2 = xg.reshape(rows, D_MODEL)
        tm = _pick_tile(seq, 1024)
        proj, proj_t, gates_t = _proj_call(x2, eg, eb, _rope_tables(jnp.arange(seq) + N_META), w_tok, w_feat_t, w_gate_t,
                                           tm=tm, zero_front=False, name=f"proj_g{gi}")
        attn_t = _attn_call(sink, proj, proj_t, proj_m, batch=batch, nb=nb, name=f"attn_g{gi}")
        h_fwd_t = _mlstm_call(proj, proj_t, gates_t, gate_bias, eye, meta, None,
                              batch=batch, nc=nb, rev=False, name=f"mlstm_fwd_g{gi}")
        mem_t = _mlstm_call(proj, proj_t, gates_t, gate_bias, eye, None, (h_fwd_t, norm_g_b),
                            batch=batch, nc=nb, rev=True, name=f"mlstm_bwd_g{gi}")
        tp = _pick_tile(seq, 512)
        merged = _merge_call(attn_t, mem_t, proj, wa, wb, tm=tp, name=f"merge_g{gi}")
        low = (jnp.arange(tp)[None, :] < jnp.arange(tp)[:, None]).astype(BF16)
        x1, ri, rw, cnt = _post_call(merged, x2, eg, eb, wo, g1, b1, wr_hi, wr_lo, rb, low, cnt, x1, tm=tp,
                                     row_off=row_off, total_rows=total_rows, name=f"post_g{gi}")
        groups.append((xg.shape, row_off, ri[:, :4], rw))
        row_off += rows

    counts = cnt[0, :N_EXPERTS].astype(I32)
    padded = (counts + SLOT_BLOCK - 1) // SLOT_BLOCK * SLOT_BLOCK
    pad_ends = jnp.cumsum(padded)
    pad_starts = pad_ends - padded
    n_blocks = -(-2 * total_rows // SLOT_BLOCK) + N_EXPERTS
    blk_start = jnp.arange(n_blocks, dtype=I32) * SLOT_BLOCK
    blk_e = jnp.minimum(jnp.sum((pad_ends[None, :] <= blk_start[:, None]).astype(I32), axis=1), N_EXPERTS - 1)
    n_act = (pad_ends[-1:] // SLOT_BLOCK).astype(I32)
    ri_all = jnp.concatenate([g[2] for g in groups], axis=0)
    dest = pad_starts[ri_all[:, 0:2]] + ri_all[:, 2:4]
    tok = jnp.broadcast_to(jnp.arange(total_rows, dtype=I32)[:, None], dest.shape)
    row_tok = jnp.zeros((n_blocks * SLOT_BLOCK,), I32).at[dest.reshape(-1)].set(tok.reshape(-1), unique_indices=True)

    yr = _expert_call(blk_e, n_act, row_tok, x1, weg, weu, wed, name="experts")
    outs = []
    for gi, (shape, off, _, rw) in enumerate(groups):
        rows = rw.shape[0]
        out = _combine_call(dest[off:off + rows], x1, rw, g2, b2, yr, row_off=off, tile=_pick_tile(rows, 256),
                            name=f"combine_g{gi}")
        outs.append(out.reshape(shape))
    return tuple(outs)
```

```python
import functools

import jax
import jax.numpy as jnp
from jax import lax
from jax.experimental import pallas as pl
from jax.experimental.pallas import tpu as pltpu

F32 = jnp.float32
BF16 = jnp.bfloat16
I32 = jnp.int32

D_MODEL = 2048
DEPTH = 1
N_META = 16
BLOCK = 128
FRONT_PAD = BLOCK - N_META
N_Q_HEADS = 16
N_KV_HEADS = 4
HEAD_DIM = 128
Q_PER_KV = 4
WINDOW = 128
ROPE_THETA = 10000.0
M_HEADS = 8
M_QK_DIM = 128
M_V_DIM = 256
M_NORM_EPS = 1e-6
N_GROUPS = 4
EXPERTS_PER_GROUP = 8
N_EXPERTS = 32
D_EXPERT = 1024
ALPHA = (2 * DEPTH) ** 0.25
LN_EPS = 1e-5
NEG = -1e30

C_GA, C_GB, C_MK, C_AK, C_AV = 0, 2048, 4096, 5120, 5632
N_TOK = 6144
F_MV, F_MO, F_AQ, F_MQ = 0, 2048, 4096, 6144
N_FEAT = 7168
PROJ_TN = 1024
LANES = 128
SLOT_BLOCK = 256
VMEM_LIMIT = 56 * 1024 * 1024


def _pick_tile(n, pref):
    t = min(pref, n)
    t -= t % BLOCK
    while n % t:
        t -= BLOCK
    return t


def _cparams(sem, vmem=VMEM_LIMIT):
    return pltpu.CompilerParams(dimension_semantics=sem, vmem_limit_bytes=vmem)


def _ln_rows(x, g, b, eps):
    mu = jnp.mean(x, axis=-1, keepdims=True)
    xc = x - mu
    var = jnp.mean(xc * xc, axis=-1, keepdims=True)
    return xc * lax.rsqrt(var + eps) * g + b


def _rope_heads(acc, cos, sin, nheads, scale):
    outs = []
    for h in range(nheads):
        xh = acc[:, h * HEAD_DIM:(h + 1) * HEAD_DIM]
        o = xh * cos + pltpu.roll(xh, HEAD_DIM // 2, axis=1) * sin
        outs.append(o * scale)
    return outs


LOG2_E = 1.4426950408889634
ATTN_Q_SCALE = HEAD_DIM ** -0.5 * LOG2_E
PROJ_SPLIT = 1
N_TOK_TILES = N_TOK // PROJ_TN
N_FEAT_TILES = N_FEAT // PROJ_TN
_NT = (((1,), (1,)), ((), ()))


def _rope_heads_t(acc, cos_t, sin_t, scale):
    half = HEAD_DIM // 2
    outs = []
    for h in range(acc.shape[0] // HEAD_DIM):
        x1 = acc[h * HEAD_DIM:h * HEAD_DIM + half, :]
        x2 = acc[h * HEAD_DIM + half:(h + 1) * HEAD_DIM, :]
        outs += [(x1 * cos_t - x2 * sin_t) * scale, (x2 * cos_t + x1 * sin_t) * scale]
    return jnp.concatenate(outs, axis=0)


def _proj_kernel(x_ref, g_ref, b_ref, cos_ref, sin_ref, cost_ref, sint_ref, w_ref, wt_ref, wg_ref,
                 out_ref, outt_ref, gate_ref, u_scr, *, zero_front):
    j = pl.program_id(1)

    @pl.when(j == 0)
    def _():
        u = _ln_rows(x_ref[...], g_ref[...], b_ref[...], LN_EPS)
        if zero_front:
            row = lax.broadcasted_iota(I32, u.shape, 0)
            u = jnp.where(row >= FRONT_PAD, u, 0.0)
        ub = u.astype(BF16)
        u_scr[...] = ub
        gate_ref[...] = lax.dot_general(wg_ref[...], ub, _NT, preferred_element_type=F32)

    cw = PROJ_TN // PROJ_SPLIT
    is_tok = j < N_TOK_TILES
    is_k = j == C_AK // PROJ_TN
    jt = j - N_TOK_TILES
    is_aq = (jt >= F_AQ // PROJ_TN) & (jt < F_MQ // PROJ_TN)

    @pl.when(is_tok & jnp.logical_not(is_k))
    def _():
        for c in range(PROJ_SPLIT):
            cols = slice(c * cw, (c + 1) * cw)
            out_ref[:, cols] = jnp.dot(u_scr[...], w_ref[:, cols], preferred_element_type=F32).astype(BF16)

    @pl.when(is_k)
    def _():
        for c in range(PROJ_SPLIT):
            cols = slice(c * cw, (c + 1) * cw)
            acc = jnp.dot(u_scr[...], w_ref[:, cols], preferred_element_type=F32)
            n_rot = min(max(N_KV_HEADS * HEAD_DIM - c * cw, 0), cw) // HEAD_DIM
            if n_rot:
                heads = _rope_heads(acc, cos_ref[...], sin_ref[...], n_rot, 1.0)
                acc = jnp.concatenate(heads + ([acc[:, n_rot * HEAD_DIM:]] if n_rot * HEAD_DIM < cw else []), axis=1)
            out_ref[:, cols] = acc.astype(BF16)

    @pl.when(jnp.logical_not(is_tok) & is_aq)
    def _():
        for c in range(PROJ_SPLIT):
            rows = slice(c * cw, (c + 1) * cw)
            acc = lax.dot_general(wt_ref[rows, :], u_scr[...], _NT, preferred_element_type=F32)
            outt_ref[rows, :] = _rope_heads_t(acc, cost_ref[...], sint_ref[...], ATTN_Q_SCALE).astype(BF16)

    @pl.when(jnp.logical_not(is_tok) & jnp.logical_not(is_aq))
    def _():
        scale = jnp.where(jt == F_MQ // PROJ_TN, M_QK_DIM ** -0.5, 1.0)
        for c in range(PROJ_SPLIT):
            rows = slice(c * cw, (c + 1) * cw)
            acc = lax.dot_general(wt_ref[rows, :], u_scr[...], _NT, preferred_element_type=F32)
            outt_ref[rows, :] = (acc * scale).astype(BF16)


def _proj_call(x, g, b, rope, w_tok, w_feat_t, w_gate_t, *, tm, zero_front, name):
    rows = x.shape[0]
    cos, sin, cos_t, sin_t = rope
    seq = cos.shape[0]
    pos_tiles = seq // tm
    grid = (rows // tm, N_TOK_TILES + N_FEAT_TILES)
    tok_tile = lambda j: jnp.minimum(j, N_TOK_TILES - 1)
    feat_tile = lambda j: jnp.clip(j - N_TOK_TILES, 0, N_FEAT_TILES - 1)
    return pl.pallas_call(
        functools.partial(_proj_kernel, zero_front=zero_front),
        grid=grid,
        in_specs=[
            pl.BlockSpec((tm, D_MODEL), lambda i, j: (i, 0)),
            pl.BlockSpec((1, D_MODEL), lambda i, j: (0, 0)),
            pl.BlockSpec((1, D_MODEL), lambda i, j: (0, 0)),
            pl.BlockSpec((tm, HEAD_DIM), lambda i, j: (i % pos_tiles, 0)),
            pl.BlockSpec((tm, HEAD_DIM), lambda i, j: (i % pos_tiles, 0)),
            pl.BlockSpec((HEAD_DIM // 2, tm), lambda i, j: (0, i % pos_tiles)),
            pl.BlockSpec((HEAD_DIM // 2, tm), lambda i, j: (0, i % pos_tiles)),
            pl.BlockSpec((D_MODEL, PROJ_TN), lambda i, j: (0, tok_tile(j))),
            pl.BlockSpec((PROJ_TN, D_MODEL), lambda i, j: (feat_tile(j), 0)),
            pl.BlockSpec((LANES, D_MODEL), lambda i, j: (0, 0)),
        ],
        out_specs=[
            pl.BlockSpec((tm, PROJ_TN), lambda i, j: (i, tok_tile(j))),
            pl.BlockSpec((None, PROJ_TN, tm), lambda i, j: (i // pos_tiles, feat_tile(j), i % pos_tiles)),
            pl.BlockSpec((None, LANES, tm), lambda i, j: (i // pos_tiles, 0, i % pos_tiles)),
        ],
        out_shape=[jax.ShapeDtypeStruct((rows, N_TOK), BF16), jax.ShapeDtypeStruct((rows // seq, N_FEAT, seq), BF16),
                   jax.ShapeDtypeStruct((rows // seq, LANES, seq), F32)],
        scratch_shapes=[pltpu.VMEM((tm, D_MODEL), BF16)],
        compiler_params=_cparams(("arbitrary", "arbitrary")),
        name=name,
    )(x, g, b, cos, sin, cos_t, sin_t, w_tok, w_feat_t, w_gate_t)


ATTN_MAX_QBLOCKS = 4


def _attn_kernel(sink_ref, qt_ref, *refs, nb, qblocks):
    nband = qblocks + 2
    k_band, v_band = refs[:nband], refs[nband:2 * nband]
    km_ref, vm_ref, ot_ref, bias_scr, s_scr, p_scr = refs[2 * nband:]
    key = lax.broadcasted_iota(I32, (4 * BLOCK, BLOCK), 0)
    qpos = lax.broadcasted_iota(I32, (4 * BLOCK, BLOCK), 1)
    key_blocks = [slice(c * BLOCK, (c + 1) * BLOCK) for c in range(4)]

    def scores(c, cols):
        s = s_scr[key_blocks[c], cols]
        return s if c == 1 else s + bias_scr[key_blocks[c], :]

    for a in range(qblocks):
        blk = pl.program_id(1) * qblocks + a
        qcols = slice(a * BLOCK, (a + 1) * BLOCK)
        lo = jnp.where(blk > 0, 0, BLOCK)
        hi = jnp.where(blk < nb - 1, 3 * BLOCK, 2 * BLOCK)
        band = (jnp.abs(key - BLOCK - qpos) <= WINDOW) & (key >= lo) & (key < hi)
        bias_scr[...] = jnp.where(band | (key >= 3 * BLOCK + FRONT_PAD), 0.0, NEG)
        for kv in range(N_KV_HEADS):
            h0 = kv * Q_PER_KV
            ks = slice(kv * HEAD_DIM, (kv + 1) * HEAD_DIM)
            k = jnp.concatenate([r[:, ks] for r in (*k_band[a:a + 3], km_ref)], axis=0)
            v = jnp.concatenate([r[:, ks] for r in (*v_band[a:a + 3], vm_ref)], axis=0)
            qt = jnp.concatenate([qt_ref[(h0 + g) * HEAD_DIM:(h0 + g + 1) * HEAD_DIM, qcols] for g in range(Q_PER_KV)],
                                 axis=1)
            s_scr[...] = jnp.dot(k, qt, preferred_element_type=F32)
            inv = []
            for g in range(Q_PER_KV):
                cols = slice(g * BLOCK, (g + 1) * BLOCK)
                sink = sink_ref[h0 + g] * LOG2_E
                m = jnp.full((1, BLOCK), sink, F32)
                for c in range(4):
                    m = jnp.maximum(m, jnp.max(scores(c, cols), axis=0, keepdims=True))
                den = jnp.exp2(sink - m)
                for c in range(4):
                    p = jnp.exp2(scores(c, cols) - m)
                    den = den + jnp.sum(p, axis=0, keepdims=True)
                    p_scr[key_blocks[c], cols] = p.astype(BF16)
                inv.append(1.0 / den)
            ot = lax.dot_general(v, p_scr[...], (((0,), (0,)), ((), ())), preferred_element_type=F32)
            for g in range(Q_PER_KV):
                ot_ref[(h0 + g) * HEAD_DIM:(h0 + g + 1) * HEAD_DIM, qcols] = (
                    ot[:, g * BLOCK:(g + 1) * BLOCK] * inv[g]).astype(BF16)


def _attn_call(sink, proj, proj_t, proj_meta, *, batch, nb, name):
    rows = proj.shape[0]
    kw = N_KV_HEADS * HEAD_DIM
    qw = N_Q_HEADS * HEAD_DIM
    kcol, vcol = C_AK // kw, C_AV // kw
    qblocks = max(q for q in range(1, ATTN_MAX_QBLOCKS + 1) if nb % q == 0)
    qt = qblocks * BLOCK

    def band_spec(off, colblk):
        def imap(b, i, sink_ref):
            return (b * nb + jnp.clip(i * qblocks + off, 0, nb - 1), colblk)
        return pl.BlockSpec((BLOCK, kw), imap)

    offs = range(-1, qblocks + 1)
    grid_spec = pltpu.PrefetchScalarGridSpec(
        num_scalar_prefetch=1,
        grid=(batch, nb // qblocks),
        in_specs=[pl.BlockSpec((None, qw, qt), lambda b, i, s: (b, F_AQ // qw, i))]
        + [band_spec(o, kcol) for o in offs] + [band_spec(o, vcol) for o in offs]
        + [pl.BlockSpec((BLOCK, kw), lambda b, i, s: (0, kcol)), pl.BlockSpec((BLOCK, kw), lambda b, i, s: (0, vcol))],
        out_specs=pl.BlockSpec((None, qw, qt), lambda b, i, s: (b, 0, i)),
        scratch_shapes=[pltpu.VMEM((4 * BLOCK, BLOCK), F32), pltpu.VMEM((4 * BLOCK, Q_PER_KV * BLOCK), F32),
                        pltpu.VMEM((4 * BLOCK, Q_PER_KV * BLOCK), BF16)],
    )
    return pl.pallas_call(
        functools.partial(_attn_kernel, nb=nb, qblocks=qblocks),
        grid_spec=grid_spec,
        out_shape=jax.ShapeDtypeStruct((batch, qw, rows // batch), BF16),
        compiler_params=_cparams(("arbitrary", "arbitrary")),
        name=name,
    )(sink, proj_t, *([proj] * (2 * len(offs))), proj_meta, proj_meta)


MLSTM_HEAD_GROUP = 4


def _scan_lanes(x, op, fill, rev):
    lane = lax.broadcasted_iota(I32, x.shape, 1)
    sh = 1
    while sh < BLOCK:
        if rev:
            y = jnp.where(lane < BLOCK - sh, pltpu.roll(x, BLOCK - sh, axis=1), fill)
        else:
            y = jnp.where(lane >= sh, pltpu.roll(x, sh, axis=1), fill)
        x = op(x, y)
        sh *= 2
    return x


def _rows_to_columns(x, eye):
    hi = x.astype(BF16)
    r1 = x - hi.astype(F32)
    mid = r1.astype(BF16)
    lo = (r1 - mid.astype(F32)).astype(BF16)
    move = lambda a: lax.dot_general(eye, a, _NT, preferred_element_type=F32)
    return move(hi) + move(mid) + move(lo)


def _chunk_terms(gts, bias, m_prev, rev, pad_front):
    d = M_HEADS if rev else 0
    li = jnp.concatenate([gt[d:d + M_HEADS, :] + bias[d:d + M_HEADS, :] for gt in gts], axis=0)
    lf = jnp.concatenate([gt[2 * M_HEADS + d:3 * M_HEADS + d, :] + bias[2 * M_HEADS + d:3 * M_HEADS + d, :] for gt in gts],
                         axis=0)
    lf = jax.nn.log_sigmoid(lf)
    if pad_front:
        real = lax.broadcasted_iota(I32, li.shape, 1) >= FRONT_PAD
        li, lf = jnp.where(real, li, NEG), jnp.where(real, lf, 0.0)
    b = _scan_lanes(lf, jnp.add, 0.0, rev)
    tot = jnp.sum(lf, axis=1, keepdims=True)
    u = li - b
    big_m = jnp.maximum(m_prev, _scan_lanes(u, jnp.maximum, -jnp.inf, rev))
    inter = jnp.exp(m_prev - big_m)
    e_neg_m = jnp.exp(-(b + big_m))
    m_new = jnp.maximum(tot + m_prev, tot + jnp.max(u, axis=1, keepdims=True))
    w = jnp.exp(tot + u - m_new)
    sp = jnp.exp(tot + m_prev - m_new)
    return u, big_m, inter, e_neg_m, w, sp, m_new


def _state_update(h, k, vt, w, sp, ct_scr, n_scr):
    wh = w[h:h + 1, :]
    lhs = jnp.concatenate([(vt.astype(F32) * wh).astype(BF16), jnp.broadcast_to(wh, (16, BLOCK)).astype(BF16)], axis=0)
    upd = jnp.dot(lhs, k, preferred_element_type=F32)
    sph = sp[h:h + 1, :]
    ct_scr[h] = sph * ct_scr[h] + upd[:M_V_DIM]
    n_scr[h:h + 1, :] = sph * n_scr[h:h + 1, :] + upd[M_V_DIM:M_V_DIM + 1]


def _mlstm_kernel(*refs, rev, final):
    (qt_ref, k_ref, vt_ref, gt_ref, gtn_ref, bias_ref, eye_ref), refs = refs[:7], refs[7:]
    if not rev:
        (km_ref, vtm_ref, gtm_ref), refs = refs[:3], refs[3:]
    if final:
        (hft_ref, mot_ref, ngb_ref), refs = refs[:3], refs[3:]
    ht_ref, ct_scr, n_scr, t_scr, ucol_scr = refs
    bias = bias_ref[...]
    eye = eye_ref[...]
    n_seq = qt_ref.shape[0]
    n_heads = n_seq * M_HEADS

    def stash_terms(g_ref, m_prev):
        gts = [g_ref[b, 0:4 * M_HEADS, :] for b in range(n_seq)]
        u, big_m, inter, e_neg_m, w, sp, m_new = _chunk_terms(gts, bias, m_prev, rev, False)
        ucol_scr[...] = _rows_to_columns(u, eye)
        for slot, val in enumerate((big_m, inter, e_neg_m, w, sp, m_new)):
            t_scr[slot] = jnp.broadcast_to(val, (n_heads, LANES))

    @pl.when(pl.program_id(0) == 0)
    def _():
        ct_scr[...] = jnp.zeros_like(ct_scr)
        n_scr[...] = jnp.zeros_like(n_scr)
        m0 = jnp.zeros((n_heads, 1), F32)
        if not rev:
            _, _, _, _, w, sp, m0 = _chunk_terms([gtm_ref[0, 0:4 * M_HEADS, :]] * n_seq, bias, m0, rev, True)
            for hh in range(n_heads):
                h = hh % M_HEADS
                _state_update(hh, km_ref[:, h * M_QK_DIM:(h + 1) * M_QK_DIM],
                              vtm_ref[0, h * M_V_DIM:(h + 1) * M_V_DIM, :], w, sp, ct_scr, n_scr)
        stash_terms(gt_ref, m0)

    big_m, inter, e_neg_m, w = t_scr[0], t_scr[1], t_scr[2], t_scr[3]
    sp, m_new = t_scr[4][:, 0:1], t_scr[5][:, 0:1]
    u_col = ucol_scr[...]
    s_idx = lax.broadcasted_iota(I32, (BLOCK, BLOCK), 0)
    t_idx = lax.broadcasted_iota(I32, (BLOCK, BLOCK), 1)
    causal = (s_idx >= t_idx) if rev else (s_idx <= t_idx)
    qk_cols = lambda h: slice((h % M_HEADS) * M_QK_DIM, (h % M_HEADS + 1) * M_QK_DIM)
    v_rows = lambda h: slice((h % M_HEADS) * M_V_DIM, (h % M_HEADS + 1) * M_V_DIM)
    qt_of = lambda h: qt_ref[h // M_HEADS, qk_cols(h), :]
    k_of = lambda h: k_ref[h // M_HEADS, :, qk_cols(h)]
    vt_of = lambda h: vt_ref[h // M_HEADS, v_rows(h), :]
    for g0 in range(0, n_heads, MLSTM_HEAD_GROUP):
        hs = range(g0, g0 + MLSTM_HEAD_GROUP)
        st = {h: jnp.dot(jnp.concatenate([k_of(h), jnp.broadcast_to(n_scr[h:h + 1, :], (16, M_QK_DIM)).astype(BF16)], axis=0),
                         qt_of(h), preferred_element_type=F32) for h in hs}
        decay = {h: jnp.exp(jnp.where(causal, u_col[:, h:h + 1] - big_m[h:h + 1, :], NEG)) for h in hs}
        sd = {h: st[h][:BLOCK] * decay[h] for h in hs}
        rden = {}
        for h in hs:
            den = inter[h:h + 1, :] * st[h][BLOCK:BLOCK + 1] + jnp.sum(sd[h], axis=0, keepdims=True)
            rden[h] = 1.0 / jnp.maximum(jnp.abs(den), e_neg_m[h:h + 1, :])
        hv = {}
        for h in hs:
            lhs = jnp.concatenate([vt_of(h), ct_scr[h].astype(BF16)], axis=1)
            rhs = jnp.concatenate([sd[h].astype(BF16), (qt_of(h).astype(F32) * inter[h:h + 1, :]).astype(BF16)], axis=0)
            hv[h] = jnp.dot(lhs, rhs, preferred_element_type=F32) * rden[h]
        for h in hs:
            b, rows = h // M_HEADS, v_rows(h)
            x = hv[h]
            if final:
                x = x + hft_ref[b, rows, :].astype(F32)
                mu = jnp.mean(x, axis=0, keepdims=True)
                xc = x - mu
                var = jnp.mean(xc * xc, axis=0, keepdims=True)
                x = xc * lax.rsqrt(var + M_NORM_EPS) * ngb_ref[rows, :] * jax.nn.sigmoid(mot_ref[b, rows, :].astype(F32))
            ht_ref[b, rows, :] = x.astype(ht_ref.dtype)
        for h in hs:
            _state_update(h, k_of(h), vt_of(h), w, sp, ct_scr, n_scr)
    stash_terms(gtn_ref, m_new)


def _mlstm_call(proj, proj_t, gates_t, bias, eye, meta, final_in, *, batch, nc, rev, name):
    final = final_in is not None
    qw, vw = M_HEADS * M_QK_DIM, M_HEADS * M_V_DIM
    seq = nc * BLOCK
    proj3 = proj.reshape(batch, seq, N_TOK)
    chunk = (lambda i: nc - 1 - i) if rev else (lambda i: i)
    feat_spec = lambda width, off: pl.BlockSpec((batch, width, BLOCK), lambda i: (0, off // width, chunk(i)))

    in_specs = [
        feat_spec(qw, F_MQ),
        pl.BlockSpec((batch, BLOCK, qw), lambda i: (0, chunk(i), C_MK // qw)),
        feat_spec(vw, F_MV),
        pl.BlockSpec((batch, LANES, BLOCK), lambda i: (0, 0, chunk(i))),
        pl.BlockSpec((batch, LANES, BLOCK), lambda i: (0, 0, chunk(jnp.minimum(i + 1, nc - 1)))),
        pl.BlockSpec((4 * M_HEADS, 1), lambda i: (0, 0)),
        pl.BlockSpec((BLOCK, BLOCK), lambda i: (0, 0)),
    ]
    args = [proj_t, proj3, proj_t, gates_t, gates_t, bias, eye]
    if not rev:
        proj_m, proj_tm, gates_tm = meta
        in_specs += [
            pl.BlockSpec((BLOCK, qw), lambda i: (0, C_MK // qw)),
            pl.BlockSpec((1, vw, BLOCK), lambda i: (0, F_MV // vw, 0)),
            pl.BlockSpec((1, LANES, BLOCK), lambda i: (0, 0, 0)),
        ]
        args += [proj_m, proj_tm, gates_tm]
    if final:
        h_fwd_t, norm_g_b = final_in
        in_specs += [feat_spec(vw, 0), feat_spec(vw, F_MO), pl.BlockSpec((vw, BLOCK), lambda i: (0, 0))]
        args += [h_fwd_t, proj_t, norm_g_b]
    n_heads = batch * M_HEADS
    return pl.pallas_call(
        functools.partial(_mlstm_kernel, rev=rev, final=final),
        grid=(nc,),
        in_specs=in_specs,
        out_specs=feat_spec(vw, 0),
        out_shape=jax.ShapeDtypeStruct((batch, vw, seq), BF16),
        scratch_shapes=[pltpu.VMEM((n_heads, M_V_DIM, M_QK_DIM), F32), pltpu.VMEM((n_heads, LANES), F32),
                        pltpu.VMEM((6, n_heads, LANES), F32), pltpu.VMEM((BLOCK, n_heads), F32)],
        compiler_params=_cparams(("arbitrary",)),
        name=name,
    )(*args)


def _merge_kernel(at_ref, mt_ref, ga_ref, gb_ref, wa_ref, wb_ref, o_ref):
    tn = (((0,), (0,)), ((), ()))
    ya = lax.dot_general(at_ref[...], wa_ref[...], tn, preferred_element_type=F32)
    yb = lax.dot_general(mt_ref[...], wb_ref[...], tn, preferred_element_type=F32)
    o_ref[...] = (jax.nn.sigmoid(ga_ref[...].astype(F32)) * ya + jax.nn.sigmoid(gb_ref[...].astype(F32)) * yb).astype(BF16)


def _merge_call(attn_t, mem_t, proj, wa, wb, *, tm, name):
    rows = proj.shape[0]
    row_spec = pl.BlockSpec((tm, D_MODEL), lambda i: (i, 0))
    tiles_per_seq = attn_t.shape[2] // tm
    feat_spec = pl.BlockSpec((None, D_MODEL, tm), lambda i: (i // tiles_per_seq, 0, i % tiles_per_seq))
    w_spec = pl.BlockSpec((D_MODEL, D_MODEL), lambda i: (0, 0), pipeline_mode=pl.Buffered(1))
    return pl.pallas_call(
        _merge_kernel,
        grid=(rows // tm,),
        in_specs=[feat_spec, feat_spec,
                  pl.BlockSpec((tm, D_MODEL), lambda i: (i, C_GA // D_MODEL)),
                  pl.BlockSpec((tm, D_MODEL), lambda i: (i, C_GB // D_MODEL)),
                  w_spec, w_spec],
        out_specs=row_spec,
        out_shape=jax.ShapeDtypeStruct((rows, D_MODEL), BF16),
        compiler_params=_cparams(("arbitrary",)),
        name=name,
    )(attn_t, mem_t, proj, proj, wa, wb)


def _post_kernel(mg_ref, x_ref, eg_ref, eb_ref, wo_ref, g1_ref, b1_ref, wr_hi_ref, wr_lo_ref, rb_ref, low_ref, cnt_in_ref,
                 x1_joint_ref, x1_ref, ri_ref, rw_ref, cnt_ref):
    del x1_joint_ref

    @pl.when(pl.program_id(0) == 0)
    def _():
        cnt_ref[...] = cnt_in_ref[...]

    x0 = _ln_rows(x_ref[...], eg_ref[...], eb_ref[...], LN_EPS)
    y = jnp.dot(mg_ref[...], wo_ref[...], preferred_element_type=F32)
    x1 = _ln_rows(ALPHA * x0 + y, g1_ref[...], b1_ref[...], LN_EPS)
    x1_ref[...] = x1

    hi = x1.astype(BF16)
    lo = (x1 - hi.astype(F32)).astype(BF16)
    logits = (jnp.dot(hi, wr_hi_ref[...], preferred_element_type=F32)
              + jnp.dot(lo, wr_hi_ref[...], preferred_element_type=F32)
              + jnp.dot(hi, wr_lo_ref[...], preferred_element_type=F32)) + rb_ref[...]
    lane = lax.broadcasted_iota(I32, logits.shape, 1)
    big = jnp.int32(LANES)

    def first_max(valid):
        vmax = jnp.max(jnp.where(valid, logits, -jnp.inf), axis=1, keepdims=True)
        idx = jnp.min(jnp.where(valid & (logits == vmax), lane, big), axis=1, keepdims=True)
        return vmax, idx

    is_grp = lane < N_GROUPS
    gmax, grp = first_max(is_grp)
    p_grp = 1.0 / jnp.sum(jnp.where(is_grp, jnp.exp(logits - gmax), 0.0), axis=1, keepdims=True)
    e_lo = N_GROUPS + grp * EXPERTS_PER_GROUP
    in_grp = (lane >= e_lo) & (lane < e_lo + EXPERTS_PER_GROUP)
    l1, i1 = first_max(in_grp)
    l2, i2 = first_max(in_grp & (lane != i1))
    t2 = jnp.exp(l2 - l1)
    w1 = p_grp / (1.0 + t2)
    w2 = p_grp * t2 / (1.0 + t2)
    e1 = i1 - N_GROUPS
    e2 = i2 - N_GROUPS

    oh1 = (lane == e1).astype(F32)
    oh2 = (lane == e2).astype(F32)
    cnt = oh1 + oh2
    before = jnp.dot(low_ref[...], cnt.astype(BF16), preferred_element_type=F32) + cnt_ref[...]
    r1 = jnp.sum(oh1 * before, axis=1, keepdims=True).astype(I32)
    r2 = jnp.sum(oh2 * before, axis=1, keepdims=True).astype(I32)
    cnt_ref[...] = cnt_ref[...] + jnp.sum(cnt, axis=0, keepdims=True)
    ri_ref[...] = jnp.where(lane == 0, e1, jnp.where(lane == 1, e2, jnp.where(lane == 2, r1, jnp.where(lane == 3, r2, 0))))
    rw_ref[...] = jnp.where(lane == 0, w1, jnp.where(lane == 1, w2, 0.0))


def _post_call(merged, x, eg, eb, wo, g1, b1, wr_hi, wr_lo, rb, low, cnt_in, x1_joint, *, tm, row_off, total_rows, name):
    rows = merged.shape[0]
    off = row_off // tm
    row_spec = pl.BlockSpec((tm, D_MODEL), lambda i: (i, 0))
    vec_spec = pl.BlockSpec((1, D_MODEL), lambda i: (0, 0))
    lane_spec = pl.BlockSpec((1, LANES), lambda i: (0, 0))
    rt_spec = pl.BlockSpec((D_MODEL, LANES), lambda i: (0, 0))
    assert row_off % tm == 0 and x1_joint.shape == (total_rows, D_MODEL)
    return pl.pallas_call(
        _post_kernel,
        grid=(rows // tm,),
        in_specs=[row_spec, row_spec, vec_spec, vec_spec,
                  pl.BlockSpec((D_MODEL, D_MODEL), lambda i: (0, 0), pipeline_mode=pl.Buffered(1)),
                  vec_spec, vec_spec, rt_spec, rt_spec, lane_spec,
                  pl.BlockSpec((tm, tm), lambda i: (0, 0)), lane_spec, pl.BlockSpec(memory_space=pl.ANY)],
        out_specs=[pl.BlockSpec((tm, D_MODEL), lambda i: (i + off, 0)),
                   pl.BlockSpec((tm, LANES), lambda i: (i, 0)), pl.BlockSpec((tm, LANES), lambda i: (i, 0)),
                   lane_spec],
        out_shape=[jax.ShapeDtypeStruct((total_rows, D_MODEL), F32), jax.ShapeDtypeStruct((rows, LANES), I32),
                   jax.ShapeDtypeStruct((rows, LANES), F32), jax.ShapeDtypeStruct((1, LANES), F32)],
        input_output_aliases={12: 0},
        compiler_params=_cparams(("arbitrary",)),
        name=name,
    )(merged, x, eg, eb, wo, g1, b1, wr_hi, wr_lo, rb, low, cnt_in, x1_joint)


def _row_copy(src, s, dst, d, sem):
    return pltpu.make_async_copy(src.at[pl.ds(s, 1), :], dst.at[pl.ds(d, 1), :], sem)


GATE_STAGE = 2


def _expert_kernel(blk_e_ref, n_act_ref, tok_ref, x_ref, wg_ref, wu_ref, wd_ref, y_ref, xbuf, sem):
    del blk_e_ref
    i = pl.program_id(0)
    n_act = n_act_ref[0]

    def start(blk, r):
        _row_copy(x_ref, tok_ref[blk * SLOT_BLOCK + r], xbuf.at[blk % 2], r, sem.at[blk % 2]).start()

    def wait(blk):
        pltpu.make_async_copy(x_ref.at[pl.ds(0, SLOT_BLOCK), :], xbuf.at[blk % 2], sem.at[blk % 2]).wait()

    @pl.when(i == 0)
    def _():
        lax.fori_loop(0, SLOT_BLOCK, lambda r, c: (start(i, r), c)[1], 0, unroll=8)

    @pl.when(i < n_act)
    def _():
        wait(i)
        xb = xbuf[i % 2].astype(BF16)
        half = D_EXPERT // 2
        xbuf[GATE_STAGE, :, 0:half] = jnp.dot(xb, wg_ref[0, :, 0:half], preferred_element_type=F32)
        for r in range(SLOT_BLOCK):
            start(i + 1, r)
        gate_hi = jnp.dot(xb, wg_ref[0, :, half:], preferred_element_type=F32)
        up = jnp.dot(xb, wu_ref[0], preferred_element_type=F32)
        gate = jnp.concatenate([xbuf[GATE_STAGE, :, 0:half], gate_hi], axis=1)
        hdn = (gate * jax.nn.sigmoid(gate) * up).astype(BF16)
        y_ref[...] = jnp.dot(hdn, wd_ref[0], preferred_element_type=F32)

    @pl.when(i >= n_act)
    def _():
        @pl.when(i == n_act)
        def _():
            wait(i)
        y_ref[...] = jnp.zeros_like(y_ref)


def _expert_call(blk_e, n_act, row_tok, x1, wg, wu, wd, *, name):
    n_blocks = blk_e.shape[0]
    grid_spec = pltpu.PrefetchScalarGridSpec(
        num_scalar_prefetch=3,
        grid=(n_blocks,),
        in_specs=[
            pl.BlockSpec(memory_space=pl.ANY),
            pl.BlockSpec((1, D_MODEL, D_EXPERT), lambda i, be, na, tok: (be[i], 0, 0)),
            pl.BlockSpec((1, D_MODEL, D_EXPERT), lambda i, be, na, tok: (be[i], 0, 0)),
            pl.BlockSpec((1, D_EXPERT, D_MODEL), lambda i, be, na, tok: (be[i], 0, 0)),
        ],
        out_specs=pl.BlockSpec((SLOT_BLOCK, D_MODEL), lambda i, be, na, tok: (i, 0)),
        scratch_shapes=[pltpu.VMEM((3, SLOT_BLOCK, D_MODEL), F32), pltpu.SemaphoreType.DMA((2,))],
    )
    return pl.pallas_call(
        _expert_kernel,
        grid_spec=grid_spec,
        out_shape=jax.ShapeDtypeStruct((n_blocks * SLOT_BLOCK, D_MODEL), F32),
        compiler_params=_cparams(("arbitrary",)),
        name=name,
    )(blk_e, n_act, row_tok, x1, wg, wu, wd)


COMBINE_STAGE = 4


def _combine_kernel(dest_ref, x1_ref, rw_ref, g2_ref, b2_ref, yr_ref, o_ref, ybuf, sem, *, tile):
    i = pl.program_id(0)
    last = pl.num_programs(0) - 1

    def start(t, par, r, k):
        _row_copy(yr_ref, dest_ref[2 * (t * tile + r) + k], ybuf.at[2 * par + k], r, sem.at[par]).start()

    def wait(par):
        for k in range(2):
            pltpu.make_async_copy(yr_ref.at[pl.ds(0, tile), :], ybuf.at[2 * par + k], sem.at[par]).wait()

    @pl.when(i == 0)
    def _():
        lax.fori_loop(0, tile, lambda r, c: (start(i, 0, r, 0), start(i, 0, r, 1), c)[2], 0, unroll=4)

    par = i % 2
    wait(par)
    rw = rw_ref[...]
    s = ALPHA * x1_ref[...] + rw[:, 0:1] * ybuf[2 * par] + rw[:, 1:2] * ybuf[2 * par + 1]
    ybuf[COMBINE_STAGE, 0:8, :] = jnp.broadcast_to(b2_ref[...], (8, D_MODEL))
    nxt = jnp.minimum(i + 1, last)
    for r in range(tile):
        for k in range(2):
            start(nxt, 1 - par, r, k)
    o_ref[...] = _ln_rows(s, g2_ref[...], ybuf[COMBINE_STAGE, 0:1, :], LN_EPS)

    @pl.when(i == last)
    def _():
        wait(1 - par)


def _combine_call(dest, x1, rw, g2, b2, yr, *, row_off, tile, name):
    rows = rw.shape[0]
    off = row_off // tile
    vec_spec = pl.BlockSpec((1, D_MODEL), lambda i, d: (0, 0))
    grid_spec = pltpu.PrefetchScalarGridSpec(
        num_scalar_prefetch=1,
        grid=(rows // tile,),
        in_specs=[pl.BlockSpec((tile, D_MODEL), lambda i, d: (i + off, 0)),
                  pl.BlockSpec((tile, LANES), lambda i, d: (i, 0)), vec_spec, vec_spec,
                  pl.BlockSpec(memory_space=pl.ANY)],
        out_specs=pl.BlockSpec((tile, D_MODEL), lambda i, d: (i, 0)),
        scratch_shapes=[pltpu.VMEM((5, tile, D_MODEL), F32), pltpu.SemaphoreType.DMA((2,))],
    )
    return pl.pallas_call(
        functools.partial(_combine_kernel, tile=tile),
        grid_spec=grid_spec,
        out_shape=jax.ShapeDtypeStruct((rows, D_MODEL), F32),
        compiler_params=_cparams(("arbitrary",)),
        name=name,
    )(dest.reshape(-1), x1, rw, g2, b2, yr)


def _rope_tables(pos):
    half = HEAD_DIM // 2
    inv = ROPE_THETA ** (-jnp.arange(half, dtype=F32) / half)
    ang = pos.astype(F32)[:, None] * inv[None, :]
    cos, sin = jnp.cos(ang), jnp.sin(ang)
    return jnp.concatenate([cos, cos], axis=1), jnp.concatenate([-sin, sin], axis=1), cos.T, sin.T


def kernel(x_prompt, x_sample, meta_tokens, ln_emb_g, ln_emb_b, w_in, attn_sink, m_gate_bias, m_norm_g, w_br_attn, w_br_mlstm, w_out, ln1_g, ln1_b, w_router_group, b_router_group, w_router_expert, b_router_expert, w_expert_gate, w_expert_up, w_expert_down, ln2_g, ln2_b):
    assert w_in.shape[0] == DEPTH == 1
    row = lambda v: v.reshape(1, -1).astype(F32)
    eg, eb = row(ln_emb_g), row(ln_emb_b)
    g1, b1, g2, b2 = row(ln1_g[0]), row(ln1_b[0]), row(ln2_g[0]), row(ln2_b[0])

    w = w_in[0]
    sec = {}
    off = 0
    for nm, width in (("aq", 2048), ("ak", 512), ("av", 512), ("mq", 1024), ("mk", 1024), ("mv", 2048), ("mo", 2048),
                      ("mg", 32), ("bg", 4096)):
        sec[nm] = w[:, off:off + width]
        off += width
    w_tok = jnp.concatenate([sec[n] for n in ("bg", "mk", "ak", "av")], axis=1).astype(BF16)
    w_feat_t = jnp.concatenate([sec[n] for n in ("mv", "mo", "aq", "mq")], axis=1).T.astype(BF16)
    w_gate_t = jnp.pad(sec["mg"], ((0, 0), (0, LANES - 4 * M_HEADS))).T.astype(BF16)
    wa, wb, wo = w_br_attn[0].astype(BF16), w_br_mlstm[0].astype(BF16), w_out[0].astype(BF16)
    wr = jnp.pad(jnp.concatenate([w_router_group[0], w_router_expert[0]], axis=1),
                 ((0, 0), (0, LANES - N_GROUPS - N_EXPERTS)))
    wr_hi = wr.astype(BF16)
    wr_lo = (wr - wr_hi.astype(F32)).astype(BF16)
    rb = jnp.pad(jnp.concatenate([b_router_group[0], b_router_expert[0]]), (0, LANES - N_GROUPS - N_EXPERTS)).reshape(1, LANES)
    weg, weu, wed = w_expert_gate[0].astype(BF16), w_expert_up[0].astype(BF16), w_expert_down[0].astype(BF16)
    gate_bias = m_gate_bias[0].reshape(4 * M_HEADS, 1).astype(F32)
    norm_g_b = jnp.broadcast_to(m_norm_g[0].astype(F32)[:, None], (M_HEADS * M_V_DIM, BLOCK))
    sink = attn_sink[0].astype(F32)
    eye = jnp.eye(BLOCK, dtype=BF16)

    xm = jnp.pad(meta_tokens.astype(F32), ((FRONT_PAD, 0), (0, 0)))
    meta = _proj_call(xm, eg, eb, _rope_tables(jnp.arange(BLOCK) - FRONT_PAD), w_tok, w_feat_t, w_gate_t, tm=BLOCK,
                      zero_front=True, name="proj_meta")
    proj_m = meta[0]

    groups = []
    cnt = jnp.zeros((1, LANES), F32)
    total_rows = x_prompt.shape[0] * x_prompt.shape[1] + x_sample.shape[0] * x_sample.shape[1]
    x1 = jnp.zeros((total_rows, D_MODEL), F32)
    row_off = 0
    for gi, xg in enumerate((x_prompt, x_sample)):
        batch, seq, _ = xg.shape
        assert seq % BLOCK == 0
        nb = seq // BLOCK
        rows = batch * seq
        x2 = xg.reshape(rows, D_MODEL)
        tm = _pick_tile(seq, 1024)
        proj, proj_t, gates_t = _proj_call(x2, eg, eb, _rope_tables(jnp.arange(seq) + N_META), w_tok, w_feat_t, w_gate_t,
                                           tm=tm, zero_front=False, name=f"proj_g{gi}")
        attn_t = _attn_call(sink, proj, proj_t, proj_m, batch=batch, nb=nb, name=f"attn_g{gi}")
        h_fwd_t = _mlstm_call(proj, proj_t, gates_t, gate_bias, eye, meta, None,
                              batch=batch, nc=nb, rev=False, name=f"mlstm_fwd_g{gi}")
        mem_t = _mlstm_call(proj, proj_t, gates_t, gate_bias, eye, None, (h_fwd_t, norm_g_b),
                            batch=batch, nc=nb, rev=True, name=f"mlstm_bwd_g{gi}")
        tp = _pick_tile(seq, 512)
        merged = _merge_call(attn_t, mem_t, proj, wa, wb, tm=tp, name=f"merge_g{gi}")
        low = (jnp.arange(tp)[None, :] < jnp.arange(tp)[:, None]).astype(BF16)
        x1, ri, rw, cnt = _post_call(merged, x2, eg, eb, wo, g1, b1, wr_hi, wr_lo, rb, low, cnt, x1, tm=tp,
                                     row_off=row_off, total_rows=total_rows, name=f"post_g{gi}")
        groups.append((xg.shape, row_off, ri[:, :4], rw))
        row_off += rows

    counts = cnt[0, :N_EXPERTS].astype(I32)
    padded = (counts + SLOT_BLOCK - 1) // SLOT_BLOCK * SLOT_BLOCK
    pad_ends = jnp.cumsum(padded)
    pad_starts = pad_ends - padded
    n_blocks = -(-2 * total_rows // SLOT_BLOCK) + N_EXPERTS
    blk_start = jnp.arange(n_blocks, dtype=I32) * SLOT_BLOCK
    blk_e = jnp.minimum(jnp.sum((pad_ends[None, :] <= blk_start[:, None]).astype(I32), axis=1), N_EXPERTS - 1)
    n_act = (pad_ends[-1:] // SLOT_BLOCK).astype(I32)
    ri_all = jnp.concatenate([g[2] for g in groups], axis=0)
    dest = pad_starts[ri_all[:, 0:2]] + ri_all[:, 2:4]
    tok = jnp.broadcast_to(jnp.arange(total_rows, dtype=I32)[:, None], dest.shape)
    row_tok = jnp.zeros((n_blocks * SLOT_BLOCK,), I32).at[dest.reshape(-1)].set(tok.reshape(-1), unique_indices=True)

    yr = _expert_call(blk_e, n_act, row_tok, x1, weg, weu, wed, name="experts")
    outs = []
    for gi, (shape, off, _, rw) in enumerate(groups):
        rows = rw.shape[0]
        out = _combine_call(dest[off:off + rows], x1, rw, g2, b2, yr, row_off=off, tile=_pick_tile(rows, 512),
                            name=f"combine_g{gi}")
        outs.append(out.reshape(shape))
    return tuple(outs)
```

```python
import functools

import jax
import jax.numpy as jnp
from jax import lax
from jax.experimental import pallas as pl
from jax.experimental.pallas import tpu as pltpu

F32 = jnp.float32
BF16 = jnp.bfloat16
I32 = jnp.int32

D_MODEL = 2048
DEPTH = 1
N_META = 16
BLOCK = 128
FRONT_PAD = BLOCK - N_META
N_Q_HEADS = 16
N_KV_HEADS = 4
HEAD_DIM = 128
Q_PER_KV = 4
WINDOW = 128
ROPE_THETA = 10000.0
M_HEADS = 8
M_QK_DIM = 128
M_V_DIM = 256
M_NORM_EPS = 1e-6
N_GROUPS = 4
EXPERTS_PER_GROUP = 8
N_EXPERTS = 32
D_EXPERT = 1024
ALPHA = (2 * DEPTH) ** 0.25
LN_EPS = 1e-5
NEG = -1e30

C_GA, C_GB, C_MK, C_AK, C_AV = 0, 2048, 4096, 5120, 5632
N_TOK = 6144
F_MV, F_MO, F_AQ, F_MQ = 0, 2048, 4096, 6144
N_FEAT = 7168
PROJ_TN = 1024
LANES = 128
SLOT_BLOCK = 256
VMEM_LIMIT = 56 * 1024 * 1024


def _pick_tile(n, pref):
    t = min(pref, n)
    t -= t % BLOCK
    while n % t:
        t -= BLOCK
    return t


def _cparams(sem, vmem=VMEM_LIMIT):
    return pltpu.CompilerParams(dimension_semantics=sem, vmem_limit_bytes=vmem)


def _ln_rows(x, g, b, eps):
    mu = jnp.mean(x, axis=-1, keepdims=True)
    xc = x - mu
    var = jnp.mean(xc * xc, axis=-1, keepdims=True)
    return xc * lax.rsqrt(var + eps) * g + b


def _rope_heads(acc, cos, sin, nheads, scale):
    outs = []
    for h in range(nheads):
        xh = acc[:, h * HEAD_DIM:(h + 1) * HEAD_DIM]
        o = xh * cos + pltpu.roll(xh, HEAD_DIM // 2, axis=1) * sin
        outs.append(o * scale)
    return outs


LOG2_E = 1.4426950408889634
ATTN_Q_SCALE = HEAD_DIM ** -0.5 * LOG2_E
PROJ_SPLIT = 1
N_TOK_TILES = N_TOK // PROJ_TN
N_FEAT_TILES = N_FEAT // PROJ_TN
_NT = (((1,), (1,)), ((), ()))


def _rope_heads_t(acc, cos_t, sin_t, scale):
    half = HEAD_DIM // 2
    outs = []
    for h in range(acc.shape[0] // HEAD_DIM):
        x1 = acc[h * HEAD_DIM:h * HEAD_DIM + half, :]
        x2 = acc[h * HEAD_DIM + half:(h + 1) * HEAD_DIM, :]
        outs += [(x1 * cos_t - x2 * sin_t) * scale, (x2 * cos_t + x1 * sin_t) * scale]
    return jnp.concatenate(outs, axis=0)


def _proj_kernel(x_ref, g_ref, b_ref, cos_ref, sin_ref, cost_ref, sint_ref, w_ref, wt_ref, wg_ref,
                 out_ref, outt_ref, gate_ref, u_scr, *, zero_front):
    j = pl.program_id(1)

    @pl.when(j == 0)
    def _():
        u = _ln_rows(x_ref[...], g_ref[...], b_ref[...], LN_EPS)
        if zero_front:
            row = lax.broadcasted_iota(I32, u.shape, 0)
            u = jnp.where(row >= FRONT_PAD, u, 0.0)
        ub = u.astype(BF16)
        u_scr[...] = ub
        gate_ref[...] = lax.dot_general(wg_ref[...], ub, _NT, preferred_element_type=F32)

    cw = PROJ_TN // PROJ_SPLIT
    is_tok = j < N_TOK_TILES
    is_k = j == C_AK // PROJ_TN
    jt = j - N_TOK_TILES
    is_aq = (jt >= F_AQ // PROJ_TN) & (jt < F_MQ // PROJ_TN)

    @pl.when(is_tok & jnp.logical_not(is_k))
    def _():
        for c in range(PROJ_SPLIT):
            cols = slice(c * cw, (c + 1) * cw)
            out_ref[:, cols] = jnp.dot(u_scr[...], w_ref[:, cols], preferred_element_type=F32).astype(BF16)

    @pl.when(is_k)
    def _():
        for c in range(PROJ_SPLIT):
            cols = slice(c * cw, (c + 1) * cw)
            acc = jnp.dot(u_scr[...], w_ref[:, cols], preferred_element_type=F32)
            n_rot = min(max(N_KV_HEADS * HEAD_DIM - c * cw, 0), cw) // HEAD_DIM
            if n_rot:
                heads = _rope_heads(acc, cos_ref[...], sin_ref[...], n_rot, 1.0)
                acc = jnp.concatenate(heads + ([acc[:, n_rot * HEAD_DIM:]] if n_rot * HEAD_DIM < cw else []), axis=1)
            out_ref[:, cols] = acc.astype(BF16)

    @pl.when(jnp.logical_not(is_tok) & is_aq)
    def _():
        for c in range(PROJ_SPLIT):
            rows = slice(c * cw, (c + 1) * cw)
            acc = lax.dot_general(wt_ref[rows, :], u_scr[...], _NT, preferred_element_type=F32)
            outt_ref[rows, :] = _rope_heads_t(acc, cost_ref[...], sint_ref[...], ATTN_Q_SCALE).astype(BF16)

    @pl.when(jnp.logical_not(is_tok) & jnp.logical_not(is_aq))
    def _():
        scale = jnp.where(jt == F_MQ // PROJ_TN, M_QK_DIM ** -0.5, 1.0)
        for c in range(PROJ_SPLIT):
            rows = slice(c * cw, (c + 1) * cw)
            acc = lax.dot_general(wt_ref[rows, :], u_scr[...], _NT, preferred_element_type=F32)
            outt_ref[rows, :] = (acc * scale).astype(BF16)


def _proj_call(x, g, b, rope, w_tok, w_feat_t, w_gate_t, *, tm, zero_front, name):
    rows = x.shape[0]
    cos, sin, cos_t, sin_t = rope
    seq = cos.shape[0]
    pos_tiles = seq // tm
    grid = (rows // tm, N_TOK_TILES + N_FEAT_TILES)
    tok_tile = lambda j: jnp.minimum(j, N_TOK_TILES - 1)
    feat_tile = lambda j: jnp.clip(j - N_TOK_TILES, 0, N_FEAT_TILES - 1)
    return pl.pallas_call(
        functools.partial(_proj_kernel, zero_front=zero_front),
        grid=grid,
        in_specs=[
            pl.BlockSpec((tm, D_MODEL), lambda i, j: (i, 0)),
            pl.BlockSpec((1, D_MODEL), lambda i, j: (0, 0)),
            pl.BlockSpec((1, D_MODEL), lambda i, j: (0, 0)),
            pl.BlockSpec((tm, HEAD_DIM), lambda i, j: (i % pos_tiles, 0)),
            pl.BlockSpec((tm, HEAD_DIM), lambda i, j: (i % pos_tiles, 0)),
            pl.BlockSpec((HEAD_DIM // 2, tm), lambda i, j: (0, i % pos_tiles)),
            pl.BlockSpec((HEAD_DIM // 2, tm), lambda i, j: (0, i % pos_tiles)),
            pl.BlockSpec((D_MODEL, PROJ_TN), lambda i, j: (0, tok_tile(j))),
            pl.BlockSpec((PROJ_TN, D_MODEL), lambda i, j: (feat_tile(j), 0)),
            pl.BlockSpec((LANES, D_MODEL), lambda i, j: (0, 0)),
        ],
        out_specs=[
            pl.BlockSpec((tm, PROJ_TN), lambda i, j: (i, tok_tile(j))),
            pl.BlockSpec((None, PROJ_TN, tm), lambda i, j: (i // pos_tiles, feat_tile(j), i % pos_tiles)),
            pl.BlockSpec((None, LANES, tm), lambda i, j: (i // pos_tiles, 0, i % pos_tiles)),
        ],
        out_shape=[jax.ShapeDtypeStruct((rows, N_TOK), BF16), jax.ShapeDtypeStruct((rows // seq, N_FEAT, seq), BF16),
                   jax.ShapeDtypeStruct((rows // seq, LANES, seq), F32)],
        scratch_shapes=[pltpu.VMEM((tm, D_MODEL), BF16)],
        compiler_params=_cparams(("arbitrary", "arbitrary")),
        name=name,
    )(x, g, b, cos, sin, cos_t, sin_t, w_tok, w_feat_t, w_gate_t)


ATTN_MAX_QBLOCKS = 4


def _attn_kernel(sink_ref, qt_ref, *refs, nb, qblocks):
    nband = qblocks + 2
    k_band, v_band = refs[:nband], refs[nband:2 * nband]
    km_ref, vm_ref, ot_ref, bias_scr, s_scr, p_scr = refs[2 * nband:]
    key = lax.broadcasted_iota(I32, (4 * BLOCK, BLOCK), 0)
    qpos = lax.broadcasted_iota(I32, (4 * BLOCK, BLOCK), 1)
    key_blocks = [slice(c * BLOCK, (c + 1) * BLOCK) for c in range(4)]

    def scores(c, cols):
        s = s_scr[key_blocks[c], cols]
        return s if c == 1 else s + bias_scr[key_blocks[c], :]

    for a in range(qblocks):
        blk = pl.program_id(1) * qblocks + a
        qcols = slice(a * BLOCK, (a + 1) * BLOCK)
        lo = jnp.where(blk > 0, 0, BLOCK)
        hi = jnp.where(blk < nb - 1, 3 * BLOCK, 2 * BLOCK)
        band = (jnp.abs(key - BLOCK - qpos) <= WINDOW) & (key >= lo) & (key < hi)
        bias_scr[...] = jnp.where(band | (key >= 3 * BLOCK + FRONT_PAD), 0.0, NEG)
        for kv in range(N_KV_HEADS):
            h0 = kv * Q_PER_KV
            ks = slice(kv * HEAD_DIM, (kv + 1) * HEAD_DIM)
            k = jnp.concatenate([r[:, ks] for r in (*k_band[a:a + 3], km_ref)], axis=0)
            v = jnp.concatenate([r[:, ks] for r in (*v_band[a:a + 3], vm_ref)], axis=0)
            qt = jnp.concatenate([qt_ref[(h0 + g) * HEAD_DIM:(h0 + g + 1) * HEAD_DIM, qcols] for g in range(Q_PER_KV)],
                                 axis=1)
            s_scr[...] = jnp.dot(k, qt, preferred_element_type=F32)
            inv = []
            for g in range(Q_PER_KV):
                cols = slice(g * BLOCK, (g + 1) * BLOCK)
                sink = sink_ref[h0 + g] * LOG2_E
                m = jnp.full((1, BLOCK), sink, F32)
                for c in range(4):
                    m = jnp.maximum(m, jnp.max(scores(c, cols), axis=0, keepdims=True))
                den = jnp.exp2(sink - m)
                for c in range(4):
                    p = jnp.exp2(scores(c, cols) - m)
                    den = den + jnp.sum(p, axis=0, keepdims=True)
                    p_scr[key_blocks[c], cols] = p.astype(BF16)
                inv.append(1.0 / den)
            ot = lax.dot_general(v, p_scr[...], (((0,), (0,)), ((), ())), preferred_element_type=F32)
            for g in range(Q_PER_KV):
                ot_ref[(h0 + g) * HEAD_DIM:(h0 + g + 1) * HEAD_DIM, qcols] = (
                    ot[:, g * BLOCK:(g + 1) * BLOCK] * inv[g]).astype(BF16)


def _attn_call(sink, proj, proj_t, proj_meta, *, batch, nb, name):
    rows = proj.shape[0]
    kw = N_KV_HEADS * HEAD_DIM
    qw = N_Q_HEADS * HEAD_DIM
    kcol, vcol = C_AK // kw, C_AV // kw
    qblocks = max(q for q in range(1, ATTN_MAX_QBLOCKS + 1) if nb % q == 0)
    qt = qblocks * BLOCK

    def band_spec(off, colblk):
        def imap(b, i, sink_ref):
            return (b * nb + jnp.clip(i * qblocks + off, 0, nb - 1), colblk)
        return pl.BlockSpec((BLOCK, kw), imap)

    offs = range(-1, qblocks + 1)
    grid_spec = pltpu.PrefetchScalarGridSpec(
        num_scalar_prefetch=1,
        grid=(batch, nb // qblocks),
        in_specs=[pl.BlockSpec((None, qw, qt), lambda b, i, s: (b, F_AQ // qw, i))]
        + [band_spec(o, kcol) for o in offs] + [band_spec(o, vcol) for o in offs]
        + [pl.BlockSpec((BLOCK, kw), lambda b, i, s: (0, kcol)), pl.BlockSpec((BLOCK, kw), lambda b, i, s: (0, vcol))],
        out_specs=pl.BlockSpec((None, qw, qt), lambda b, i, s: (b, 0, i)),
        scratch_shapes=[pltpu.VMEM((4 * BLOCK, BLOCK), F32), pltpu.VMEM((4 * BLOCK, Q_PER_KV * BLOCK), F32),
                        pltpu.VMEM((4 * BLOCK, Q_PER_KV * BLOCK), BF16)],
    )
    return pl.pallas_call(
        functools.partial(_attn_kernel, nb=nb, qblocks=qblocks),
        grid_spec=grid_spec,
        out_shape=jax.ShapeDtypeStruct((batch, qw, rows // batch), BF16),
        compiler_params=_cparams(("arbitrary", "arbitrary")),
        name=name,
    )(sink, proj_t, *([proj] * (2 * len(offs))), proj_meta, proj_meta)


MLSTM_HEAD_GROUP = 4


def _scan_lanes(x, op, fill, rev):
    lane = lax.broadcasted_iota(I32, x.shape, 1)
    sh = 1
    while sh < BLOCK:
        if rev:
            y = jnp.where(lane < BLOCK - sh, pltpu.roll(x, BLOCK - sh, axis=1), fill)
        else:
            y = jnp.where(lane >= sh, pltpu.roll(x, sh, axis=1), fill)
        x = op(x, y)
        sh *= 2
    return x


def _rows_to_columns(x, eye):
    hi = x.astype(BF16)
    r1 = x - hi.astype(F32)
    mid = r1.astype(BF16)
    lo = (r1 - mid.astype(F32)).astype(BF16)
    move = lambda a: lax.dot_general(eye, a, _NT, preferred_element_type=F32)
    return move(hi) + move(mid) + move(lo)


def _chunk_terms(gts, bias, m_prev, rev, pad_front):
    d = M_HEADS if rev else 0
    li = jnp.concatenate([gt[d:d + M_HEADS, :] + bias[d:d + M_HEADS, :] for gt in gts], axis=0)
    lf = jnp.concatenate([gt[2 * M_HEADS + d:3 * M_HEADS + d, :] + bias[2 * M_HEADS + d:3 * M_HEADS + d, :] for gt in gts],
                         axis=0)
    lf = jax.nn.log_sigmoid(lf)
    if pad_front:
        real = lax.broadcasted_iota(I32, li.shape, 1) >= FRONT_PAD
        li, lf = jnp.where(real, li, NEG), jnp.where(real, lf, 0.0)
    b = _scan_lanes(lf, jnp.add, 0.0, rev)
    tot = jnp.sum(lf, axis=1, keepdims=True)
    u = li - b
    big_m = jnp.maximum(m_prev, _scan_lanes(u, jnp.maximum, -jnp.inf, rev))
    inter = jnp.exp(m_prev - big_m)
    e_neg_m = jnp.exp(-(b + big_m))
    m_new = jnp.maximum(tot + m_prev, tot + jnp.max(u, axis=1, keepdims=True))
    w = jnp.exp(tot + u - m_new)
    sp = jnp.exp(tot + m_prev - m_new)
    return u, big_m, inter, e_neg_m, w, sp, m_new


def _state_update(h, k, vt, w, sp, ct_scr, n_scr):
    wh = w[h:h + 1, :]
    lhs = jnp.concatenate([(vt.astype(F32) * wh).astype(BF16), jnp.broadcast_to(wh, (16, BLOCK)).astype(BF16)], axis=0)
    upd = jnp.dot(lhs, k, preferred_element_type=F32)
    sph = sp[h:h + 1, :]
    ct_scr[h] = sph * ct_scr[h] + upd[:M_V_DIM]
    n_scr[h:h + 1, :] = sph * n_scr[h:h + 1, :] + upd[M_V_DIM:M_V_DIM + 1]


def _mlstm_kernel(*refs, rev, final):
    (qt_ref, k_ref, vt_ref, gt_ref, gtn_ref, bias_ref, eye_ref), refs = refs[:7], refs[7:]
    if not rev:
        (km_ref, vtm_ref, gtm_ref), refs = refs[:3], refs[3:]
    if final:
        (hft_ref, mot_ref, ngb_ref), refs = refs[:3], refs[3:]
    ht_ref, ct_scr, n_scr, t_scr, ucol_scr = refs
    bias = bias_ref[...]
    eye = eye_ref[...]
    n_seq = qt_ref.shape[0]
    n_heads = n_seq * M_HEADS

    def stash_terms(g_ref, m_prev):
        gts = [g_ref[b, 0:4 * M_HEADS, :] for b in range(n_seq)]
        u, big_m, inter, e_neg_m, w, sp, m_new = _chunk_terms(gts, bias, m_prev, rev, False)
        ucol_scr[...] = _rows_to_columns(u, eye)
        for slot, val in enumerate((big_m, inter, e_neg_m, w, sp, m_new)):
            t_scr[slot] = jnp.broadcast_to(val, (n_heads, LANES))

    @pl.when(pl.program_id(0) == 0)
    def _():
        ct_scr[...] = jnp.zeros_like(ct_scr)
        n_scr[...] = jnp.zeros_like(n_scr)
        m0 = jnp.zeros((n_heads, 1), F32)
        if not rev:
            _, _, _, _, w, sp, m0 = _chunk_terms([gtm_ref[0, 0:4 * M_HEADS, :]] * n_seq, bias, m0, rev, True)
            for hh in range(n_heads):
                h = hh % M_HEADS
                _state_update(hh, km_ref[:, h * M_QK_DIM:(h + 1) * M_QK_DIM],
                              vtm_ref[0, h * M_V_DIM:(h + 1) * M_V_DIM, :], w, sp, ct_scr, n_scr)
        stash_terms(gt_ref, m0)

    big_m, inter, e_neg_m, w = t_scr[0], t_scr[1], t_scr[2], t_scr[3]
    sp, m_new = t_scr[4][:, 0:1], t_scr[5][:, 0:1]
    u_col = ucol_scr[...]
    s_idx = lax.broadcasted_iota(I32, (BLOCK, BLOCK), 0)
    t_idx = lax.broadcasted_iota(I32, (BLOCK, BLOCK), 1)
    causal = (s_idx >= t_idx) if rev else (s_idx <= t_idx)
    qk_cols = lambda h: slice((h % M_HEADS) * M_QK_DIM, (h % M_HEADS + 1) * M_QK_DIM)
    v_rows = lambda h: slice((h % M_HEADS) * M_V_DIM, (h % M_HEADS + 1) * M_V_DIM)
    qt_of = lambda h: qt_ref[h // M_HEADS, qk_cols(h), :]
    k_of = lambda h: k_ref[h // M_HEADS, :, qk_cols(h)]
    vt_of = lambda h: vt_ref[h // M_HEADS, v_rows(h), :]
    for g0 in range(0, n_heads, MLSTM_HEAD_GROUP):
        hs = range(g0, g0 + MLSTM_HEAD_GROUP)
        st = {h: jnp.dot(jnp.concatenate([k_of(h), jnp.broadcast_to(n_scr[h:h + 1, :], (16, M_QK_DIM)).astype(BF16)], axis=0),
                         qt_of(h), preferred_element_type=F32) for h in hs}
        decay = {h: jnp.exp(jnp.where(causal, u_col[:, h:h + 1] - big_m[h:h + 1, :], NEG)) for h in hs}
        sd = {h: st[h][:BLOCK] * decay[h] for h in hs}
        rden = {}
        for h in hs:
            den = inter[h:h + 1, :] * st[h][BLOCK:BLOCK + 1] + jnp.sum(sd[h], axis=0, keepdims=True)
            rden[h] = 1.0 / jnp.maximum(jnp.abs(den), e_neg_m[h:h + 1, :])
        hv = {}
        for h in hs:
            lhs = jnp.concatenate([vt_of(h), ct_scr[h].astype(BF16)], axis=1)
            rhs = jnp.concatenate([sd[h].astype(BF16), (qt_of(h).astype(F32) * inter[h:h + 1, :]).astype(BF16)], axis=0)
            hv[h] = jnp.dot(lhs, rhs, preferred_element_type=F32) * rden[h]
        for h in hs:
            b, rows = h // M_HEADS, v_rows(h)
            x = hv[h]
            if final:
                x = x + hft_ref[b, rows, :].astype(F32)
                mu = jnp.mean(x, axis=0, keepdims=True)
                xc = x - mu
                var = jnp.mean(xc * xc, axis=0, keepdims=True)
                x = xc * lax.rsqrt(var + M_NORM_EPS) * ngb_ref[rows, :] * jax.nn.sigmoid(mot_ref[b, rows, :].astype(F32))
            ht_ref[b, rows, :] = x.astype(ht_ref.dtype)
        for h in hs:
            _state_update(h, k_of(h), vt_of(h), w, sp, ct_scr, n_scr)
    stash_terms(gtn_ref, m_new)


def _mlstm_call(proj, proj_t, gates_t, bias, eye, meta, final_in, *, batch, nc, rev, name):
    final = final_in is not None
    qw, vw = M_HEADS * M_QK_DIM, M_HEADS * M_V_DIM
    seq = nc * BLOCK
    proj3 = proj.reshape(batch, seq, N_TOK)
    chunk = (lambda i: nc - 1 - i) if rev else (lambda i: i)
    feat_spec = lambda width, off: pl.BlockSpec((batch, width, BLOCK), lambda i: (0, off // width, chunk(i)))

    in_specs = [
        feat_spec(qw, F_MQ),
        pl.BlockSpec((batch, BLOCK, qw), lambda i: (0, chunk(i), C_MK // qw)),
        feat_spec(vw, F_MV),
        pl.BlockSpec((batch, LANES, BLOCK), lambda i: (0, 0, chunk(i))),
        pl.BlockSpec((batch, LANES, BLOCK), lambda i: (0, 0, chunk(jnp.minimum(i + 1, nc - 1)))),
        pl.BlockSpec((4 * M_HEADS, 1), lambda i: (0, 0)),
        pl.BlockSpec((BLOCK, BLOCK), lambda i: (0, 0)),
    ]
    args = [proj_t, proj3, proj_t, gates_t, gates_t, bias, eye]
    if not rev:
        proj_m, proj_tm, gates_tm = meta
        in_specs += [
            pl.BlockSpec((BLOCK, qw), lambda i: (0, C_MK // qw)),
            pl.BlockSpec((1, vw, BLOCK), lambda i: (0, F_MV // vw, 0)),
            pl.BlockSpec((1, LANES, BLOCK), lambda i: (0, 0, 0)),
        ]
        args += [proj_m, proj_tm, gates_tm]
    if final:
        h_fwd_t, norm_g_b = final_in
        in_specs += [feat_spec(vw, 0), feat_spec(vw, F_MO), pl.BlockSpec((vw, BLOCK), lambda i: (0, 0))]
        args += [h_fwd_t, proj_t, norm_g_b]
    n_heads = batch * M_HEADS
    return pl.pallas_call(
        functools.partial(_mlstm_kernel, rev=rev, final=final),
        grid=(nc,),
        in_specs=in_specs,
        out_specs=feat_spec(vw, 0),
        out_shape=jax.ShapeDtypeStruct((batch, vw, seq), BF16),
        scratch_shapes=[pltpu.VMEM((n_heads, M_V_DIM, M_QK_DIM), F32), pltpu.VMEM((n_heads, LANES), F32),
                        pltpu.VMEM((6, n_heads, LANES), F32), pltpu.VMEM((BLOCK, n_heads), F32)],
        compiler_params=_cparams(("arbitrary",)),
        name=name,
    )(*args)


def _merge_kernel(at_ref, mt_ref, ga_ref, gb_ref, wa_ref, wb_ref, o_ref):
    tn = (((0,), (0,)), ((), ()))
    ya = lax.dot_general(at_ref[...], wa_ref[...], tn, preferred_element_type=F32)
    yb = lax.dot_general(mt_ref[...], wb_ref[...], tn, preferred_element_type=F32)
    o_ref[...] = (jax.nn.sigmoid(ga_ref[...].astype(F32)) * ya + jax.nn.sigmoid(gb_ref[...].astype(F32)) * yb).astype(BF16)


def _merge_call(attn_t, mem_t, proj, wa, wb, *, tm, name):
    rows = proj.shape[0]
    row_spec = pl.BlockSpec((tm, D_MODEL), lambda i: (i, 0))
    tiles_per_seq = attn_t.shape[2] // tm
    feat_spec = pl.BlockSpec((None, D_MODEL, tm), lambda i: (i // tiles_per_seq, 0, i % tiles_per_seq))
    w_spec = pl.BlockSpec((D_MODEL, D_MODEL), lambda i: (0, 0), pipeline_mode=pl.Buffered(1))
    return pl.pallas_call(
        _merge_kernel,
        grid=(rows // tm,),
        in_specs=[feat_spec, feat_spec,
                  pl.BlockSpec((tm, D_MODEL), lambda i: (i, C_GA // D_MODEL)),
                  pl.BlockSpec((tm, D_MODEL), lambda i: (i, C_GB // D_MODEL)),
                  w_spec, w_spec],
        out_specs=row_spec,
        out_shape=jax.ShapeDtypeStruct((rows, D_MODEL), BF16),
        compiler_params=_cparams(("arbitrary",)),
        name=name,
    )(attn_t, mem_t, proj, proj, wa, wb)


def _post_kernel(mg_ref, x_ref, eg_ref, eb_ref, wo_ref, g1_ref, b1_ref, wr_hi_ref, wr_lo_ref, rb_ref, low_ref, cnt_in_ref,
                 x1_joint_ref, x1_ref, ri_ref, rw_ref, cnt_ref):
    del x1_joint_ref

    @pl.when(pl.program_id(0) == 0)
    def _():
        cnt_ref[...] = cnt_in_ref[...]

    x0 = _ln_rows(x_ref[...], eg_ref[...], eb_ref[...], LN_EPS)
    y = jnp.dot(mg_ref[...], wo_ref[...], preferred_element_type=F32)
    x1 = _ln_rows(ALPHA * x0 + y, g1_ref[...], b1_ref[...], LN_EPS)
    x1_ref[...] = x1

    hi = x1.astype(BF16)
    lo = (x1 - hi.astype(F32)).astype(BF16)
    logits = (jnp.dot(hi, wr_hi_ref[...], preferred_element_type=F32)
              + jnp.dot(lo, wr_hi_ref[...], preferred_element_type=F32)
              + jnp.dot(hi, wr_lo_ref[...], preferred_element_type=F32)) + rb_ref[...]
    lane = lax.broadcasted_iota(I32, logits.shape, 1)
    big = jnp.int32(LANES)

    def first_max(valid):
        vmax = jnp.max(jnp.where(valid, logits, -jnp.inf), axis=1, keepdims=True)
        idx = jnp.min(jnp.where(valid & (logits == vmax), lane, big), axis=1, keepdims=True)
        return vmax, idx

    is_grp = lane < N_GROUPS
    gmax, grp = first_max(is_grp)
    p_grp = 1.0 / jnp.sum(jnp.where(is_grp, jnp.exp(logits - gmax), 0.0), axis=1, keepdims=True)
    e_lo = N_GROUPS + grp * EXPERTS_PER_GROUP
    in_grp = (lane >= e_lo) & (lane < e_lo + EXPERTS_PER_GROUP)
    l1, i1 = first_max(in_grp)
    l2, i2 = first_max(in_grp & (lane != i1))
    t2 = jnp.exp(l2 - l1)
    w1 = p_grp / (1.0 + t2)
    w2 = p_grp * t2 / (1.0 + t2)
    e1 = i1 - N_GROUPS
    e2 = i2 - N_GROUPS

    oh1 = (lane == e1).astype(F32)
    oh2 = (lane == e2).astype(F32)
    cnt = oh1 + oh2
    before = jnp.dot(low_ref[...], cnt.astype(BF16), preferred_element_type=F32) + cnt_ref[...]
    r1 = jnp.sum(oh1 * before, axis=1, keepdims=True).astype(I32)
    r2 = jnp.sum(oh2 * before, axis=1, keepdims=True).astype(I32)
    cnt_ref[...] = cnt_ref[...] + jnp.sum(cnt, axis=0, keepdims=True)
    ri_ref[...] = jnp.where(lane == 0, e1, jnp.where(lane == 1, e2, jnp.where(lane == 2, r1, jnp.where(lane == 3, r2, 0))))
    rw_ref[...] = jnp.where(lane == 0, w1, jnp.where(lane == 1, w2, 0.0))


def _post_call(merged, x, eg, eb, wo, g1, b1, wr_hi, wr_lo, rb, low, cnt_in, x1_joint, *, tm, row_off, total_rows, name):
    rows = merged.shape[0]
    off = row_off // tm
    row_spec = pl.BlockSpec((tm, D_MODEL), lambda i: (i, 0))
    vec_spec = pl.BlockSpec((1, D_MODEL), lambda i: (0, 0))
    lane_spec = pl.BlockSpec((1, LANES), lambda i: (0, 0))
    rt_spec = pl.BlockSpec((D_MODEL, LANES), lambda i: (0, 0))
    assert row_off % tm == 0 and x1_joint.shape == (total_rows, D_MODEL)
    return pl.pallas_call(
        _post_kernel,
        grid=(rows // tm,),
        in_specs=[row_spec, row_spec, vec_spec, vec_spec,
                  pl.BlockSpec((D_MODEL, D_MODEL), lambda i: (0, 0), pipeline_mode=pl.Buffered(1)),
                  vec_spec, vec_spec, rt_spec, rt_spec, lane_spec,
                  pl.BlockSpec((tm, tm), lambda i: (0, 0)), lane_spec, pl.BlockSpec(memory_space=pl.ANY)],
        out_specs=[pl.BlockSpec((tm, D_MODEL), lambda i: (i + off, 0)),
                   pl.BlockSpec((tm, LANES), lambda i: (i, 0)), pl.BlockSpec((tm, LANES), lambda i: (i, 0)),
                   lane_spec],
        out_shape=[jax.ShapeDtypeStruct((total_rows, D_MODEL), F32), jax.ShapeDtypeStruct((rows, LANES), I32),
                   jax.ShapeDtypeStruct((rows, LANES), F32), jax.ShapeDtypeStruct((1, LANES), F32)],
        input_output_aliases={12: 0},
        compiler_params=_cparams(("arbitrary",)),
        name=name,
    )(merged, x, eg, eb, wo, g1, b1, wr_hi, wr_lo, rb, low, cnt_in, x1_joint)


def _row_copy(src, s, dst, d, sem):
    return pltpu.make_async_copy(src.at[pl.ds(s, 1), :], dst.at[pl.ds(d, 1), :], sem)


GATHER_SLOTS = 3
GATE_STAGE = GATHER_SLOTS


def _expert_kernel(blk_e_ref, n_act_ref, tok_ref, x_ref, wg_ref, wu_ref, wd_ref, y_ref, xbuf, sem):
    del blk_e_ref
    i = pl.program_id(0)
    n_act = n_act_ref[0]

    def start(blk, r):
        slot = blk % GATHER_SLOTS
        _row_copy(x_ref, tok_ref[blk * SLOT_BLOCK + r], xbuf.at[slot], r, sem.at[slot]).start()

    def wait(blk):
        slot = blk % GATHER_SLOTS
        pltpu.make_async_copy(x_ref.at[pl.ds(0, SLOT_BLOCK), :], xbuf.at[slot], sem.at[slot]).wait()

    @pl.when(i == 0)
    def _():
        lax.fori_loop(0, SLOT_BLOCK, lambda r, c: (start(0, r), start(1, r), c)[2], 0, unroll=4)

    @pl.when(i < n_act)
    def _():
        wait(i)
        xb = xbuf[i % GATHER_SLOTS].astype(BF16)
        half = D_EXPERT // 2
        xbuf[GATE_STAGE, :, 0:half] = jnp.dot(xb, wg_ref[0, :, 0:half], preferred_element_type=F32)
        for r in range(SLOT_BLOCK):
            start(i + 2, r)
        gate_hi = jnp.dot(xb, wg_ref[0, :, half:], preferred_element_type=F32)
        up = jnp.dot(xb, wu_ref[0], preferred_element_type=F32)
        gate = jnp.concatenate([xbuf[GATE_STAGE, :, 0:half], gate_hi], axis=1)
        hdn = (gate * jax.nn.sigmoid(gate) * up).astype(BF16)
        y_ref[...] = jnp.dot(hdn, wd_ref[0], preferred_element_type=F32)

    @pl.when(i >= n_act)
    def _():
        @pl.when(i == n_act)
        def _():
            wait(i)
            wait(i + 1)
        y_ref[...] = jnp.zeros_like(y_ref)


def _expert_call(blk_e, n_act, row_tok, x1, wg, wu, wd, *, name):
    n_blocks = blk_e.shape[0]
    grid_spec = pltpu.PrefetchScalarGridSpec(
        num_scalar_prefetch=3,
        grid=(n_blocks,),
        in_specs=[
            pl.BlockSpec(memory_space=pl.ANY),
            pl.BlockSpec((1, D_MODEL, D_EXPERT), lambda i, be, na, tok: (be[i], 0, 0)),
            pl.BlockSpec((1, D_MODEL, D_EXPERT), lambda i, be, na, tok: (be[i], 0, 0)),
            pl.BlockSpec((1, D_EXPERT, D_MODEL), lambda i, be, na, tok: (be[i], 0, 0)),
        ],
        out_specs=pl.BlockSpec((SLOT_BLOCK, D_MODEL), lambda i, be, na, tok: (i, 0)),
        scratch_shapes=[pltpu.VMEM((GATHER_SLOTS + 1, SLOT_BLOCK, D_MODEL), F32),
                        pltpu.SemaphoreType.DMA((GATHER_SLOTS,))],
    )
    return pl.pallas_call(
        _expert_kernel,
        grid_spec=grid_spec,
        out_shape=jax.ShapeDtypeStruct((n_blocks * SLOT_BLOCK, D_MODEL), F32),
        compiler_params=_cparams(("arbitrary",)),
        name=name,
    )(blk_e, n_act, row_tok, x1, wg, wu, wd)


COMBINE_STAGE = 4


def _combine_kernel(dest_ref, x1_ref, rw_ref, g2_ref, b2_ref, yr_ref, o_ref, ybuf, sem, *, tile):
    i = pl.program_id(0)
    last = pl.num_programs(0) - 1

    def start(t, par, r, k):
        _row_copy(yr_ref, dest_ref[2 * (t * tile + r) + k], ybuf.at[2 * par + k], r, sem.at[par]).start()

    def wait(par):
        for k in range(2):
            pltpu.make_async_copy(yr_ref.at[pl.ds(0, tile), :], ybuf.at[2 * par + k], sem.at[par]).wait()

    @pl.when(i == 0)
    def _():
        lax.fori_loop(0, tile, lambda r, c: (start(i, 0, r, 0), start(i, 0, r, 1), c)[2], 0, unroll=4)

    par = i % 2
    wait(par)
    rw = rw_ref[...]
    s = ALPHA * x1_ref[...] + rw[:, 0:1] * ybuf[2 * par] + rw[:, 1:2] * ybuf[2 * par + 1]
    ybuf[COMBINE_STAGE, 0:8, :] = jnp.broadcast_to(b2_ref[...], (8, D_MODEL))
    nxt = jnp.minimum(i + 1, last)
    for r in range(tile):
        for k in range(2):
            start(nxt, 1 - par, r, k)
    o_ref[...] = _ln_rows(s, g2_ref[...], ybuf[COMBINE_STAGE, 0:1, :], LN_EPS)

    @pl.when(i == last)
    def _():
        wait(1 - par)


def _combine_call(dest, x1, rw, g2, b2, yr, *, row_off, tile, name):
    rows = rw.shape[0]
    off = row_off // tile
    vec_spec = pl.BlockSpec((1, D_MODEL), lambda i, d: (0, 0))
    grid_spec = pltpu.PrefetchScalarGridSpec(
        num_scalar_prefetch=1,
        grid=(rows // tile,),
        in_specs=[pl.BlockSpec((tile, D_MODEL), lambda i, d: (i + off, 0)),
                  pl.BlockSpec((tile, LANES), lambda i, d: (i, 0)), vec_spec, vec_spec,
                  pl.BlockSpec(memory_space=pl.ANY)],
        out_specs=pl.BlockSpec((tile, D_MODEL), lambda i, d: (i, 0)),
        scratch_shapes=[pltpu.VMEM((5, tile, D_MODEL), F32), pltpu.SemaphoreType.DMA((2,))],
    )
    return pl.pallas_call(
        functools.partial(_combine_kernel, tile=tile),
        grid_spec=grid_spec,
        out_shape=jax.ShapeDtypeStruct((rows, D_MODEL), F32),
        compiler_params=_cparams(("arbitrary",)),
        name=name,
    )(dest.reshape(-1), x1, rw, g2, b2, yr)


def _rope_tables(pos):
    half = HEAD_DIM // 2
    inv = ROPE_THETA ** (-jnp.arange(half, dtype=F32) / half)
    ang = pos.astype(F32)[:, None] * inv[None, :]
    cos, sin = jnp.cos(ang), jnp.sin(ang)
    return jnp.concatenate([cos, cos], axis=1), jnp.concatenate([-sin, sin], axis=1), cos.T, sin.T


def kernel(x_prompt, x_sample, meta_tokens, ln_emb_g, ln_emb_b, w_in, attn_sink, m_gate_bias, m_norm_g, w_br_attn, w_br_mlstm, w_out, ln1_g, ln1_b, w_router_group, b_router_group, w_router_expert, b_router_expert, w_expert_gate, w_expert_up, w_expert_down, ln2_g, ln2_b):
    assert w_in.shape[0] == DEPTH == 1
    row = lambda v: v.reshape(1, -1).astype(F32)
    eg, eb = row(ln_emb_g), row(ln_emb_b)
    g1, b1, g2, b2 = row(ln1_g[0]), row(ln1_b[0]), row(ln2_g[0]), row(ln2_b[0])

    w = w_in[0]
    sec = {}
    off = 0
    for nm, width in (("aq", 2048), ("ak", 512), ("av", 512), ("mq", 1024), ("mk", 1024), ("mv", 2048), ("mo", 2048),
                      ("mg", 32), ("bg", 4096)):
        sec[nm] = w[:, off:off + width]
        off += width
    w_tok = jnp.concatenate([sec[n] for n in ("bg", "mk", "ak", "av")], axis=1).astype(BF16)
    w_feat_t = jnp.concatenate([sec[n] for n in ("mv", "mo", "aq", "mq")], axis=1).T.astype(BF16)
    w_gate_t = jnp.pad(sec["mg"], ((0, 0), (0, LANES - 4 * M_HEADS))).T.astype(BF16)
    wa, wb, wo = w_br_attn[0].astype(BF16), w_br_mlstm[0].astype(BF16), w_out[0].astype(BF16)
    wr = jnp.pad(jnp.concatenate([w_router_group[0], w_router_expert[0]], axis=1),
                 ((0, 0), (0, LANES - N_GROUPS - N_EXPERTS)))
    wr_hi = wr.astype(BF16)
    wr_lo = (wr - wr_hi.astype(F32)).astype(BF16)
    rb = jnp.pad(jnp.concatenate([b_router_group[0], b_router_expert[0]]), (0, LANES - N_GROUPS - N_EXPERTS)).reshape(1, LANES)
    weg, weu, wed = w_expert_gate[0].astype(BF16), w_expert_up[0].astype(BF16), w_expert_down[0].astype(BF16)
    gate_bias = m_gate_bias[0].reshape(4 * M_HEADS, 1).astype(F32)
    norm_g_b = jnp.broadcast_to(m_norm_g[0].astype(F32)[:, None], (M_HEADS * M_V_DIM, BLOCK))
    sink = attn_sink[0].astype(F32)
    eye = jnp.eye(BLOCK, dtype=BF16)

    xm = jnp.pad(meta_tokens.astype(F32), ((FRONT_PAD, 0), (0, 0)))
    meta = _proj_call(xm, eg, eb, _rope_tables(jnp.arange(BLOCK) - FRONT_PAD), w_tok, w_feat_t, w_gate_t, tm=BLOCK,
                      zero_front=True, name="proj_meta")
    proj_m = meta[0]

    groups = []
    cnt = jnp.zeros((1, LANES), F32)
    total_rows = x_prompt.shape[0] * x_prompt.shape[1] + x_sample.shape[0] * x_sample.shape[1]
    x1 = jnp.zeros((total_rows, D_MODEL), F32)
    row_off = 0
    for gi, xg in enumerate((x_prompt, x_sample)):
        batch, seq, _ = xg.shape
        assert seq % BLOCK == 0
        nb = seq // BLOCK
        rows = batch * seq
        x2 = xg.reshape(rows, D_MODEL)
        tm = _pick_tile(seq, 1024)
        proj, proj_t, gates_t = _proj_call(x2, eg, eb, _rope_tables(jnp.arange(seq) + N_META), w_tok, w_feat_t, w_gate_t,
                                           tm=tm, zero_front=False, name=f"proj_g{gi}")
        attn_t = _attn_call(sink, proj, proj_t, proj_m, batch=batch, nb=nb, name=f"attn_g{gi}")
        h_fwd_t = _mlstm_call(proj, proj_t, gates_t, gate_bias, eye, meta, None,
                              batch=batch, nc=nb, rev=False, name=f"mlstm_fwd_g{gi}")
        mem_t = _mlstm_call(proj, proj_t, gates_t, gate_bias, eye, None, (h_fwd_t, norm_g_b),
                            batch=batch, nc=nb, rev=True, name=f"mlstm_bwd_g{gi}")
        tp = _pick_tile(seq, 512)
        merged = _merge_call(attn_t, mem_t, proj, wa, wb, tm=tp, name=f"merge_g{gi}")
        low = (jnp.arange(tp)[None, :] < jnp.arange(tp)[:, None]).astype(BF16)
        x1, ri, rw, cnt = _post_call(merged, x2, eg, eb, wo, g1, b1, wr_hi, wr_lo, rb, low, cnt, x1, tm=tp,
                                     row_off=row_off, total_rows=total_rows, name=f"post_g{gi}")
        groups.append((xg.shape, row_off, ri[:, :4], rw))
        row_off += rows

    counts = cnt[0, :N_EXPERTS].astype(I32)
    padded = (counts + SLOT_BLOCK - 1) // SLOT_BLOCK * SLOT_BLOCK
    pad_ends = jnp.cumsum(padded)
    pad_starts = pad_ends - padded
    n_blocks = -(-2 * total_rows // SLOT_BLOCK) + N_EXPERTS
    blk_start = jnp.arange(n_blocks, dtype=I32) * SLOT_BLOCK
    blk_e = jnp.minimum(jnp.sum((pad_ends[None, :] <= blk_start[:, None]).astype(I32), axis=1), N_EXPERTS - 1)
    n_act = (pad_ends[-1:] // SLOT_BLOCK).astype(I32)
    ri_all = jnp.concatenate([g[2] for g in groups], axis=0)
    dest = pad_starts[ri_all[:, 0:2]] + ri_all[:, 2:4]
    tok = jnp.broadcast_to(jnp.arange(total_rows, dtype=I32)[:, None], dest.shape)
    row_tok = jnp.zeros(((n_blocks + 1) * SLOT_BLOCK,), I32).at[dest.reshape(-1)].set(tok.reshape(-1), unique_indices=True)

    yr = _expert_call(blk_e, n_act, row_tok, x1, weg, weu, wed, name="experts")
    outs = []
    for gi, (shape, off, _, rw) in enumerate(groups):
        rows = rw.shape[0]
        out = _combine_call(dest[off:off + rows], x1, rw, g2, b2, yr, row_off=off, tile=_pick_tile(rows, 512),
                            name=f"combine_g{gi}")
        outs.append(out.reshape(shape))
    return tuple(outs)
```

```python
import functools

import jax
import jax.numpy as jnp
from jax import lax
from jax.experimental import pallas as pl
from jax.experimental.pallas import tpu as pltpu

F32 = jnp.float32
BF16 = jnp.bfloat16
I32 = jnp.int32

D_MODEL = 2048
DEPTH = 1
N_META = 16
BLOCK = 128
FRONT_PAD = BLOCK - N_META
N_Q_HEADS = 16
N_KV_HEADS = 4
HEAD_DIM = 128
Q_PER_KV = 4
WINDOW = 128
ROPE_THETA = 10000.0
M_HEADS = 8
M_QK_DIM = 128
M_V_DIM = 256
M_NORM_EPS = 1e-6
N_GROUPS = 4
EXPERTS_PER_GROUP = 8
N_EXPERTS = 32
D_EXPERT = 1024
ALPHA = (2 * DEPTH) ** 0.25
LN_EPS = 1e-5
NEG = -1e30

C_GA, C_GB, C_MK, C_AK, C_AV = 0, 2048, 4096, 5120, 5632
N_TOK = 6144
F_MV, F_MO, F_AQ, F_MQ = 0, 2048, 4096, 6144
N_FEAT = 7168
PROJ_TN = 1024
LANES = 128
SLOT_BLOCK = 256
VMEM_LIMIT = 56 * 1024 * 1024


def _pick_tile(n, pref):
    t = min(pref, n)
    t -= t % BLOCK
    while n % t:
        t -= BLOCK
    return t


def _cparams(sem, vmem=VMEM_LIMIT):
    return pltpu.CompilerParams(dimension_semantics=sem, vmem_limit_bytes=vmem)


def _ln_rows(x, g, b, eps):
    mu = jnp.mean(x, axis=-1, keepdims=True)
    xc = x - mu
    var = jnp.mean(xc * xc, axis=-1, keepdims=True)
    return xc * lax.rsqrt(var + eps) * g + b


def _rope_heads(acc, cos, sin, nheads, scale):
    outs = []
    for h in range(nheads):
        xh = acc[:, h * HEAD_DIM:(h + 1) * HEAD_DIM]
        o = xh * cos + pltpu.roll(xh, HEAD_DIM // 2, axis=1) * sin
        outs.append(o * scale)
    return outs


LOG2_E = 1.4426950408889634
ATTN_Q_SCALE = HEAD_DIM ** -0.5 * LOG2_E
PROJ_SPLIT = 1
N_TOK_TILES = N_TOK // PROJ_TN
N_FEAT_TILES = N_FEAT // PROJ_TN
_NT = (((1,), (1,)), ((), ()))


def _rope_heads_t(acc, cos_t, sin_t, scale):
    half = HEAD_DIM // 2
    outs = []
    for h in range(acc.shape[0] // HEAD_DIM):
        x1 = acc[h * HEAD_DIM:h * HEAD_DIM + half, :]
        x2 = acc[h * HEAD_DIM + half:(h + 1) * HEAD_DIM, :]
        outs += [(x1 * cos_t - x2 * sin_t) * scale, (x2 * cos_t + x1 * sin_t) * scale]
    return jnp.concatenate(outs, axis=0)


def _proj_kernel(x_ref, g_ref, b_ref, cos_ref, sin_ref, cost_ref, sint_ref, w_ref, wt_ref, wg_ref,
                 out_ref, outt_ref, gate_ref, u_scr, *, zero_front):
    j = pl.program_id(1)

    @pl.when(j == 0)
    def _():
        u = _ln_rows(x_ref[...], g_ref[...], b_ref[...], LN_EPS)
        if zero_front:
            row = lax.broadcasted_iota(I32, u.shape, 0)
            u = jnp.where(row >= FRONT_PAD, u, 0.0)
        ub = u.astype(BF16)
        u_scr[...] = ub
        gate_ref[...] = lax.dot_general(wg_ref[...], ub, _NT, preferred_element_type=F32)

    cw = PROJ_TN // PROJ_SPLIT
    is_tok = j < N_TOK_TILES
    is_k = j == C_AK // PROJ_TN
    jt = j - N_TOK_TILES
    is_aq = (jt >= F_AQ // PROJ_TN) & (jt < F_MQ // PROJ_TN)

    @pl.when(is_tok & jnp.logical_not(is_k))
    def _():
        for c in range(PROJ_SPLIT):
            cols = slice(c * cw, (c + 1) * cw)
            out_ref[:, cols] = jnp.dot(u_scr[...], w_ref[:, cols], preferred_element_type=F32).astype(BF16)

    @pl.when(is_k)
    def _():
        for c in range(PROJ_SPLIT):
            cols = slice(c * cw, (c + 1) * cw)
            acc = jnp.dot(u_scr[...], w_ref[:, cols], preferred_element_type=F32)
            n_rot = min(max(N_KV_HEADS * HEAD_DIM - c * cw, 0), cw) // HEAD_DIM
            if n_rot:
                heads = _rope_heads(acc, cos_ref[...], sin_ref[...], n_rot, 1.0)
                acc = jnp.concatenate(heads + ([acc[:, n_rot * HEAD_DIM:]] if n_rot * HEAD_DIM < cw else []), axis=1)
            out_ref[:, cols] = acc.astype(BF16)

    @pl.when(jnp.logical_not(is_tok) & is_aq)
    def _():
        for c in range(PROJ_SPLIT):
            rows = slice(c * cw, (c + 1) * cw)
            acc = lax.dot_general(wt_ref[rows, :], u_scr[...], _NT, preferred_element_type=F32)
            outt_ref[rows, :] = _rope_heads_t(acc, cost_ref[...], sint_ref[...], ATTN_Q_SCALE).astype(BF16)

    @pl.when(jnp.logical_not(is_tok) & jnp.logical_not(is_aq))
    def _():
        scale = jnp.where(jt == F_MQ // PROJ_TN, M_QK_DIM ** -0.5, 1.0)
        for c in range(PROJ_SPLIT):
            rows = slice(c * cw, (c + 1) * cw)
            acc = lax.dot_general(wt_ref[rows, :], u_scr[...], _NT, preferred_element_type=F32)
            outt_ref[rows, :] = (acc * scale).astype(BF16)


def _proj_call(x, g, b, rope, w_tok, w_feat_t, w_gate_t, *, tm, zero_front, name):
    rows = x.shape[0]
    cos, sin, cos_t, sin_t = rope
    seq = cos.shape[0]
    pos_tiles = seq // tm
    grid = (rows // tm, N_TOK_TILES + N_FEAT_TILES)
    tok_tile = lambda j: jnp.minimum(j, N_TOK_TILES - 1)
    feat_tile = lambda j: jnp.clip(j - N_TOK_TILES, 0, N_FEAT_TILES - 1)
    return pl.pallas_call(
        functools.partial(_proj_kernel, zero_front=zero_front),
        grid=grid,
        in_specs=[
            pl.BlockSpec((tm, D_MODEL), lambda i, j: (i, 0)),
            pl.BlockSpec((1, D_MODEL), lambda i, j: (0, 0)),
            pl.BlockSpec((1, D_MODEL), lambda i, j: (0, 0)),
            pl.BlockSpec((tm, HEAD_DIM), lambda i, j: (i % pos_tiles, 0)),
            pl.BlockSpec((tm, HEAD_DIM), lambda i, j: (i % pos_tiles, 0)),
            pl.BlockSpec((HEAD_DIM // 2, tm), lambda i, j: (0, i % pos_tiles)),
            pl.BlockSpec((HEAD_DIM // 2, tm), lambda i, j: (0, i % pos_tiles)),
            pl.BlockSpec((D_MODEL, PROJ_TN), lambda i, j: (0, tok_tile(j))),
            pl.BlockSpec((PROJ_TN, D_MODEL), lambda i, j: (feat_tile(j), 0)),
            pl.BlockSpec((LANES, D_MODEL), lambda i, j: (0, 0)),
        ],
        out_specs=[
            pl.BlockSpec((tm, PROJ_TN), lambda i, j: (i, tok_tile(j))),
            pl.BlockSpec((None, PROJ_TN, tm), lambda i, j: (i // pos_tiles, feat_tile(j), i % pos_tiles)),
            pl.BlockSpec((None, LANES, tm), lambda i, j: (i // pos_tiles, 0, i % pos_tiles)),
        ],
        out_shape=[jax.ShapeDtypeStruct((rows, N_TOK), BF16), jax.ShapeDtypeStruct((rows // seq, N_FEAT, seq), BF16),
                   jax.ShapeDtypeStruct((rows // seq, LANES, seq), F32)],
        scratch_shapes=[pltpu.VMEM((tm, D_MODEL), BF16)],
        compiler_params=_cparams(("arbitrary", "arbitrary")),
        name=name,
    )(x, g, b, cos, sin, cos_t, sin_t, w_tok, w_feat_t, w_gate_t)


ATTN_MAX_QBLOCKS = 4


def _attn_kernel(sink_ref, qt_ref, *refs, nb, qblocks):
    nband = qblocks + 2
    k_band, v_band = refs[:nband], refs[nband:2 * nband]
    km_ref, vm_ref, ot_ref, bias_scr, s_scr, p_scr = refs[2 * nband:]
    key = lax.broadcasted_iota(I32, (4 * BLOCK, BLOCK), 0)
    qpos = lax.broadcasted_iota(I32, (4 * BLOCK, BLOCK), 1)
    key_blocks = [slice(c * BLOCK, (c + 1) * BLOCK) for c in range(4)]

    def scores(c, cols):
        s = s_scr[key_blocks[c], cols]
        return s if c == 1 else s + bias_scr[key_blocks[c], :]

    for a in range(qblocks):
        blk = pl.program_id(1) * qblocks + a
        qcols = slice(a * BLOCK, (a + 1) * BLOCK)
        lo = jnp.where(blk > 0, 0, BLOCK)
        hi = jnp.where(blk < nb - 1, 3 * BLOCK, 2 * BLOCK)
        band = (jnp.abs(key - BLOCK - qpos) <= WINDOW) & (key >= lo) & (key < hi)
        bias_scr[...] = jnp.where(band | (key >= 3 * BLOCK + FRONT_PAD), 0.0, NEG)
        for kv in range(N_KV_HEADS):
            h0 = kv * Q_PER_KV
            ks = slice(kv * HEAD_DIM, (kv + 1) * HEAD_DIM)
            k = jnp.concatenate([r[:, ks] for r in (*k_band[a:a + 3], km_ref)], axis=0)
            v = jnp.concatenate([r[:, ks] for r in (*v_band[a:a + 3], vm_ref)], axis=0)
            qt = jnp.concatenate([qt_ref[(h0 + g) * HEAD_DIM:(h0 + g + 1) * HEAD_DIM, qcols] for g in range(Q_PER_KV)],
                                 axis=1)
            s_scr[...] = jnp.dot(k, qt, preferred_element_type=F32)
            inv = []
            for g in range(Q_PER_KV):
                cols = slice(g * BLOCK, (g + 1) * BLOCK)
                sink = sink_ref[h0 + g] * LOG2_E
                m = jnp.full((1, BLOCK), sink, F32)
                for c in range(4):
                    m = jnp.maximum(m, jnp.max(scores(c, cols), axis=0, keepdims=True))
                den = jnp.exp2(sink - m)
                for c in range(4):
                    p = jnp.exp2(scores(c, cols) - m)
                    den = den + jnp.sum(p, axis=0, keepdims=True)
                    p_scr[key_blocks[c], cols] = p.astype(BF16)
                inv.append(1.0 / den)
            ot = lax.dot_general(v, p_scr[...], (((0,), (0,)), ((), ())), preferred_element_type=F32)
            for g in range(Q_PER_KV):
                ot_ref[(h0 + g) * HEAD_DIM:(h0 + g + 1) * HEAD_DIM, qcols] = (
                    ot[:, g * BLOCK:(g + 1) * BLOCK] * inv[g]).astype(BF16)


def _attn_call(sink, proj, proj_t, proj_meta, *, batch, nb, name):
    rows = proj.shape[0]
    kw = N_KV_HEADS * HEAD_DIM
    qw = N_Q_HEADS * HEAD_DIM
    kcol, vcol = C_AK // kw, C_AV // kw
    qblocks = max(q for q in range(1, ATTN_MAX_QBLOCKS + 1) if nb % q == 0)
    qt = qblocks * BLOCK

    def band_spec(off, colblk):
        def imap(b, i, sink_ref):
            return (b * nb + jnp.clip(i * qblocks + off, 0, nb - 1), colblk)
        return pl.BlockSpec((BLOCK, kw), imap)

    offs = range(-1, qblocks + 1)
    grid_spec = pltpu.PrefetchScalarGridSpec(
        num_scalar_prefetch=1,
        grid=(batch, nb // qblocks),
        in_specs=[pl.BlockSpec((None, qw, qt), lambda b, i, s: (b, F_AQ // qw, i))]
        + [band_spec(o, kcol) for o in offs] + [band_spec(o, vcol) for o in offs]
        + [pl.BlockSpec((BLOCK, kw), lambda b, i, s: (0, kcol)), pl.BlockSpec((BLOCK, kw), lambda b, i, s: (0, vcol))],
        out_specs=pl.BlockSpec((None, qw, qt), lambda b, i, s: (b, 0, i)),
        scratch_shapes=[pltpu.VMEM((4 * BLOCK, BLOCK), F32), pltpu.VMEM((4 * BLOCK, Q_PER_KV * BLOCK), F32),
                        pltpu.VMEM((4 * BLOCK, Q_PER_KV * BLOCK), BF16)],
    )
    return pl.pallas_call(
        functools.partial(_attn_kernel, nb=nb, qblocks=qblocks),
        grid_spec=grid_spec,
        out_shape=jax.ShapeDtypeStruct((batch, qw, rows // batch), BF16),
        compiler_params=_cparams(("arbitrary", "arbitrary")),
        name=name,
    )(sink, proj_t, *([proj] * (2 * len(offs))), proj_meta, proj_meta)


MLSTM_HEAD_GROUP = 8


def _scan_lanes(x, op, fill, rev):
    lane = lax.broadcasted_iota(I32, x.shape, 1)
    sh = 1
    while sh < BLOCK:
        if rev:
            y = jnp.where(lane < BLOCK - sh, pltpu.roll(x, BLOCK - sh, axis=1), fill)
        else:
            y = jnp.where(lane >= sh, pltpu.roll(x, sh, axis=1), fill)
        x = op(x, y)
        sh *= 2
    return x


def _rows_to_columns(x, eye):
    hi = x.astype(BF16)
    r1 = x - hi.astype(F32)
    mid = r1.astype(BF16)
    lo = (r1 - mid.astype(F32)).astype(BF16)
    move = lambda a: lax.dot_general(eye, a, _NT, preferred_element_type=F32)
    return move(hi) + move(mid) + move(lo)


def _chunk_terms(gts, bias, m_prev, rev, pad_front):
    d = M_HEADS if rev else 0
    li = jnp.concatenate([gt[d:d + M_HEADS, :] + bias[d:d + M_HEADS, :] for gt in gts], axis=0)
    lf = jnp.concatenate([gt[2 * M_HEADS + d:3 * M_HEADS + d, :] + bias[2 * M_HEADS + d:3 * M_HEADS + d, :] for gt in gts],
                         axis=0)
    lf = jax.nn.log_sigmoid(lf)
    if pad_front:
        real = lax.broadcasted_iota(I32, li.shape, 1) >= FRONT_PAD
        li, lf = jnp.where(real, li, NEG), jnp.where(real, lf, 0.0)
    b = _scan_lanes(lf, jnp.add, 0.0, rev)
    tot = jnp.sum(lf, axis=1, keepdims=True)
    u = li - b
    big_m = jnp.maximum(m_prev, _scan_lanes(u, jnp.maximum, -jnp.inf, rev))
    inter = jnp.exp(m_prev - big_m)
    e_neg_m = jnp.exp(-(b + big_m))
    m_new = jnp.maximum(tot + m_prev, tot + jnp.max(u, axis=1, keepdims=True))
    w = jnp.exp(tot + u - m_new)
    sp = jnp.exp(tot + m_prev - m_new)
    return u, big_m, inter, e_neg_m, w, sp, m_new


def _state_update(h, k, vt, w, sp, ct_scr, n_scr):
    wh = w[h:h + 1, :]
    lhs = jnp.concatenate([(vt.astype(F32) * wh).astype(BF16), jnp.broadcast_to(wh, (16, BLOCK)).astype(BF16)], axis=0)
    upd = jnp.dot(lhs, k, preferred_element_type=F32)
    sph = sp[h:h + 1, :]
    ct_scr[h] = sph * ct_scr[h] + upd[:M_V_DIM]
    n_scr[h:h + 1, :] = sph * n_scr[h:h + 1, :] + upd[M_V_DIM:M_V_DIM + 1]


def _mlstm_kernel(*refs, rev, final):
    (qt_ref, k_ref, vt_ref, gt_ref, gtn_ref, bias_ref, eye_ref), refs = refs[:7], refs[7:]
    if not rev:
        (km_ref, vtm_ref, gtm_ref), refs = refs[:3], refs[3:]
    if final:
        (hft_ref, mot_ref, ngb_ref), refs = refs[:3], refs[3:]
    ht_ref, ct_scr, n_scr, t_scr, ucol_scr = refs
    bias = bias_ref[...]
    eye = eye_ref[...]
    n_seq = qt_ref.shape[0]
    n_heads = n_seq * M_HEADS

    def stash_terms(g_ref, m_prev):
        gts = [g_ref[b, 0:4 * M_HEADS, :] for b in range(n_seq)]
        u, big_m, inter, e_neg_m, w, sp, m_new = _chunk_terms(gts, bias, m_prev, rev, False)
        ucol_scr[...] = _rows_to_columns(u, eye)
        for slot, val in enumerate((big_m, inter, e_neg_m, w, sp, m_new)):
            t_scr[slot] = jnp.broadcast_to(val, (n_heads, LANES))

    @pl.when(pl.program_id(0) == 0)
    def _():
        ct_scr[...] = jnp.zeros_like(ct_scr)
        n_scr[...] = jnp.zeros_like(n_scr)
        m0 = jnp.zeros((n_heads, 1), F32)
        if not rev:
            _, _, _, _, w, sp, m0 = _chunk_terms([gtm_ref[0, 0:4 * M_HEADS, :]] * n_seq, bias, m0, rev, True)
            for hh in range(n_heads):
                h = hh % M_HEADS
                _state_update(hh, km_ref[:, h * M_QK_DIM:(h + 1) * M_QK_DIM],
                              vtm_ref[0, h * M_V_DIM:(h + 1) * M_V_DIM, :], w, sp, ct_scr, n_scr)
        stash_terms(gt_ref, m0)

    big_m, inter, e_neg_m, w = t_scr[0], t_scr[1], t_scr[2], t_scr[3]
    sp, m_new = t_scr[4][:, 0:1], t_scr[5][:, 0:1]
    u_col = ucol_scr[...]
    s_idx = lax.broadcasted_iota(I32, (BLOCK, BLOCK), 0)
    t_idx = lax.broadcasted_iota(I32, (BLOCK, BLOCK), 1)
    causal = (s_idx >= t_idx) if rev else (s_idx <= t_idx)
    qk_cols = lambda h: slice((h % M_HEADS) * M_QK_DIM, (h % M_HEADS + 1) * M_QK_DIM)
    v_rows = lambda h: slice((h % M_HEADS) * M_V_DIM, (h % M_HEADS + 1) * M_V_DIM)
    qt_of = lambda h: qt_ref[h // M_HEADS, qk_cols(h), :]
    k_of = lambda h: k_ref[h // M_HEADS, :, qk_cols(h)]
    vt_of = lambda h: vt_ref[h // M_HEADS, v_rows(h), :]
    for g0 in range(0, n_heads, MLSTM_HEAD_GROUP):
        hs = range(g0, g0 + MLSTM_HEAD_GROUP)
        st = {h: jnp.dot(jnp.concatenate([k_of(h), jnp.broadcast_to(n_scr[h:h + 1, :], (16, M_QK_DIM)).astype(BF16)], axis=0),
                         qt_of(h), preferred_element_type=F32) for h in hs}
        decay = {h: jnp.exp(jnp.where(causal, u_col[:, h:h + 1] - big_m[h:h + 1, :], NEG)) for h in hs}
        sd = {h: st[h][:BLOCK] * decay[h] for h in hs}
        rden = {}
        for h in hs:
            den = inter[h:h + 1, :] * st[h][BLOCK:BLOCK + 1] + jnp.sum(sd[h], axis=0, keepdims=True)
            rden[h] = 1.0 / jnp.maximum(jnp.abs(den), e_neg_m[h:h + 1, :])
        hv = {}
        for h in hs:
            lhs = jnp.concatenate([vt_of(h), ct_scr[h].astype(BF16)], axis=1)
            rhs = jnp.concatenate([sd[h].astype(BF16), (qt_of(h).astype(F32) * inter[h:h + 1, :]).astype(BF16)], axis=0)
            hv[h] = jnp.dot(lhs, rhs, preferred_element_type=F32) * rden[h]
        for h in hs:
            b, rows = h // M_HEADS, v_rows(h)
            x = hv[h]
            if final:
                x = x + hft_ref[b, rows, :].astype(F32)
                mu = jnp.mean(x, axis=0, keepdims=True)
                xc = x - mu
                var = jnp.mean(xc * xc, axis=0, keepdims=True)
                x = xc * lax.rsqrt(var + M_NORM_EPS) * ngb_ref[rows, :] * jax.nn.sigmoid(mot_ref[b, rows, :].astype(F32))
            ht_ref[b, rows, :] = x.astype(ht_ref.dtype)
        for h in hs:
            _state_update(h, k_of(h), vt_of(h), w, sp, ct_scr, n_scr)
    stash_terms(gtn_ref, m_new)


def _mlstm_call(proj, proj_t, gates_t, bias, eye, meta, final_in, *, batch, nc, rev, name):
    final = final_in is not None
    qw, vw = M_HEADS * M_QK_DIM, M_HEADS * M_V_DIM
    seq = nc * BLOCK
    proj3 = proj.reshape(batch, seq, N_TOK)
    chunk = (lambda i: nc - 1 - i) if rev else (lambda i: i)
    feat_spec = lambda width, off: pl.BlockSpec((batch, width, BLOCK), lambda i: (0, off // width, chunk(i)))

    in_specs = [
        feat_spec(qw, F_MQ),
        pl.BlockSpec((batch, BLOCK, qw), lambda i: (0, chunk(i), C_MK // qw)),
        feat_spec(vw, F_MV),
        pl.BlockSpec((batch, LANES, BLOCK), lambda i: (0, 0, chunk(i))),
        pl.BlockSpec((batch, LANES, BLOCK), lambda i: (0, 0, chunk(jnp.minimum(i + 1, nc - 1)))),
        pl.BlockSpec((4 * M_HEADS, 1), lambda i: (0, 0)),
        pl.BlockSpec((BLOCK, BLOCK), lambda i: (0, 0)),
    ]
    args = [proj_t, proj3, proj_t, gates_t, gates_t, bias, eye]
    if not rev:
        proj_m, proj_tm, gates_tm = meta
        in_specs += [
            pl.BlockSpec((BLOCK, qw), lambda i: (0, C_MK // qw)),
            pl.BlockSpec((1, vw, BLOCK), lambda i: (0, F_MV // vw, 0)),
            pl.BlockSpec((1, LANES, BLOCK), lambda i: (0, 0, 0)),
        ]
        args += [proj_m, proj_tm, gates_tm]
    if final:
        h_fwd_t, norm_g_b = final_in
        in_specs += [feat_spec(vw, 0), feat_spec(vw, F_MO), pl.BlockSpec((vw, BLOCK), lambda i: (0, 0))]
        args += [h_fwd_t, proj_t, norm_g_b]
    n_heads = batch * M_HEADS
    return pl.pallas_call(
        functools.partial(_mlstm_kernel, rev=rev, final=final),
        grid=(nc,),
        in_specs=in_specs,
        out_specs=feat_spec(vw, 0),
        out_shape=jax.ShapeDtypeStruct((batch, vw, seq), BF16),
        scratch_shapes=[pltpu.VMEM((n_heads, M_V_DIM, M_QK_DIM), F32), pltpu.VMEM((n_heads, LANES), F32),
                        pltpu.VMEM((6, n_heads, LANES), F32), pltpu.VMEM((BLOCK, n_heads), F32)],
        compiler_params=_cparams(("arbitrary",)),
        name=name,
    )(*args)


def _merge_kernel(at_ref, mt_ref, ga_ref, gb_ref, wa_ref, wb_ref, o_ref):
    tn = (((0,), (0,)), ((), ()))
    ya = lax.dot_general(at_ref[...], wa_ref[...], tn, preferred_element_type=F32)
    yb = lax.dot_general(mt_ref[...], wb_ref[...], tn, preferred_element_type=F32)
    o_ref[...] = (jax.nn.sigmoid(ga_ref[...].astype(F32)) * ya + jax.nn.sigmoid(gb_ref[...].astype(F32)) * yb).astype(BF16)


def _merge_call(attn_t, mem_t, proj, wa, wb, *, tm, name):
    rows = proj.shape[0]
    row_spec = pl.BlockSpec((tm, D_MODEL), lambda i: (i, 0))
    tiles_per_seq = attn_t.shape[2] // tm
    feat_spec = pl.BlockSpec((None, D_MODEL, tm), lambda i: (i // tiles_per_seq, 0, i % tiles_per_seq))
    w_spec = pl.BlockSpec((D_MODEL, D_MODEL), lambda i: (0, 0), pipeline_mode=pl.Buffered(1))
    return pl.pallas_call(
        _merge_kernel,
        grid=(rows // tm,),
        in_specs=[feat_spec, feat_spec,
                  pl.BlockSpec((tm, D_MODEL), lambda i: (i, C_GA // D_MODEL)),
                  pl.BlockSpec((tm, D_MODEL), lambda i: (i, C_GB // D_MODEL)),
                  w_spec, w_spec],
        out_specs=row_spec,
        out_shape=jax.ShapeDtypeStruct((rows, D_MODEL), BF16),
        compiler_params=_cparams(("arbitrary",)),
        name=name,
    )(attn_t, mem_t, proj, proj, wa, wb)


def _post_kernel(mg_ref, x_ref, eg_ref, eb_ref, wo_ref, g1_ref, b1_ref, wr_hi_ref, wr_lo_ref, rb_ref, low_ref, cnt_in_ref,
                 x1_joint_ref, x1_ref, ri_ref, rw_ref, cnt_ref):
    del x1_joint_ref

    @pl.when(pl.program_id(0) == 0)
    def _():
        cnt_ref[...] = cnt_in_ref[...]

    x0 = _ln_rows(x_ref[...], eg_ref[...], eb_ref[...], LN_EPS)
    y = jnp.dot(mg_ref[...], wo_ref[...], preferred_element_type=F32)
    x1 = _ln_rows(ALPHA * x0 + y, g1_ref[...], b1_ref[...], LN_EPS)
    x1_ref[...] = x1

    hi = x1.astype(BF16)
    lo = (x1 - hi.astype(F32)).astype(BF16)
    logits = (jnp.dot(hi, wr_hi_ref[...], preferred_element_type=F32)
              + jnp.dot(lo, wr_hi_ref[...], preferred_element_type=F32)
              + jnp.dot(hi, wr_lo_ref[...], preferred_element_type=F32)) + rb_ref[...]
    lane = lax.broadcasted_iota(I32, logits.shape, 1)
    big = jnp.int32(LANES)

    def first_max(valid):
        vmax = jnp.max(jnp.where(valid, logits, -jnp.inf), axis=1, keepdims=True)
        idx = jnp.min(jnp.where(valid & (logits == vmax), lane, big), axis=1, keepdims=True)
        return vmax, idx

    is_grp = lane < N_GROUPS
    gmax, grp = first_max(is_grp)
    p_grp = 1.0 / jnp.sum(jnp.where(is_grp, jnp.exp(logits - gmax), 0.0), axis=1, keepdims=True)
    e_lo = N_GROUPS + grp * EXPERTS_PER_GROUP
    in_grp = (lane >= e_lo) & (lane < e_lo + EXPERTS_PER_GROUP)
    l1, i1 = first_max(in_grp)
    l2, i2 = first_max(in_grp & (lane != i1))
    t2 = jnp.exp(l2 - l1)
    w1 = p_grp / (1.0 + t2)
    w2 = p_grp * t2 / (1.0 + t2)
    e1 = i1 - N_GROUPS
    e2 = i2 - N_GROUPS

    oh1 = (lane == e1).astype(F32)
    oh2 = (lane == e2).astype(F32)
    cnt = oh1 + oh2
    before = jnp.dot(low_ref[...], cnt.astype(BF16), preferred_element_type=F32) + cnt_ref[...]
    r1 = jnp.sum(oh1 * before, axis=1, keepdims=True).astype(I32)
    r2 = jnp.sum(oh2 * before, axis=1, keepdims=True).astype(I32)
    cnt_ref[...] = cnt_ref[...] + jnp.sum(cnt, axis=0, keepdims=True)
    ri_ref[...] = jnp.where(lane == 0, e1, jnp.where(lane == 1, e2, jnp.where(lane == 2, r1, jnp.where(lane == 3, r2, 0))))
    rw_ref[...] = jnp.where(lane == 0, w1, jnp.where(lane == 1, w2, 0.0))


def _post_call(merged, x, eg, eb, wo, g1, b1, wr_hi, wr_lo, rb, low, cnt_in, x1_joint, *, tm, row_off, total_rows, name):
    rows = merged.shape[0]
    off = row_off // tm
    row_spec = pl.BlockSpec((tm, D_MODEL), lambda i: (i, 0))
    vec_spec = pl.BlockSpec((1, D_MODEL), lambda i: (0, 0))
    lane_spec = pl.BlockSpec((1, LANES), lambda i: (0, 0))
    rt_spec = pl.BlockSpec((D_MODEL, LANES), lambda i: (0, 0))
    assert row_off % tm == 0 and x1_joint.shape == (total_rows, D_MODEL)
    return pl.pallas_call(
        _post_kernel,
        grid=(rows // tm,),
        in_specs=[row_spec, row_spec, vec_spec, vec_spec,
                  pl.BlockSpec((D_MODEL, D_MODEL), lambda i: (0, 0), pipeline_mode=pl.Buffered(1)),
                  vec_spec, vec_spec, rt_spec, rt_spec, lane_spec,
                  pl.BlockSpec((tm, tm), lambda i: (0, 0)), lane_spec, pl.BlockSpec(memory_space=pl.ANY)],
        out_specs=[pl.BlockSpec((tm, D_MODEL), lambda i: (i + off, 0)),
                   pl.BlockSpec((tm, LANES), lambda i: (i, 0)), pl.BlockSpec((tm, LANES), lambda i: (i, 0)),
                   lane_spec],
        out_shape=[jax.ShapeDtypeStruct((total_rows, D_MODEL), F32), jax.ShapeDtypeStruct((rows, LANES), I32),
                   jax.ShapeDtypeStruct((rows, LANES), F32), jax.ShapeDtypeStruct((1, LANES), F32)],
        input_output_aliases={12: 0},
        compiler_params=_cparams(("arbitrary",)),
        name=name,
    )(merged, x, eg, eb, wo, g1, b1, wr_hi, wr_lo, rb, low, cnt_in, x1_joint)


def _row_copy(src, s, dst, d, sem):
    return pltpu.make_async_copy(src.at[pl.ds(s, 1), :], dst.at[pl.ds(d, 1), :], sem)


GATHER_SLOTS = 3
GATE_STAGE = GATHER_SLOTS


def _expert_kernel(blk_e_ref, n_act_ref, tok_ref, x_ref, wg_ref, wu_ref, wd_ref, y_ref, xbuf, sem):
    del blk_e_ref
    i = pl.program_id(0)
    n_act = n_act_ref[0]

    def start(blk, r):
        slot = blk % GATHER_SLOTS
        _row_copy(x_ref, tok_ref[blk * SLOT_BLOCK + r], xbuf.at[slot], r, sem.at[slot]).start()

    def wait(blk):
        slot = blk % GATHER_SLOTS
        pltpu.make_async_copy(x_ref.at[pl.ds(0, SLOT_BLOCK), :], xbuf.at[slot], sem.at[slot]).wait()

    @pl.when(i == 0)
    def _():
        lax.fori_loop(0, SLOT_BLOCK, lambda r, c: (start(0, r), start(1, r), c)[2], 0, unroll=4)

    @pl.when(i < n_act)
    def _():
        wait(i)
        xb = xbuf[i % GATHER_SLOTS].astype(BF16)
        half = D_EXPERT // 2
        xbuf[GATE_STAGE, :, 0:half] = jnp.dot(xb, wg_ref[0, :, 0:half], preferred_element_type=F32)
        for r in range(SLOT_BLOCK):
            start(i + 2, r)
        gate_hi = jnp.dot(xb, wg_ref[0, :, half:], preferred_element_type=F32)
        up = jnp.dot(xb, wu_ref[0], preferred_element_type=F32)
        gate = jnp.concatenate([xbuf[GATE_STAGE, :, 0:half], gate_hi], axis=1)
        hdn = (gate * jax.nn.sigmoid(gate) * up).astype(BF16)
        y_ref[...] = jnp.dot(hdn, wd_ref[0], preferred_element_type=F32)

    @pl.when(i >= n_act)
    def _():
        @pl.when(i == n_act)
        def _():
            wait(i)
            wait(i + 1)
        y_ref[...] = jnp.zeros_like(y_ref)


def _expert_call(blk_e, n_act, row_tok, x1, wg, wu, wd, *, name):
    n_blocks = blk_e.shape[0]
    grid_spec = pltpu.PrefetchScalarGridSpec(
        num_scalar_prefetch=3,
        grid=(n_blocks,),
        in_specs=[
            pl.BlockSpec(memory_space=pl.ANY),
            pl.BlockSpec((1, D_MODEL, D_EXPERT), lambda i, be, na, tok: (be[i], 0, 0)),
            pl.BlockSpec((1, D_MODEL, D_EXPERT), lambda i, be, na, tok: (be[i], 0, 0)),
            pl.BlockSpec((1, D_EXPERT, D_MODEL), lambda i, be, na, tok: (be[i], 0, 0)),
        ],
        out_specs=pl.BlockSpec((SLOT_BLOCK, D_MODEL), lambda i, be, na, tok: (i, 0)),
        scratch_shapes=[pltpu.VMEM((GATHER_SLOTS + 1, SLOT_BLOCK, D_MODEL), F32),
                        pltpu.SemaphoreType.DMA((GATHER_SLOTS,))],
    )
    return pl.pallas_call(
        _expert_kernel,
        grid_spec=grid_spec,
        out_shape=jax.ShapeDtypeStruct((n_blocks * SLOT_BLOCK, D_MODEL), F32),
        compiler_params=_cparams(("arbitrary",)),
        name=name,
    )(blk_e, n_act, row_tok, x1, wg, wu, wd)


COMBINE_STAGE = 4


def _combine_kernel(dest_ref, x1_ref, rw_ref, g2_ref, b2_ref, yr_ref, o_ref, ybuf, sem, *, tile):
    i = pl.program_id(0)
    last = pl.num_programs(0) - 1

    def start(t, par, r, k):
        _row_copy(yr_ref, dest_ref[2 * (t * tile + r) + k], ybuf.at[2 * par + k], r, sem.at[par]).start()

    def wait(par):
        for k in range(2):
            pltpu.make_async_copy(yr_ref.at[pl.ds(0, tile), :], ybuf.at[2 * par + k], sem.at[par]).wait()

    @pl.when(i == 0)
    def _():
        lax.fori_loop(0, tile, lambda r, c: (start(i, 0, r, 0), start(i, 0, r, 1), c)[2], 0, unroll=4)

    par = i % 2
    wait(par)
    rw = rw_ref[...]
    s = ALPHA * x1_ref[...] + rw[:, 0:1] * ybuf[2 * par] + rw[:, 1:2] * ybuf[2 * par + 1]
    ybuf[COMBINE_STAGE, 0:8, :] = jnp.broadcast_to(b2_ref[...], (8, D_MODEL))
    nxt = jnp.minimum(i + 1, last)
    for r in range(tile):
        for k in range(2):
            start(nxt, 1 - par, r, k)
    o_ref[...] = _ln_rows(s, g2_ref[...], ybuf[COMBINE_STAGE, 0:1, :], LN_EPS)

    @pl.when(i == last)
    def _():
        wait(1 - par)


def _combine_call(dest, x1, rw, g2, b2, yr, *, row_off, tile, name):
    rows = rw.shape[0]
    off = row_off // tile
    vec_spec = pl.BlockSpec((1, D_MODEL), lambda i, d: (0, 0))
    grid_spec = pltpu.PrefetchScalarGridSpec(
        num_scalar_prefetch=1,
        grid=(rows // tile,),
        in_specs=[pl.BlockSpec((tile, D_MODEL), lambda i, d: (i + off, 0)),
                  pl.BlockSpec((tile, LANES), lambda i, d: (i, 0)), vec_spec, vec_spec,
                  pl.BlockSpec(memory_space=pl.ANY)],
        out_specs=pl.BlockSpec((tile, D_MODEL), lambda i, d: (i, 0)),
        scratch_shapes=[pltpu.VMEM((5, tile, D_MODEL), F32), pltpu.SemaphoreType.DMA((2,))],
    )
    return pl.pallas_call(
        functools.partial(_combine_kernel, tile=tile),
        grid_spec=grid_spec,
        out_shape=jax.ShapeDtypeStruct((rows, D_MODEL), F32),
        compiler_params=_cparams(("arbitrary",)),
        name=name,
    )(dest.reshape(-1), x1, rw, g2, b2, yr)


def _rope_tables(pos):
    half = HEAD_DIM // 2
    inv = ROPE_THETA ** (-jnp.arange(half, dtype=F32) / half)
    ang = pos.astype(F32)[:, None] * inv[None, :]
    cos, sin = jnp.cos(ang), jnp.sin(ang)
    return jnp.concatenate([cos, cos], axis=1), jnp.concatenate([-sin, sin], axis=1), cos.T, sin.T


def kernel(x_prompt, x_sample, meta_tokens, ln_emb_g, ln_emb_b, w_in, attn_sink, m_gate_bias, m_norm_g, w_br_attn, w_br_mlstm, w_out, ln1_g, ln1_b, w_router_group, b_router_group, w_router_expert, b_router_expert, w_expert_gate, w_expert_up, w_expert_down, ln2_g, ln2_b):
    assert w_in.shape[0] == DEPTH == 1
    row = lambda v: v.reshape(1, -1).astype(F32)
    eg, eb = row(ln_emb_g), row(ln_emb_b)
    g1, b1, g2, b2 = row(ln1_g[0]), row(ln1_b[0]), row(ln2_g[0]), row(ln2_b[0])

    w = w_in[0]
    sec = {}
    off = 0
    for nm, width in (("aq", 2048), ("ak", 512), ("av", 512), ("mq", 1024), ("mk", 1024), ("mv", 2048), ("mo", 2048),
                      ("mg", 32), ("bg", 4096)):
        sec[nm] = w[:, off:off + width]
        off += width
    w_tok = jnp.concatenate([sec[n] for n in ("bg", "mk", "ak", "av")], axis=1).astype(BF16)
    w_feat_t = jnp.concatenate([sec[n] for n in ("mv", "mo", "aq", "mq")], axis=1).T.astype(BF16)
    w_gate_t = jnp.pad(sec["mg"], ((0, 0), (0, LANES - 4 * M_HEADS))).T.astype(BF16)
    wa, wb, wo = w_br_attn[0].astype(BF16), w_br_mlstm[0].astype(BF16), w_out[0].astype(BF16)
    wr = jnp.pad(jnp.concatenate([w_router_group[0], w_router_expert[0]], axis=1),
                 ((0, 0), (0, LANES - N_GROUPS - N_EXPERTS)))
    wr_hi = wr.astype(BF16)
    wr_lo = (wr - wr_hi.astype(F32)).astype(BF16)
    rb = jnp.pad(jnp.concatenate([b_router_group[0], b_router_expert[0]]), (0, LANES - N_GROUPS - N_EXPERTS)).reshape(1, LANES)
    weg, weu, wed = w_expert_gate[0].astype(BF16), w_expert_up[0].astype(BF16), w_expert_down[0].astype(BF16)
    gate_bias = m_gate_bias[0].reshape(4 * M_HEADS, 1).astype(F32)
    norm_g_b = jnp.broadcast_to(m_norm_g[0].astype(F32)[:, None], (M_HEADS * M_V_DIM, BLOCK))
    sink = attn_sink[0].astype(F32)
    eye = jnp.eye(BLOCK, dtype=BF16)

    xm = jnp.pad(meta_tokens.astype(F32), ((FRONT_PAD, 0), (0, 0)))
    meta = _proj_call(xm, eg, eb, _rope_tables(jnp.arange(BLOCK) - FRONT_PAD), w_tok, w_feat_t, w_gate_t, tm=BLOCK,
                      zero_front=True, name="proj_meta")
    proj_m = meta[0]

    groups = []
    cnt = jnp.zeros((1, LANES), F32)
    total_rows = x_prompt.shape[0] * x_prompt.shape[1] + x_sample.shape[0] * x_sample.shape[1]
    x1 = jnp.zeros((total_rows, D_MODEL), F32)
    row_off = 0
    for gi, xg in enumerate((x_prompt, x_sample)):
        batch, seq, _ = xg.shape
        assert seq % BLOCK == 0
        nb = seq // BLOCK
        rows = batch * seq
        x2 = xg.reshape(rows, D_MODEL)
        tm = _pick_tile(seq, 1024)
        proj, proj_t, gates_t = _proj_call(x2, eg, eb, _rope_tables(jnp.arange(seq) + N_META), w_tok, w_feat_t, w_gate_t,
                                           tm=tm, zero_front=False, name=f"proj_g{gi}")
        attn_t = _attn_call(sink, proj, proj_t, proj_m, batch=batch, nb=nb, name=f"attn_g{gi}")
        h_fwd_t = _mlstm_call(proj, proj_t, gates_t, gate_bias, eye, meta, None,
                              batch=batch, nc=nb, rev=False, name=f"mlstm_fwd_g{gi}")
        mem_t = _mlstm_call(proj, proj_t, gates_t, gate_bias, eye, None, (h_fwd_t, norm_g_b),
                            batch=batch, nc=nb, rev=True, name=f"mlstm_bwd_g{gi}")
        tp = _pick_tile(seq, 512)
        merged = _merge_call(attn_t, mem_t, proj, wa, wb, tm=tp, name=f"merge_g{gi}")
        low = (jnp.arange(tp)[None, :] < jnp.arange(tp)[:, None]).astype(BF16)
        x1, ri, rw, cnt = _post_call(merged, x2, eg, eb, wo, g1, b1, wr_hi, wr_lo, rb, low, cnt, x1, tm=tp,
                                     row_off=row_off, total_rows=total_rows, name=f"post_g{gi}")
        groups.append((xg.shape, row_off, ri[:, :4], rw))
        row_off += rows

    counts = cnt[0, :N_EXPERTS].astype(I32)
    padded = (counts + SLOT_BLOCK - 1) // SLOT_BLOCK * SLOT_BLOCK
    pad_ends = jnp.cumsum(padded)
    pad_starts = pad_ends - padded
    n_blocks = -(-2 * total_rows // SLOT_BLOCK) + N_EXPERTS
    blk_start = jnp.arange(n_blocks, dtype=I32) * SLOT_BLOCK
    blk_e = jnp.minimum(jnp.sum((pad_ends[None, :] <= blk_start[:, None]).astype(I32), axis=1), N_EXPERTS - 1)
    n_act = (pad_ends[-1:] // SLOT_BLOCK).astype(I32)
    ri_all = jnp.concatenate([g[2] for g in groups], axis=0)
    dest = pad_starts[ri_all[:, 0:2]] + ri_all[:, 2:4]
    tok = jnp.broadcast_to(jnp.arange(total_rows, dtype=I32)[:, None], dest.shape)
    row_tok = jnp.zeros(((n_blocks + 1) * SLOT_BLOCK,), I32).at[dest.reshape(-1)].add(tok.reshape(-1))

    yr = _expert_call(blk_e, n_act, row_tok, x1, weg, weu, wed, name="experts")
    outs = []
    for gi, (shape, off, _, rw) in enumerate(groups):
        rows = rw.shape[0]
        out = _combine_call(dest[off:off + rows], x1, rw, g2, b2, yr, row_off=off, tile=_pick_tile(rows, 512),
                            name=f"combine_g{gi}")
        outs.append(out.reshape(shape))
    return tuple(outs)
```

```python
import functools

import jax
import jax.numpy as jnp
from jax import lax
from jax.experimental import pallas as pl
from jax.experimental.pallas import tpu as pltpu

F32 = jnp.float32
BF16 = jnp.bfloat16
I32 = jnp.int32

D_MODEL = 2048
DEPTH = 1
N_META = 16
BLOCK = 128
FRONT_PAD = BLOCK - N_META
N_Q_HEADS = 16
N_KV_HEADS = 4
HEAD_DIM = 128
Q_PER_KV = 4
WINDOW = 128
ROPE_THETA = 10000.0
M_HEADS = 8
M_QK_DIM = 128
M_V_DIM = 256
M_NORM_EPS = 1e-6
N_GROUPS = 4
EXPERTS_PER_GROUP = 8
N_EXPERTS = 32
D_EXPERT = 1024
ALPHA = (2 * DEPTH) ** 0.25
LN_EPS = 1e-5
NEG = -1e30

C_GA, C_GB, C_MK, C_AK, C_AV = 0, 2048, 4096, 5120, 5632
N_TOK = 6144
F_MV, F_MO, F_AQ, F_MQ = 0, 2048, 4096, 6144
N_FEAT = 7168
PROJ_TN = 1024
LANES = 128
SLOT_BLOCK = 256
VMEM_LIMIT = 56 * 1024 * 1024


def _pick_tile(n, pref):
    t = min(pref, n)
    t -= t % BLOCK
    while n % t:
        t -= BLOCK
    return t


def _cparams(sem, vmem=VMEM_LIMIT):
    return pltpu.CompilerParams(dimension_semantics=sem, vmem_limit_bytes=vmem)


def _ln_rows(x, g, b, eps):
    mu = jnp.mean(x, axis=-1, keepdims=True)
    xc = x - mu
    var = jnp.mean(xc * xc, axis=-1, keepdims=True)
    return xc * lax.rsqrt(var + eps) * g + b


def _rope_heads(acc, cos, sin, nheads, scale):
    outs = []
    for h in range(nheads):
        xh = acc[:, h * HEAD_DIM:(h + 1) * HEAD_DIM]
        o = xh * cos + pltpu.roll(xh, HEAD_DIM // 2, axis=1) * sin
        outs.append(o * scale)
    return outs


LOG2_E = 1.4426950408889634
ATTN_Q_SCALE = HEAD_DIM ** -0.5 * LOG2_E
PROJ_SPLIT = 1
N_TOK_TILES = N_TOK // PROJ_TN
N_FEAT_TILES = N_FEAT // PROJ_TN
_NT = (((1,), (1,)), ((), ()))


def _rope_heads_t(acc, cos_t, sin_t, scale):
    half = HEAD_DIM // 2
    outs = []
    for h in range(acc.shape[0] // HEAD_DIM):
        x1 = acc[h * HEAD_DIM:h * HEAD_DIM + half, :]
        x2 = acc[h * HEAD_DIM + half:(h + 1) * HEAD_DIM, :]
        outs += [(x1 * cos_t - x2 * sin_t) * scale, (x2 * cos_t + x1 * sin_t) * scale]
    return jnp.concatenate(outs, axis=0)


def _proj_kernel(x_ref, g_ref, b_ref, cos_ref, sin_ref, cost_ref, sint_ref, w_ref, wt_ref, wg_ref,
                 out_ref, outt_ref, gate_ref, u_scr, *, zero_front):
    j = pl.program_id(1)

    @pl.when(j == 0)
    def _():
        u = _ln_rows(x_ref[...], g_ref[...], b_ref[...], LN_EPS)
        if zero_front:
            row = lax.broadcasted_iota(I32, u.shape, 0)
            u = jnp.where(row >= FRONT_PAD, u, 0.0)
        ub = u.astype(BF16)
        u_scr[...] = ub
        gate_ref[...] = lax.dot_general(wg_ref[...], ub, _NT, preferred_element_type=F32)

    cw = PROJ_TN // PROJ_SPLIT
    is_tok = j < N_TOK_TILES
    is_k = j == C_AK // PROJ_TN
    jt = j - N_TOK_TILES
    is_aq = (jt >= F_AQ // PROJ_TN) & (jt < F_MQ // PROJ_TN)

    @pl.when(is_tok & jnp.logical_not(is_k))
    def _():
        for c in range(PROJ_SPLIT):
            cols = slice(c * cw, (c + 1) * cw)
            out_ref[:, cols] = jnp.dot(u_scr[...], w_ref[:, cols], preferred_element_type=F32).astype(BF16)

    @pl.when(is_k)
    def _():
        for c in range(PROJ_SPLIT):
            cols = slice(c * cw, (c + 1) * cw)
            acc = jnp.dot(u_scr[...], w_ref[:, cols], preferred_element_type=F32)
            n_rot = min(max(N_KV_HEADS * HEAD_DIM - c * cw, 0), cw) // HEAD_DIM
            if n_rot:
                heads = _rope_heads(acc, cos_ref[...], sin_ref[...], n_rot, 1.0)
                acc = jnp.concatenate(heads + ([acc[:, n_rot * HEAD_DIM:]] if n_rot * HEAD_DIM < cw else []), axis=1)
            out_ref[:, cols] = acc.astype(BF16)

    @pl.when(jnp.logical_not(is_tok) & is_aq)
    def _():
        for c in range(PROJ_SPLIT):
            rows = slice(c * cw, (c + 1) * cw)
            acc = lax.dot_general(wt_ref[rows, :], u_scr[...], _NT, preferred_element_type=F32)
            outt_ref[rows, :] = _rope_heads_t(acc, cost_ref[...], sint_ref[...], ATTN_Q_SCALE).astype(BF16)

    @pl.when(jnp.logical_not(is_tok) & jnp.logical_not(is_aq))
    def _():
        scale = jnp.where(jt == F_MQ // PROJ_TN, M_QK_DIM ** -0.5, 1.0)
        for c in range(PROJ_SPLIT):
            rows = slice(c * cw, (c + 1) * cw)
            acc = lax.dot_general(wt_ref[rows, :], u_scr[...], _NT, preferred_element_type=F32)
            outt_ref[rows, :] = (acc * scale).astype(BF16)


def _proj_call(x, g, b, rope, w_tok, w_feat_t, w_gate_t, *, tm, zero_front, name):
    rows = x.shape[0]
    cos, sin, cos_t, sin_t = rope
    seq = cos.shape[0]
    pos_tiles = seq // tm
    grid = (rows // tm, N_TOK_TILES + N_FEAT_TILES)
    tok_tile = lambda j: jnp.minimum(j, N_TOK_TILES - 1)
    feat_tile = lambda j: jnp.clip(j - N_TOK_TILES, 0, N_FEAT_TILES - 1)
    return pl.pallas_call(
        functools.partial(_proj_kernel, zero_front=zero_front),
        grid=grid,
        in_specs=[
            pl.BlockSpec((tm, D_MODEL), lambda i, j: (i, 0)),
            pl.BlockSpec((1, D_MODEL), lambda i, j: (0, 0)),
            pl.BlockSpec((1, D_MODEL), lambda i, j: (0, 0)),
            pl.BlockSpec((tm, HEAD_DIM), lambda i, j: (i % pos_tiles, 0)),
            pl.BlockSpec((tm, HEAD_DIM), lambda i, j: (i % pos_tiles, 0)),
            pl.BlockSpec((HEAD_DIM // 2, tm), lambda i, j: (0, i % pos_tiles)),
            pl.BlockSpec((HEAD_DIM // 2, tm), lambda i, j: (0, i % pos_tiles)),
            pl.BlockSpec((D_MODEL, PROJ_TN), lambda i, j: (0, tok_tile(j))),
            pl.BlockSpec((PROJ_TN, D_MODEL), lambda i, j: (feat_tile(j), 0)),
            pl.BlockSpec((LANES, D_MODEL), lambda i, j: (0, 0)),
        ],
        out_specs=[
            pl.BlockSpec((tm, PROJ_TN), lambda i, j: (i, tok_tile(j))),
            pl.BlockSpec((None, PROJ_TN, tm), lambda i, j: (i // pos_tiles, feat_tile(j), i % pos_tiles)),
            pl.BlockSpec((None, LANES, tm), lambda i, j: (i // pos_tiles, 0, i % pos_tiles)),
        ],
        out_shape=[jax.ShapeDtypeStruct((rows, N_TOK), BF16), jax.ShapeDtypeStruct((rows // seq, N_FEAT, seq), BF16),
                   jax.ShapeDtypeStruct((rows // seq, LANES, seq), F32)],
        scratch_shapes=[pltpu.VMEM((tm, D_MODEL), BF16)],
        compiler_params=_cparams(("arbitrary", "arbitrary")),
        name=name,
    )(x, g, b, cos, sin, cos_t, sin_t, w_tok, w_feat_t, w_gate_t)


ATTN_MAX_QBLOCKS = 4


def _attn_kernel(sink_ref, qt_ref, *refs, nb, qblocks):
    nband = qblocks + 2
    k_band, v_band = refs[:nband], refs[nband:2 * nband]
    km_ref, vm_ref, ot_ref, bias_scr, s_scr, p_scr = refs[2 * nband:]
    key = lax.broadcasted_iota(I32, (4 * BLOCK, BLOCK), 0)
    qpos = lax.broadcasted_iota(I32, (4 * BLOCK, BLOCK), 1)
    key_blocks = [slice(c * BLOCK, (c + 1) * BLOCK) for c in range(4)]

    def scores(c, cols):
        s = s_scr[key_blocks[c], cols]
        return s if c == 1 else s + bias_scr[key_blocks[c], :]

    for a in range(qblocks):
        blk = pl.program_id(1) * qblocks + a
        qcols = slice(a * BLOCK, (a + 1) * BLOCK)
        lo = jnp.where(blk > 0, 0, BLOCK)
        hi = jnp.where(blk < nb - 1, 3 * BLOCK, 2 * BLOCK)
        band = (jnp.abs(key - BLOCK - qpos) <= WINDOW) & (key >= lo) & (key < hi)
        bias_scr[...] = jnp.where(band | (key >= 3 * BLOCK + FRONT_PAD), 0.0, NEG)
        for kv in range(N_KV_HEADS):
            h0 = kv * Q_PER_KV
            ks = slice(kv * HEAD_DIM, (kv + 1) * HEAD_DIM)
            k = jnp.concatenate([r[:, ks] for r in (*k_band[a:a + 3], km_ref)], axis=0)
            v = jnp.concatenate([r[:, ks] for r in (*v_band[a:a + 3], vm_ref)], axis=0)
            qt = jnp.concatenate([qt_ref[(h0 + g) * HEAD_DIM:(h0 + g + 1) * HEAD_DIM, qcols] for g in range(Q_PER_KV)],
                                 axis=1)
            s_scr[...] = jnp.dot(k, qt, preferred_element_type=F32)
            inv = []
            for g in range(Q_PER_KV):
                cols = slice(g * BLOCK, (g + 1) * BLOCK)
                sink = sink_ref[h0 + g] * LOG2_E
                m = jnp.full((1, BLOCK), sink, F32)
                for c in range(4):
                    m = jnp.maximum(m, jnp.max(scores(c, cols), axis=0, keepdims=True))
                den = jnp.exp2(sink - m)
                for c in range(4):
                    p = jnp.exp2(scores(c, cols) - m)
                    den = den + jnp.sum(p, axis=0, keepdims=True)
                    p_scr[key_blocks[c], cols] = p.astype(BF16)
                inv.append(1.0 / den)
            ot = lax.dot_general(v, p_scr[...], (((0,), (0,)), ((), ())), preferred_element_type=F32)
            for g in range(Q_PER_KV):
                ot_ref[(h0 + g) * HEAD_DIM:(h0 + g + 1) * HEAD_DIM, qcols] = (
                    ot[:, g * BLOCK:(g + 1) * BLOCK] * inv[g]).astype(BF16)


def _attn_call(sink, proj, proj_t, proj_meta, *, batch, nb, name):
    rows = proj.shape[0]
    kw = N_KV_HEADS * HEAD_DIM
    qw = N_Q_HEADS * HEAD_DIM
    kcol, vcol = C_AK // kw, C_AV // kw
    qblocks = max(q for q in range(1, ATTN_MAX_QBLOCKS + 1) if nb % q == 0)
    qt = qblocks * BLOCK

    def band_spec(off, colblk):
        def imap(b, i, sink_ref):
            return (b * nb + jnp.clip(i * qblocks + off, 0, nb - 1), colblk)
        return pl.BlockSpec((BLOCK, kw), imap)

    offs = range(-1, qblocks + 1)
    grid_spec = pltpu.PrefetchScalarGridSpec(
        num_scalar_prefetch=1,
        grid=(batch, nb // qblocks),
        in_specs=[pl.BlockSpec((None, qw, qt), lambda b, i, s: (b, F_AQ // qw, i))]
        + [band_spec(o, kcol) for o in offs] + [band_spec(o, vcol) for o in offs]
        + [pl.BlockSpec((BLOCK, kw), lambda b, i, s: (0, kcol)), pl.BlockSpec((BLOCK, kw), lambda b, i, s: (0, vcol))],
        out_specs=pl.BlockSpec((None, qw, qt), lambda b, i, s: (b, 0, i)),
        scratch_shapes=[pltpu.VMEM((4 * BLOCK, BLOCK), F32), pltpu.VMEM((4 * BLOCK, Q_PER_KV * BLOCK), F32),
                        pltpu.VMEM((4 * BLOCK, Q_PER_KV * BLOCK), BF16)],
    )
    return pl.pallas_call(
        functools.partial(_attn_kernel, nb=nb, qblocks=qblocks),
        grid_spec=grid_spec,
        out_shape=jax.ShapeDtypeStruct((batch, qw, rows // batch), BF16),
        compiler_params=_cparams(("arbitrary", "arbitrary")),
        name=name,
    )(sink, proj_t, *([proj] * (2 * len(offs))), proj_meta, proj_meta)


MLSTM_HEAD_GROUP = 8


def _scan_lanes(x, op, fill, rev):
    lane = lax.broadcasted_iota(I32, x.shape, 1)
    sh = 1
    while sh < BLOCK:
        if rev:
            y = jnp.where(lane < BLOCK - sh, pltpu.roll(x, BLOCK - sh, axis=1), fill)
        else:
            y = jnp.where(lane >= sh, pltpu.roll(x, sh, axis=1), fill)
        x = op(x, y)
        sh *= 2
    return x


def _rows_to_columns(x, eye):
    hi = x.astype(BF16)
    r1 = x - hi.astype(F32)
    mid = r1.astype(BF16)
    lo = (r1 - mid.astype(F32)).astype(BF16)
    move = lambda a: lax.dot_general(eye, a, _NT, preferred_element_type=F32)
    return move(hi) + move(mid) + move(lo)


def _chunk_terms(gts, bias, m_prev, rev, pad_front):
    d = M_HEADS if rev else 0
    li = jnp.concatenate([gt[d:d + M_HEADS, :] + bias[d:d + M_HEADS, :] for gt in gts], axis=0)
    lf = jnp.concatenate([gt[2 * M_HEADS + d:3 * M_HEADS + d, :] + bias[2 * M_HEADS + d:3 * M_HEADS + d, :] for gt in gts],
                         axis=0)
    lf = jax.nn.log_sigmoid(lf)
    if pad_front:
        real = lax.broadcasted_iota(I32, li.shape, 1) >= FRONT_PAD
        li, lf = jnp.where(real, li, NEG), jnp.where(real, lf, 0.0)
    b = _scan_lanes(lf, jnp.add, 0.0, rev)
    tot = jnp.sum(lf, axis=1, keepdims=True)
    u = li - b
    big_m = jnp.maximum(m_prev, _scan_lanes(u, jnp.maximum, -jnp.inf, rev))
    inter = jnp.exp(m_prev - big_m)
    e_neg_m = jnp.exp(-(b + big_m))
    m_new = jnp.maximum(tot + m_prev, tot + jnp.max(u, axis=1, keepdims=True))
    w = jnp.exp(tot + u - m_new)
    sp = jnp.exp(tot + m_prev - m_new)
    return u, big_m, inter, e_neg_m, w, sp, m_new


def _state_update(h, k, vt, w, sp, ct_scr, n_scr):
    wh = w[h:h + 1, :]
    lhs = jnp.concatenate([(vt.astype(F32) * wh).astype(BF16), jnp.broadcast_to(wh, (16, BLOCK)).astype(BF16)], axis=0)
    upd = jnp.dot(lhs, k, preferred_element_type=F32)
    sph = sp[h:h + 1, :]
    ct_scr[h] = sph * ct_scr[h] + upd[:M_V_DIM]
    n_scr[h:h + 1, :] = sph * n_scr[h:h + 1, :] + upd[M_V_DIM:M_V_DIM + 1]


def _mlstm_kernel(*refs, rev, final):
    (qt_ref, k_ref, vt_ref, gt_ref, gtn_ref, bias_ref, eye_ref), refs = refs[:7], refs[7:]
    if not rev:
        (km_ref, vtm_ref, gtm_ref), refs = refs[:3], refs[3:]
    if final:
        (hft_ref, mot_ref, ngb_ref), refs = refs[:3], refs[3:]
    ht_ref, ct_scr, n_scr, t_scr, ucol_scr = refs
    bias = bias_ref[...]
    eye = eye_ref[...]
    n_seq = qt_ref.shape[0]
    n_heads = n_seq * M_HEADS

    def stash_terms(g_ref, m_prev):
        gts = [g_ref[b, 0:4 * M_HEADS, :] for b in range(n_seq)]
        u, big_m, inter, e_neg_m, w, sp, m_new = _chunk_terms(gts, bias, m_prev, rev, False)
        ucol_scr[...] = _rows_to_columns(u, eye)
        for slot, val in enumerate((big_m, inter, e_neg_m, w, sp, m_new)):
            t_scr[slot] = jnp.broadcast_to(val, (n_heads, LANES))

    @pl.when(pl.program_id(0) == 0)
    def _():
        ct_scr[...] = jnp.zeros_like(ct_scr)
        n_scr[...] = jnp.zeros_like(n_scr)
        m0 = jnp.zeros((n_heads, 1), F32)
        if not rev:
            _, _, _, _, w, sp, m0 = _chunk_terms([gtm_ref[0, 0:4 * M_HEADS, :]] * n_seq, bias, m0, rev, True)
            for hh in range(n_heads):
                h = hh % M_HEADS
                _state_update(hh, km_ref[:, h * M_QK_DIM:(h + 1) * M_QK_DIM],
                              vtm_ref[0, h * M_V_DIM:(h + 1) * M_V_DIM, :], w, sp, ct_scr, n_scr)
        stash_terms(gt_ref, m0)

    big_m, inter, e_neg_m, w = t_scr[0], t_scr[1], t_scr[2], t_scr[3]
    sp, m_new = t_scr[4][:, 0:1], t_scr[5][:, 0:1]
    u_col = ucol_scr[...]
    s_idx = lax.broadcasted_iota(I32, (BLOCK, BLOCK), 0)
    t_idx = lax.broadcasted_iota(I32, (BLOCK, BLOCK), 1)
    causal = (s_idx >= t_idx) if rev else (s_idx <= t_idx)
    qk_cols = lambda h: slice((h % M_HEADS) * M_QK_DIM, (h % M_HEADS + 1) * M_QK_DIM)
    v_rows = lambda h: slice((h % M_HEADS) * M_V_DIM, (h % M_HEADS + 1) * M_V_DIM)
    qt_of = lambda h: qt_ref[h // M_HEADS, qk_cols(h), :]
    k_of = lambda h: k_ref[h // M_HEADS, :, qk_cols(h)]
    vt_of = lambda h: vt_ref[h // M_HEADS, v_rows(h), :]
    for g0 in range(0, n_heads, MLSTM_HEAD_GROUP):
        hs = range(g0, g0 + MLSTM_HEAD_GROUP)
        st = {h: jnp.dot(jnp.concatenate([k_of(h), jnp.broadcast_to(n_scr[h:h + 1, :], (16, M_QK_DIM)).astype(BF16)], axis=0),
                         qt_of(h), preferred_element_type=F32) for h in hs}
        decay = {h: jnp.exp(jnp.where(causal, u_col[:, h:h + 1] - big_m[h:h + 1, :], NEG)) for h in hs}
        sd = {h: st[h][:BLOCK] * decay[h] for h in hs}
        rden = {}
        for h in hs:
            den = inter[h:h + 1, :] * st[h][BLOCK:BLOCK + 1] + jnp.sum(sd[h], axis=0, keepdims=True)
            rden[h] = 1.0 / jnp.maximum(jnp.abs(den), e_neg_m[h:h + 1, :])
        hv = {}
        for h in hs:
            lhs = jnp.concatenate([vt_of(h), ct_scr[h].astype(BF16)], axis=1)
            rhs = jnp.concatenate([sd[h].astype(BF16), (qt_of(h).astype(F32) * inter[h:h + 1, :]).astype(BF16)], axis=0)
            hv[h] = jnp.dot(lhs, rhs, preferred_element_type=F32) * rden[h]
        for h in hs:
            b, rows = h // M_HEADS, v_rows(h)
            x = hv[h]
            if final:
                x = x + hft_ref[b, rows, :].astype(F32)
                mu = jnp.mean(x, axis=0, keepdims=True)
                xc = x - mu
                var = jnp.mean(xc * xc, axis=0, keepdims=True)
                x = xc * lax.rsqrt(var + M_NORM_EPS) * ngb_ref[rows, :] * jax.nn.sigmoid(mot_ref[b, rows, :].astype(F32))
            ht_ref[b, rows, :] = x.astype(ht_ref.dtype)
        for h in hs:
            _state_update(h, k_of(h), vt_of(h), w, sp, ct_scr, n_scr)
    stash_terms(gtn_ref, m_new)


def _mlstm_call(proj, proj_t, gates_t, bias, eye, meta, final_in, *, batch, nc, rev, name):
    final = final_in is not None
    qw, vw = M_HEADS * M_QK_DIM, M_HEADS * M_V_DIM
    seq = nc * BLOCK
    proj3 = proj.reshape(batch, seq, N_TOK)
    chunk = (lambda i: nc - 1 - i) if rev else (lambda i: i)
    feat_spec = lambda width, off: pl.BlockSpec((batch, width, BLOCK), lambda i: (0, off // width, chunk(i)))

    in_specs = [
        feat_spec(qw, F_MQ),
        pl.BlockSpec((batch, BLOCK, qw), lambda i: (0, chunk(i), C_MK // qw)),
        feat_spec(vw, F_MV),
        pl.BlockSpec((batch, LANES, BLOCK), lambda i: (0, 0, chunk(i))),
        pl.BlockSpec((batch, LANES, BLOCK), lambda i: (0, 0, chunk(jnp.minimum(i + 1, nc - 1)))),
        pl.BlockSpec((4 * M_HEADS, 1), lambda i: (0, 0)),
        pl.BlockSpec((BLOCK, BLOCK), lambda i: (0, 0)),
    ]
    args = [proj_t, proj3, proj_t, gates_t, gates_t, bias, eye]
    if not rev:
        proj_m, proj_tm, gates_tm = meta
        in_specs += [
            pl.BlockSpec((BLOCK, qw), lambda i: (0, C_MK // qw)),
            pl.BlockSpec((1, vw, BLOCK), lambda i: (0, F_MV // vw, 0)),
            pl.BlockSpec((1, LANES, BLOCK), lambda i: (0, 0, 0)),
        ]
        args += [proj_m, proj_tm, gates_tm]
    if final:
        h_fwd_t, norm_g_b = final_in
        in_specs += [feat_spec(vw, 0), feat_spec(vw, F_MO), pl.BlockSpec((vw, BLOCK), lambda i: (0, 0))]
        args += [h_fwd_t, proj_t, norm_g_b]
    n_heads = batch * M_HEADS
    return pl.pallas_call(
        functools.partial(_mlstm_kernel, rev=rev, final=final),
        grid=(nc,),
        in_specs=in_specs,
        out_specs=feat_spec(vw, 0),
        out_shape=jax.ShapeDtypeStruct((batch, vw, seq), BF16),
        scratch_shapes=[pltpu.VMEM((n_heads, M_V_DIM, M_QK_DIM), F32), pltpu.VMEM((n_heads, LANES), F32),
                        pltpu.VMEM((6, n_heads, LANES), F32), pltpu.VMEM((BLOCK, n_heads), F32)],
        compiler_params=_cparams(("arbitrary",)),
        name=name,
    )(*args)


def _merge_kernel(at_ref, mt_ref, ga_ref, gb_ref, wa_ref, wb_ref, o_ref):
    tn = (((0,), (0,)), ((), ()))
    ya = lax.dot_general(at_ref[...], wa_ref[...], tn, preferred_element_type=F32)
    yb = lax.dot_general(mt_ref[...], wb_ref[...], tn, preferred_element_type=F32)
    o_ref[...] = (jax.nn.sigmoid(ga_ref[...].astype(F32)) * ya + jax.nn.sigmoid(gb_ref[...].astype(F32)) * yb).astype(BF16)


def _merge_call(attn_t, mem_t, proj, wa, wb, *, tm, name):
    rows = proj.shape[0]
    row_spec = pl.BlockSpec((tm, D_MODEL), lambda i: (i, 0))
    tiles_per_seq = attn_t.shape[2] // tm
    feat_spec = pl.BlockSpec((None, D_MODEL, tm), lambda i: (i // tiles_per_seq, 0, i % tiles_per_seq))
    w_spec = pl.BlockSpec((D_MODEL, D_MODEL), lambda i: (0, 0), pipeline_mode=pl.Buffered(1))
    return pl.pallas_call(
        _merge_kernel,
        grid=(rows // tm,),
        in_specs=[feat_spec, feat_spec,
                  pl.BlockSpec((tm, D_MODEL), lambda i: (i, C_GA // D_MODEL)),
                  pl.BlockSpec((tm, D_MODEL), lambda i: (i, C_GB // D_MODEL)),
                  w_spec, w_spec],
        out_specs=row_spec,
        out_shape=jax.ShapeDtypeStruct((rows, D_MODEL), BF16),
        compiler_params=_cparams(("arbitrary",)),
        name=name,
    )(attn_t, mem_t, proj, proj, wa, wb)


def _post_kernel(mg_ref, x_ref, eg_ref, eb_ref, wo_ref, g1_ref, b1_ref, wr_hi_ref, wr_lo_ref, rb_ref, low_ref, cnt_in_ref,
                 x1_joint_ref, x1_ref, ri_ref, rw_ref, cnt_ref):
    del x1_joint_ref

    @pl.when(pl.program_id(0) == 0)
    def _():
        cnt_ref[...] = cnt_in_ref[...]

    x0 = _ln_rows(x_ref[...], eg_ref[...], eb_ref[...], LN_EPS)
    y = jnp.dot(mg_ref[...], wo_ref[...], preferred_element_type=F32)
    x1 = _ln_rows(ALPHA * x0 + y, g1_ref[...], b1_ref[...], LN_EPS)
    x1_ref[...] = x1

    hi = x1.astype(BF16)
    lo = (x1 - hi.astype(F32)).astype(BF16)
    logits = (jnp.dot(hi, wr_hi_ref[...], preferred_element_type=F32)
              + jnp.dot(lo, wr_hi_ref[...], preferred_element_type=F32)
              + jnp.dot(hi, wr_lo_ref[...], preferred_element_type=F32)) + rb_ref[...]
    lane = lax.broadcasted_iota(I32, logits.shape, 1)
    big = jnp.int32(LANES)

    def first_max(valid):
        vmax = jnp.max(jnp.where(valid, logits, -jnp.inf), axis=1, keepdims=True)
        idx = jnp.min(jnp.where(valid & (logits == vmax), lane, big), axis=1, keepdims=True)
        return vmax, idx

    is_grp = lane < N_GROUPS
    gmax, grp = first_max(is_grp)
    p_grp = 1.0 / jnp.sum(jnp.where(is_grp, jnp.exp(logits - gmax), 0.0), axis=1, keepdims=True)
    e_lo = N_GROUPS + grp * EXPERTS_PER_GROUP
    in_grp = (lane >= e_lo) & (lane < e_lo + EXPERTS_PER_GROUP)
    l1, i1 = first_max(in_grp)
    l2, i2 = first_max(in_grp & (lane != i1))
    t2 = jnp.exp(l2 - l1)
    w1 = p_grp / (1.0 + t2)
    w2 = p_grp * t2 / (1.0 + t2)
    e1 = i1 - N_GROUPS
    e2 = i2 - N_GROUPS

    oh1 = (lane == e1).astype(F32)
    oh2 = (lane == e2).astype(F32)
    cnt = oh1 + oh2
    before = jnp.dot(low_ref[...], cnt.astype(BF16), preferred_element_type=F32) + cnt_ref[...]
    r1 = jnp.sum(oh1 * before, axis=1, keepdims=True).astype(I32)
    r2 = jnp.sum(oh2 * before, axis=1, keepdims=True).astype(I32)
    cnt_ref[...] = cnt_ref[...] + jnp.sum(cnt, axis=0, keepdims=True)
    ri_ref[...] = jnp.where(lane == 0, e1, jnp.where(lane == 1, e2, jnp.where(lane == 2, r1, jnp.where(lane == 3, r2, 0))))
    rw_ref[...] = jnp.where(lane == 0, w1, jnp.where(lane == 1, w2, 0.0))


def _post_call(merged, x, eg, eb, wo, g1, b1, wr_hi, wr_lo, rb, low, cnt_in, x1_joint, *, tm, row_off, total_rows, name):
    rows = merged.shape[0]
    off = row_off // tm
    row_spec = pl.BlockSpec((tm, D_MODEL), lambda i: (i, 0))
    vec_spec = pl.BlockSpec((1, D_MODEL), lambda i: (0, 0))
    lane_spec = pl.BlockSpec((1, LANES), lambda i: (0, 0))
    rt_spec = pl.BlockSpec((D_MODEL, LANES), lambda i: (0, 0))
    assert row_off % tm == 0 and x1_joint.shape == (total_rows, D_MODEL)
    return pl.pallas_call(
        _post_kernel,
        grid=(rows // tm,),
        in_specs=[row_spec, row_spec, vec_spec, vec_spec,
                  pl.BlockSpec((D_MODEL, D_MODEL), lambda i: (0, 0), pipeline_mode=pl.Buffered(1)),
                  vec_spec, vec_spec, rt_spec, rt_spec, lane_spec,
                  pl.BlockSpec((tm, tm), lambda i: (0, 0)), lane_spec, pl.BlockSpec(memory_space=pl.ANY)],
        out_specs=[pl.BlockSpec((tm, D_MODEL), lambda i: (i + off, 0)),
                   pl.BlockSpec((tm, LANES), lambda i: (i, 0)), pl.BlockSpec((tm, LANES), lambda i: (i, 0)),
                   lane_spec],
        out_shape=[jax.ShapeDtypeStruct((total_rows, D_MODEL), F32), jax.ShapeDtypeStruct((rows, LANES), I32),
                   jax.ShapeDtypeStruct((rows, LANES), F32), jax.ShapeDtypeStruct((1, LANES), F32)],
        input_output_aliases={12: 0},
        compiler_params=_cparams(("arbitrary",)),
        name=name,
    )(merged, x, eg, eb, wo, g1, b1, wr_hi, wr_lo, rb, low, cnt_in, x1_joint)


def _row_copy(src, s, dst, d, sem):
    return pltpu.make_async_copy(src.at[pl.ds(s, 1), :], dst.at[pl.ds(d, 1), :], sem)


GATHER_SLOTS = 3
GATE_STAGE = GATHER_SLOTS


def _expert_kernel(blk_e_ref, n_act_ref, tok_ref, x_ref, wg_ref, wu_ref, wd_ref, y_ref, xbuf, sem):
    del blk_e_ref
    i = pl.program_id(0)
    n_act = n_act_ref[0]

    def start(blk, r):
        slot = blk % GATHER_SLOTS
        _row_copy(x_ref, tok_ref[blk * SLOT_BLOCK + r], xbuf.at[slot], r, sem.at[slot]).start()

    def wait(blk):
        slot = blk % GATHER_SLOTS
        pltpu.make_async_copy(x_ref.at[pl.ds(0, SLOT_BLOCK), :], xbuf.at[slot], sem.at[slot]).wait()

    @pl.when(i == 0)
    def _():
        lax.fori_loop(0, SLOT_BLOCK, lambda r, c: (start(0, r), start(1, r), c)[2], 0, unroll=4)

    @pl.when(i < n_act)
    def _():
        wait(i)
        xb = xbuf[i % GATHER_SLOTS].astype(BF16)
        half = D_EXPERT // 2
        xbuf[GATE_STAGE, :, 0:half] = jnp.dot(xb, wg_ref[0, :, 0:half], preferred_element_type=F32)
        for r in range(SLOT_BLOCK):
            start(i + 2, r)
        gate_hi = jnp.dot(xb, wg_ref[0, :, half:], preferred_element_type=F32)
        up = jnp.dot(xb, wu_ref[0], preferred_element_type=F32)
        gate = jnp.concatenate([xbuf[GATE_STAGE, :, 0:half], gate_hi], axis=1)
        hdn = (gate * jax.nn.sigmoid(gate) * up).astype(BF16)
        y_ref[...] = jnp.dot(hdn, wd_ref[0], preferred_element_type=F32)

    @pl.when(i >= n_act)
    def _():
        @pl.when(i == n_act)
        def _():
            wait(i)
            wait(i + 1)
        y_ref[...] = jnp.zeros_like(y_ref)


def _expert_call(blk_e, n_act, row_tok, x1, wg, wu, wd, *, name):
    n_blocks = blk_e.shape[0]
    grid_spec = pltpu.PrefetchScalarGridSpec(
        num_scalar_prefetch=3,
        grid=(n_blocks,),
        in_specs=[
            pl.BlockSpec(memory_space=pl.ANY),
            pl.BlockSpec((1, D_MODEL, D_EXPERT), lambda i, be, na, tok: (be[i], 0, 0)),
            pl.BlockSpec((1, D_MODEL, D_EXPERT), lambda i, be, na, tok: (be[i], 0, 0)),
            pl.BlockSpec((1, D_EXPERT, D_MODEL), lambda i, be, na, tok: (be[i], 0, 0)),
        ],
        out_specs=pl.BlockSpec((SLOT_BLOCK, D_MODEL), lambda i, be, na, tok: (i, 0)),
        scratch_shapes=[pltpu.VMEM((GATHER_SLOTS + 1, SLOT_BLOCK, D_MODEL), F32),
                        pltpu.SemaphoreType.DMA((GATHER_SLOTS,))],
    )
    return pl.pallas_call(
        _expert_kernel,
        grid_spec=grid_spec,
        out_shape=jax.ShapeDtypeStruct((n_blocks * SLOT_BLOCK, D_MODEL), F32),
        compiler_params=_cparams(("arbitrary",)),
        name=name,
    )(blk_e, n_act, row_tok, x1, wg, wu, wd)


COMBINE_STAGE = 4


def _combine_kernel(dest_ref, x1_ref, rw_ref, g2_ref, b2_ref, yr_ref, o_ref, ybuf, sem, *, tile):
    i = pl.program_id(0)
    last = pl.num_programs(0) - 1

    def start(t, par, r, k):
        _row_copy(yr_ref, dest_ref[2 * (t * tile + r) + k], ybuf.at[2 * par + k], r, sem.at[par]).start(priority=k)

    def wait(par):
        for k in range(2):
            pltpu.make_async_copy(yr_ref.at[pl.ds(0, tile), :], ybuf.at[2 * par + k], sem.at[par]).wait()

    @pl.when(i == 0)
    def _():
        lax.fori_loop(0, tile, lambda r, c: (start(i, 0, r, 0), start(i, 0, r, 1), c)[2], 0, unroll=4)

    par = i % 2
    wait(par)
    rw = rw_ref[...]
    s = ALPHA * x1_ref[...] + rw[:, 0:1] * ybuf[2 * par] + rw[:, 1:2] * ybuf[2 * par + 1]
    ybuf[COMBINE_STAGE, 0:8, :] = jnp.broadcast_to(b2_ref[...], (8, D_MODEL))
    nxt = jnp.minimum(i + 1, last)
    for r in range(tile):
        for k in range(2):
            start(nxt, 1 - par, r, k)
    o_ref[...] = _ln_rows(s, g2_ref[...], ybuf[COMBINE_STAGE, 0:1, :], LN_EPS)

    @pl.when(i == last)
    def _():
        wait(1 - par)


def _combine_call(dest, x1, rw, g2, b2, yr, *, row_off, tile, name):
    rows = rw.shape[0]
    off = row_off // tile
    vec_spec = pl.BlockSpec((1, D_MODEL), lambda i, d: (0, 0))
    grid_spec = pltpu.PrefetchScalarGridSpec(
        num_scalar_prefetch=1,
        grid=(rows // tile,),
        in_specs=[pl.BlockSpec((tile, D_MODEL), lambda i, d: (i + off, 0)),
                  pl.BlockSpec((tile, LANES), lambda i, d: (i, 0)), vec_spec, vec_spec,
                  pl.BlockSpec(memory_space=pl.ANY)],
        out_specs=pl.BlockSpec((tile, D_MODEL), lambda i, d: (i, 0)),
        scratch_shapes=[pltpu.VMEM((5, tile, D_MODEL), F32), pltpu.SemaphoreType.DMA((2,))],
    )
    return pl.pallas_call(
        functools.partial(_combine_kernel, tile=tile),
        grid_spec=grid_spec,
        out_shape=jax.ShapeDtypeStruct((rows, D_MODEL), F32),
        compiler_params=_cparams(("arbitrary",)),
        name=name,
    )(dest.reshape(-1), x1, rw, g2, b2, yr)


def _rope_tables(pos):
    half = HEAD_DIM // 2
    inv = ROPE_THETA ** (-jnp.arange(half, dtype=F32) / half)
    ang = pos.astype(F32)[:, None] * inv[None, :]
    cos, sin = jnp.cos(ang), jnp.sin(ang)
    return jnp.concatenate([cos, cos], axis=1), jnp.concatenate([-sin, sin], axis=1), cos.T, sin.T


def kernel(x_prompt, x_sample, meta_tokens, ln_emb_g, ln_emb_b, w_in, attn_sink, m_gate_bias, m_norm_g, w_br_attn, w_br_mlstm, w_out, ln1_g, ln1_b, w_router_group, b_router_group, w_router_expert, b_router_expert, w_expert_gate, w_expert_up, w_expert_down, ln2_g, ln2_b):
    assert w_in.shape[0] == DEPTH == 1
    row = lambda v: v.reshape(1, -1).astype(F32)
    eg, eb = row(ln_emb_g), row(ln_emb_b)
    g1, b1, g2, b2 = row(ln1_g[0]), row(ln1_b[0]), row(ln2_g[0]), row(ln2_b[0])

    w = w_in[0]
    sec = {}
    off = 0
    for nm, width in (("aq", 2048), ("ak", 512), ("av", 512), ("mq", 1024), ("mk", 1024), ("mv", 2048), ("mo", 2048),
                      ("mg", 32), ("bg", 4096)):
        sec[nm] = w[:, off:off + width]
        off += width
    w_tok = jnp.concatenate([sec[n] for n in ("bg", "mk", "ak", "av")], axis=1).astype(BF16)
    w_feat_t = jnp.concatenate([sec[n] for n in ("mv", "mo", "aq", "mq")], axis=1).T.astype(BF16)
    w_gate_t = jnp.pad(sec["mg"], ((0, 0), (0, LANES - 4 * M_HEADS))).T.astype(BF16)
    wa, wb, wo = w_br_attn[0].astype(BF16), w_br_mlstm[0].astype(BF16), w_out[0].astype(BF16)
    wr = jnp.pad(jnp.concatenate([w_router_group[0], w_router_expert[0]], axis=1),
                 ((0, 0), (0, LANES - N_GROUPS - N_EXPERTS)))
    wr_hi = wr.astype(BF16)
    wr_lo = (wr - wr_hi.astype(F32)).astype(BF16)
    rb = jnp.pad(jnp.concatenate([b_router_group[0], b_router_expert[0]]), (0, LANES - N_GROUPS - N_EXPERTS)).reshape(1, LANES)
    weg, weu, wed = w_expert_gate[0].astype(BF16), w_expert_up[0].astype(BF16), w_expert_down[0].astype(BF16)
    gate_bias = m_gate_bias[0].reshape(4 * M_HEADS, 1).astype(F32)
    norm_g_b = jnp.broadcast_to(m_norm_g[0].astype(F32)[:, None], (M_HEADS * M_V_DIM, BLOCK))
    sink = attn_sink[0].astype(F32)
    eye = jnp.eye(BLOCK, dtype=BF16)

    xm = jnp.pad(meta_tokens.astype(F32), ((FRONT_PAD, 0), (0, 0)))
    meta = _proj_call(xm, eg, eb, _rope_tables(jnp.arange(BLOCK) - FRONT_PAD), w_tok, w_feat_t, w_gate_t, tm=BLOCK,
                      zero_front=True, name="proj_meta")
    proj_m = meta[0]

    groups = []
    cnt = jnp.zeros((1, LANES), F32)
    total_rows = x_prompt.shape[0] * x_prompt.shape[1] + x_sample.shape[0] * x_sample.shape[1]
    x1 = jnp.zeros((total_rows, D_MODEL), F32)
    row_off = 0
    for gi, xg in enumerate((x_prompt, x_sample)):
        batch, seq, _ = xg.shape
        assert seq % BLOCK == 0
        nb = seq // BLOCK
        rows = batch * seq
        x2 = xg.reshape(rows, D_MODEL)
        tm = _pick_tile(seq, 1024)
        proj, proj_t, gates_t = _proj_call(x2, eg, eb, _rope_tables(jnp.arange(seq) + N_META), w_tok, w_feat_t, w_gate_t,
                                           tm=tm, zero_front=False, name=f"proj_g{gi}")
        attn_t = _attn_call(sink, proj, proj_t, proj_m, batch=batch, nb=nb, name=f"attn_g{gi}")
        h_fwd_t = _mlstm_call(proj, proj_t, gates_t, gate_bias, eye, meta, None,
                              batch=batch, nc=nb, rev=False, name=f"mlstm_fwd_g{gi}")
        mem_t = _mlstm_call(proj, proj_t, gates_t, gate_bias, eye, None, (h_fwd_t, norm_g_b),
                            batch=batch, nc=nb, rev=True, name=f"mlstm_bwd_g{gi}")
        tp = _pick_tile(seq, 512)
        merged = _merge_call(attn_t, mem_t, proj, wa, wb, tm=tp, name=f"merge_g{gi}")
        low = (jnp.arange(tp)[None, :] < jnp.arange(tp)[:, None]).astype(BF16)
        x1, ri, rw, cnt = _post_call(merged, x2, eg, eb, wo, g1, b1, wr_hi, wr_lo, rb, low, cnt, x1, tm=tp,
                                     row_off=row_off, total_rows=total_rows, name=f"post_g{gi}")
        groups.append((xg.shape, row_off, ri[:, :4], rw))
        row_off += rows

    counts = cnt[0, :N_EXPERTS].astype(I32)
    padded = (counts + SLOT_BLOCK - 1) // SLOT_BLOCK * SLOT_BLOCK
    pad_ends = jnp.cumsum(padded)
    pad_starts = pad_ends - padded
    n_blocks = -(-2 * total_rows // SLOT_BLOCK) + N_EXPERTS
    blk_start = jnp.arange(n_blocks, dtype=I32) * SLOT_BLOCK
    blk_e = jnp.minimum(jnp.sum((pad_ends[None, :] <= blk_start[:, None]).astype(I32), axis=1), N_EXPERTS - 1)
    n_act = (pad_ends[-1:] // SLOT_BLOCK).astype(I32)
    ri_all = jnp.concatenate([g[2] for g in groups], axis=0)
    dest = pad_starts[ri_all[:, 0:2]] + ri_all[:, 2:4]
    tok = jnp.broadcast_to(jnp.arange(total_rows, dtype=I32)[:, None], dest.shape)
    row_tok = jnp.zeros(((n_blocks + 1) * SLOT_BLOCK,), I32).at[dest.reshape(-1)].add(tok.reshape(-1))

    yr = _expert_call(blk_e, n_act, row_tok, x1, weg, weu, wed, name="experts")
    outs = []
    for gi, (shape, off, _, rw) in enumerate(groups):
        rows = rw.shape[0]
        out = _combine_call(dest[off:off + rows], x1, rw, g2, b2, yr, row_off=off, tile=_pick_tile(rows, 512),
                            name=f"combine_g{gi}")
        outs.append(out.reshape(shape))
    return tuple(outs)
```

```python
import functools

import jax
import jax.numpy as jnp
from jax import lax
from jax.experimental import pallas as pl
from jax.experimental.pallas import tpu as pltpu

F32 = jnp.float32
BF16 = jnp.bfloat16
I32 = jnp.int32

D_MODEL = 2048
DEPTH = 1
N_META = 16
BLOCK = 128
FRONT_PAD = BLOCK - N_META
N_Q_HEADS = 16
N_KV_HEADS = 4
HEAD_DIM = 128
Q_PER_KV = 4
WINDOW = 128
ROPE_THETA = 10000.0
M_HEADS = 8
M_QK_DIM = 128
M_V_DIM = 256
M_NORM_EPS = 1e-6
N_GROUPS = 4
EXPERTS_PER_GROUP = 8
N_EXPERTS = 32
D_EXPERT = 1024
ALPHA = (2 * DEPTH) ** 0.25
LN_EPS = 1e-5
NEG = -1e30

C_GA, C_GB, C_MK, C_AK, C_AV = 0, 2048, 4096, 5120, 5632
N_TOK = 6144
F_MV, F_MO, F_AQ, F_MQ = 0, 2048, 4096, 6144
N_FEAT = 7168
PROJ_TN = 1024
LANES = 128
SLOT_BLOCK = 256
VMEM_LIMIT = 56 * 1024 * 1024


def _pick_tile(n, pref):
    t = min(pref, n)
    t -= t % BLOCK
    while n % t:
        t -= BLOCK
    return t


def _cparams(sem, vmem=VMEM_LIMIT):
    return pltpu.CompilerParams(dimension_semantics=sem, vmem_limit_bytes=vmem)


def _ln_rows(x, g, b, eps):
    mu = jnp.mean(x, axis=-1, keepdims=True)
    xc = x - mu
    var = jnp.mean(xc * xc, axis=-1, keepdims=True)
    return xc * lax.rsqrt(var + eps) * g + b


def _rope_heads(acc, cos, sin, nheads, scale):
    outs = []
    for h in range(nheads):
        xh = acc[:, h * HEAD_DIM:(h + 1) * HEAD_DIM]
        o = xh * cos + pltpu.roll(xh, HEAD_DIM // 2, axis=1) * sin
        outs.append(o * scale)
    return outs


LOG2_E = 1.4426950408889634
ATTN_Q_SCALE = HEAD_DIM ** -0.5 * LOG2_E
PROJ_SPLIT = 1
N_TOK_TILES = N_TOK // PROJ_TN
N_FEAT_TILES = N_FEAT // PROJ_TN
_NT = (((1,), (1,)), ((), ()))


def _rope_heads_t(acc, cos_t, sin_t, scale):
    half = HEAD_DIM // 2
    outs = []
    for h in range(acc.shape[0] // HEAD_DIM):
        x1 = acc[h * HEAD_DIM:h * HEAD_DIM + half, :]
        x2 = acc[h * HEAD_DIM + half:(h + 1) * HEAD_DIM, :]
        outs += [(x1 * cos_t - x2 * sin_t) * scale, (x2 * cos_t + x1 * sin_t) * scale]
    return jnp.concatenate(outs, axis=0)


def _proj_kernel(x_ref, g_ref, b_ref, cos_ref, sin_ref, cost_ref, sint_ref, w_ref, wt_ref, wg_ref,
                 out_ref, outt_ref, gate_ref, u_scr, *, zero_front):
    j = pl.program_id(1)

    @pl.when(j == 0)
    def _():
        u = _ln_rows(x_ref[...], g_ref[...], b_ref[...], LN_EPS)
        if zero_front:
            row = lax.broadcasted_iota(I32, u.shape, 0)
            u = jnp.where(row >= FRONT_PAD, u, 0.0)
        ub = u.astype(BF16)
        u_scr[...] = ub
        gate_ref[...] = lax.dot_general(wg_ref[...], ub, _NT, preferred_element_type=F32)

    cw = PROJ_TN // PROJ_SPLIT
    is_tok = j < N_TOK_TILES
    is_k = j == C_AK // PROJ_TN
    jt = j - N_TOK_TILES
    is_aq = (jt >= F_AQ // PROJ_TN) & (jt < F_MQ // PROJ_TN)

    @pl.when(is_tok & jnp.logical_not(is_k))
    def _():
        for c in range(PROJ_SPLIT):
            cols = slice(c * cw, (c + 1) * cw)
            out_ref[:, cols] = jnp.dot(u_scr[...], w_ref[:, cols], preferred_element_type=F32).astype(BF16)

    @pl.when(is_k)
    def _():
        for c in range(PROJ_SPLIT):
            cols = slice(c * cw, (c + 1) * cw)
            acc = jnp.dot(u_scr[...], w_ref[:, cols], preferred_element_type=F32)
            n_rot = min(max(N_KV_HEADS * HEAD_DIM - c * cw, 0), cw) // HEAD_DIM
            if n_rot:
                heads = _rope_heads(acc, cos_ref[...], sin_ref[...], n_rot, 1.0)
                acc = jnp.concatenate(heads + ([acc[:, n_rot * HEAD_DIM:]] if n_rot * HEAD_DIM < cw else []), axis=1)
            out_ref[:, cols] = acc.astype(BF16)

    @pl.when(jnp.logical_not(is_tok) & is_aq)
    def _():
        for c in range(PROJ_SPLIT):
            rows = slice(c * cw, (c + 1) * cw)
            acc = lax.dot_general(wt_ref[rows, :], u_scr[...], _NT, preferred_element_type=F32)
            outt_ref[rows, :] = _rope_heads_t(acc, cost_ref[...], sint_ref[...], ATTN_Q_SCALE).astype(BF16)

    @pl.when(jnp.logical_not(is_tok) & jnp.logical_not(is_aq))
    def _():
        scale = jnp.where(jt == F_MQ // PROJ_TN, M_QK_DIM ** -0.5, 1.0)
        for c in range(PROJ_SPLIT):
            rows = slice(c * cw, (c + 1) * cw)
            acc = lax.dot_general(wt_ref[rows, :], u_scr[...], _NT, preferred_element_type=F32)
            outt_ref[rows, :] = (acc * scale).astype(BF16)


def _proj_call(x, g, b, rope, w_tok, w_feat_t, w_gate_t, *, tm, zero_front, name):
    rows = x.shape[0]
    cos, sin, cos_t, sin_t = rope
    seq = cos.shape[0]
    pos_tiles = seq // tm
    grid = (rows // tm, N_TOK_TILES + N_FEAT_TILES)
    tok_tile = lambda j: jnp.minimum(j, N_TOK_TILES - 1)
    feat_tile = lambda j: jnp.clip(j - N_TOK_TILES, 0, N_FEAT_TILES - 1)
    return pl.pallas_call(
        functools.partial(_proj_kernel, zero_front=zero_front),
        grid=grid,
        in_specs=[
            pl.BlockSpec((tm, D_MODEL), lambda i, j: (i, 0)),
            pl.BlockSpec((1, D_MODEL), lambda i, j: (0, 0)),
            pl.BlockSpec((1, D_MODEL), lambda i, j: (0, 0)),
            pl.BlockSpec((tm, HEAD_DIM), lambda i, j: (i % pos_tiles, 0)),
            pl.BlockSpec((tm, HEAD_DIM), lambda i, j: (i % pos_tiles, 0)),
            pl.BlockSpec((HEAD_DIM // 2, tm), lambda i, j: (0, i % pos_tiles)),
            pl.BlockSpec((HEAD_DIM // 2, tm), lambda i, j: (0, i % pos_tiles)),
            pl.BlockSpec((D_MODEL, PROJ_TN), lambda i, j: (0, tok_tile(j))),
            pl.BlockSpec((PROJ_TN, D_MODEL), lambda i, j: (feat_tile(j), 0)),
            pl.BlockSpec((LANES, D_MODEL), lambda i, j: (0, 0)),
        ],
        out_specs=[
            pl.BlockSpec((tm, PROJ_TN), lambda i, j: (i, tok_tile(j))),
            pl.BlockSpec((None, PROJ_TN, tm), lambda i, j: (i // pos_tiles, feat_tile(j), i % pos_tiles)),
            pl.BlockSpec((None, LANES, tm), lambda i, j: (i // pos_tiles, 0, i % pos_tiles)),
        ],
        out_shape=[jax.ShapeDtypeStruct((rows, N_TOK), BF16), jax.ShapeDtypeStruct((rows // seq, N_FEAT, seq), BF16),
                   jax.ShapeDtypeStruct((rows // seq, LANES, seq), F32)],
        scratch_shapes=[pltpu.VMEM((tm, D_MODEL), BF16)],
        compiler_params=_cparams(("arbitrary", "arbitrary")),
        name=name,
    )(x, g, b, cos, sin, cos_t, sin_t, w_tok, w_feat_t, w_gate_t)


ATTN_MAX_QBLOCKS = 4


def _attn_kernel(sink_ref, qt_ref, *refs, nb, qblocks):
    nband = qblocks + 2
    k_band, v_band = refs[:nband], refs[nband:2 * nband]
    km_ref, vm_ref, ot_ref, bias_scr, s_scr, p_scr = refs[2 * nband:]
    key = lax.broadcasted_iota(I32, (4 * BLOCK, BLOCK), 0)
    qpos = lax.broadcasted_iota(I32, (4 * BLOCK, BLOCK), 1)
    key_blocks = [slice(c * BLOCK, (c + 1) * BLOCK) for c in range(4)]

    def scores(c, cols):
        s = s_scr[key_blocks[c], cols]
        return s if c == 1 else s + bias_scr[key_blocks[c], :]

    for a in range(qblocks):
        blk = pl.program_id(1) * qblocks + a
        qcols = slice(a * BLOCK, (a + 1) * BLOCK)
        lo = jnp.where(blk > 0, 0, BLOCK)
        hi = jnp.where(blk < nb - 1, 3 * BLOCK, 2 * BLOCK)
        band = (jnp.abs(key - BLOCK - qpos) <= WINDOW) & (key >= lo) & (key < hi)
        bias_scr[...] = jnp.where(band | (key >= 3 * BLOCK + FRONT_PAD), 0.0, NEG)
        for kv in range(N_KV_HEADS):
            h0 = kv * Q_PER_KV
            ks = slice(kv * HEAD_DIM, (kv + 1) * HEAD_DIM)
            k = jnp.concatenate([r[:, ks] for r in (*k_band[a:a + 3], km_ref)], axis=0)
            v = jnp.concatenate([r[:, ks] for r in (*v_band[a:a + 3], vm_ref)], axis=0)
            qt = jnp.concatenate([qt_ref[(h0 + g) * HEAD_DIM:(h0 + g + 1) * HEAD_DIM, qcols] for g in range(Q_PER_KV)],
                                 axis=1)
            s_scr[...] = jnp.dot(k, qt, preferred_element_type=F32)
            inv = []
            for g in range(Q_PER_KV):
                cols = slice(g * BLOCK, (g + 1) * BLOCK)
                sink = sink_ref[h0 + g] * LOG2_E
                m = jnp.full((1, BLOCK), sink, F32)
                for c in range(4):
                    m = jnp.maximum(m, jnp.max(scores(c, cols), axis=0, keepdims=True))
                den = jnp.exp2(sink - m)
                for c in range(4):
                    p = jnp.exp2(scores(c, cols) - m)
                    den = den + jnp.sum(p, axis=0, keepdims=True)
                    p_scr[key_blocks[c], cols] = p.astype(BF16)
                inv.append(1.0 / den)
            ot = lax.dot_general(v, p_scr[...], (((0,), (0,)), ((), ())), preferred_element_type=F32)
            for g in range(Q_PER_KV):
                ot_ref[(h0 + g) * HEAD_DIM:(h0 + g + 1) * HEAD_DIM, qcols] = (
                    ot[:, g * BLOCK:(g + 1) * BLOCK] * inv[g]).astype(BF16)


def _attn_call(sink, proj, proj_t, proj_meta, *, batch, nb, name):
    rows = proj.shape[0]
    kw = N_KV_HEADS * HEAD_DIM
    qw = N_Q_HEADS * HEAD_DIM
    kcol, vcol = C_AK // kw, C_AV // kw
    qblocks = max(q for q in range(1, ATTN_MAX_QBLOCKS + 1) if nb % q == 0)
    qt = qblocks * BLOCK

    def band_spec(off, colblk):
        def imap(b, i, sink_ref):
            return (b * nb + jnp.clip(i * qblocks + off, 0, nb - 1), colblk)
        return pl.BlockSpec((BLOCK, kw), imap)

    offs = range(-1, qblocks + 1)
    grid_spec = pltpu.PrefetchScalarGridSpec(
        num_scalar_prefetch=1,
        grid=(batch, nb // qblocks),
        in_specs=[pl.BlockSpec((None, qw, qt), lambda b, i, s: (b, F_AQ // qw, i))]
        + [band_spec(o, kcol) for o in offs] + [band_spec(o, vcol) for o in offs]
        + [pl.BlockSpec((BLOCK, kw), lambda b, i, s: (0, kcol)), pl.BlockSpec((BLOCK, kw), lambda b, i, s: (0, vcol))],
        out_specs=pl.BlockSpec((None, qw, qt), lambda b, i, s: (b, 0, i)),
        scratch_shapes=[pltpu.VMEM((4 * BLOCK, BLOCK), F32), pltpu.VMEM((4 * BLOCK, Q_PER_KV * BLOCK), F32),
                        pltpu.VMEM((4 * BLOCK, Q_PER_KV * BLOCK), BF16)],
    )
    return pl.pallas_call(
        functools.partial(_attn_kernel, nb=nb, qblocks=qblocks),
        grid_spec=grid_spec,
        out_shape=jax.ShapeDtypeStruct((batch, qw, rows // batch), BF16),
        compiler_params=_cparams(("arbitrary", "arbitrary")),
        name=name,
    )(sink, proj_t, *([proj] * (2 * len(offs))), proj_meta, proj_meta)


MLSTM_HEAD_GROUP = 8


def _scan_lanes(x, op, fill, rev):
    lane = lax.broadcasted_iota(I32, x.shape, 1)
    sh = 1
    while sh < BLOCK:
        if rev:
            y = jnp.where(lane < BLOCK - sh, pltpu.roll(x, BLOCK - sh, axis=1), fill)
        else:
            y = jnp.where(lane >= sh, pltpu.roll(x, sh, axis=1), fill)
        x = op(x, y)
        sh *= 2
    return x


def _rows_to_columns(x, eye):
    hi = x.astype(BF16)
    r1 = x - hi.astype(F32)
    mid = r1.astype(BF16)
    lo = (r1 - mid.astype(F32)).astype(BF16)
    move = lambda a: lax.dot_general(eye, a, _NT, preferred_element_type=F32)
    return move(hi) + move(mid) + move(lo)


def _chunk_terms(gts, bias, m_prev, rev, pad_front):
    d = M_HEADS if rev else 0
    li = jnp.concatenate([gt[d:d + M_HEADS, :] + bias[d:d + M_HEADS, :] for gt in gts], axis=0)
    lf = jnp.concatenate([gt[2 * M_HEADS + d:3 * M_HEADS + d, :] + bias[2 * M_HEADS + d:3 * M_HEADS + d, :] for gt in gts],
                         axis=0)
    lf = jax.nn.log_sigmoid(lf)
    if pad_front:
        real = lax.broadcasted_iota(I32, li.shape, 1) >= FRONT_PAD
        li, lf = jnp.where(real, li, NEG), jnp.where(real, lf, 0.0)
    b = _scan_lanes(lf, jnp.add, 0.0, rev)
    tot = jnp.sum(lf, axis=1, keepdims=True)
    u = li - b
    big_m = jnp.maximum(m_prev, _scan_lanes(u, jnp.maximum, -jnp.inf, rev))
    inter = jnp.exp(m_prev - big_m)
    e_neg_m = jnp.exp(-(b + big_m))
    m_new = jnp.maximum(tot + m_prev, tot + jnp.max(u, axis=1, keepdims=True))
    w = jnp.exp(tot + u - m_new)
    sp = jnp.exp(tot + m_prev - m_new)
    return u, big_m, inter, e_neg_m, w, sp, m_new


def _state_update(h, k, vt, w, sp, ct_scr, n_scr):
    wh = w[h:h + 1, :]
    lhs = jnp.concatenate([(vt.astype(F32) * wh).astype(BF16), jnp.broadcast_to(wh, (16, BLOCK)).astype(BF16)], axis=0)
    upd = jnp.dot(lhs, k, preferred_element_type=F32)
    sph = sp[h:h + 1, :]
    ct_scr[h] = sph * ct_scr[h] + upd[:M_V_DIM]
    n_scr[h:h + 1, :] = sph * n_scr[h:h + 1, :] + upd[M_V_DIM:M_V_DIM + 1]


def _mlstm_kernel(*refs, rev, final):
    (qt_ref, k_ref, vt_ref, gt_ref, gtn_ref, bias_ref, eye_ref), refs = refs[:7], refs[7:]
    if not rev:
        (km_ref, vtm_ref, gtm_ref), refs = refs[:3], refs[3:]
    if final:
        (hft_ref, mot_ref, ngb_ref), refs = refs[:3], refs[3:]
    ht_ref, ct_scr, n_scr, t_scr, ucol_scr = refs
    bias = bias_ref[...]
    eye = eye_ref[...]
    n_seq = qt_ref.shape[0]
    n_heads = n_seq * M_HEADS

    def stash_terms(g_ref, m_prev):
        gts = [g_ref[b, 0:4 * M_HEADS, :] for b in range(n_seq)]
        u, big_m, inter, e_neg_m, w, sp, m_new = _chunk_terms(gts, bias, m_prev, rev, False)
        ucol_scr[...] = _rows_to_columns(u, eye)
        for slot, val in enumerate((big_m, inter, e_neg_m, w, sp, m_new)):
            t_scr[slot] = jnp.broadcast_to(val, (n_heads, LANES))

    @pl.when(pl.program_id(0) == 0)
    def _():
        ct_scr[...] = jnp.zeros_like(ct_scr)
        n_scr[...] = jnp.zeros_like(n_scr)
        m0 = jnp.zeros((n_heads, 1), F32)
        if not rev:
            _, _, _, _, w, sp, m0 = _chunk_terms([gtm_ref[0, 0:4 * M_HEADS, :]] * n_seq, bias, m0, rev, True)
            for hh in range(n_heads):
                h = hh % M_HEADS
                _state_update(hh, km_ref[:, h * M_QK_DIM:(h + 1) * M_QK_DIM],
                              vtm_ref[0, h * M_V_DIM:(h + 1) * M_V_DIM, :], w, sp, ct_scr, n_scr)
        stash_terms(gt_ref, m0)

    big_m, inter, e_neg_m, w = t_scr[0], t_scr[1], t_scr[2], t_scr[3]
    sp, m_new = t_scr[4][:, 0:1], t_scr[5][:, 0:1]
    u_col = ucol_scr[...]
    s_idx = lax.broadcasted_iota(I32, (BLOCK, BLOCK), 0)
    t_idx = lax.broadcasted_iota(I32, (BLOCK, BLOCK), 1)
    causal = (s_idx >= t_idx) if rev else (s_idx <= t_idx)
    qk_cols = lambda h: slice((h % M_HEADS) * M_QK_DIM, (h % M_HEADS + 1) * M_QK_DIM)
    v_rows = lambda h: slice((h % M_HEADS) * M_V_DIM, (h % M_HEADS + 1) * M_V_DIM)
    qt_of = lambda h: qt_ref[h // M_HEADS, qk_cols(h), :]
    k_of = lambda h: k_ref[h // M_HEADS, :, qk_cols(h)]
    vt_of = lambda h: vt_ref[h // M_HEADS, v_rows(h), :]
    for g0 in range(0, n_heads, MLSTM_HEAD_GROUP):
        hs = range(g0, g0 + MLSTM_HEAD_GROUP)
        st = {h: jnp.dot(jnp.concatenate([k_of(h), jnp.broadcast_to(n_scr[h:h + 1, :], (16, M_QK_DIM)).astype(BF16)], axis=0),
                         qt_of(h), preferred_element_type=F32) for h in hs}
        decay = {h: jnp.exp(jnp.where(causal, u_col[:, h:h + 1] - big_m[h:h + 1, :], NEG)) for h in hs}
        sd = {h: st[h][:BLOCK] * decay[h] for h in hs}
        rden = {}
        for h in hs:
            den = inter[h:h + 1, :] * st[h][BLOCK:BLOCK + 1] + jnp.sum(sd[h], axis=0, keepdims=True)
            rden[h] = 1.0 / jnp.maximum(jnp.abs(den), e_neg_m[h:h + 1, :])
        hv = {}
        for h in hs:
            lhs = jnp.concatenate([vt_of(h), ct_scr[h].astype(BF16)], axis=1)
            rhs = jnp.concatenate([sd[h].astype(BF16), (qt_of(h).astype(F32) * inter[h:h + 1, :]).astype(BF16)], axis=0)
            hv[h] = jnp.dot(lhs, rhs, preferred_element_type=F32) * rden[h]
        for h in hs:
            b, rows = h // M_HEADS, v_rows(h)
            x = hv[h]
            if final:
                x = x + hft_ref[b, rows, :].astype(F32)
                mu = jnp.mean(x, axis=0, keepdims=True)
                xc = x - mu
                var = jnp.mean(xc * xc, axis=0, keepdims=True)
                x = xc * lax.rsqrt(var + M_NORM_EPS) * ngb_ref[rows, :] * jax.nn.sigmoid(mot_ref[b, rows, :].astype(F32))
            ht_ref[b, rows, :] = x.astype(ht_ref.dtype)
        for h in hs:
            _state_update(h, k_of(h), vt_of(h), w, sp, ct_scr, n_scr)
    stash_terms(gtn_ref, m_new)


def _mlstm_call(proj, proj_t, gates_t, bias, eye, meta, final_in, *, batch, nc, rev, name):
    final = final_in is not None
    qw, vw = M_HEADS * M_QK_DIM, M_HEADS * M_V_DIM
    seq = nc * BLOCK
    proj3 = proj.reshape(batch, seq, N_TOK)
    chunk = (lambda i: nc - 1 - i) if rev else (lambda i: i)
    feat_spec = lambda width, off: pl.BlockSpec((batch, width, BLOCK), lambda i: (0, off // width, chunk(i)))

    in_specs = [
        feat_spec(qw, F_MQ),
        pl.BlockSpec((batch, BLOCK, qw), lambda i: (0, chunk(i), C_MK // qw)),
        feat_spec(vw, F_MV),
        pl.BlockSpec((batch, LANES, BLOCK), lambda i: (0, 0, chunk(i))),
        pl.BlockSpec((batch, LANES, BLOCK), lambda i: (0, 0, chunk(jnp.minimum(i + 1, nc - 1)))),
        pl.BlockSpec((4 * M_HEADS, 1), lambda i: (0, 0)),
        pl.BlockSpec((BLOCK, BLOCK), lambda i: (0, 0)),
    ]
    args = [proj_t, proj3, proj_t, gates_t, gates_t, bias, eye]
    if not rev:
        proj_m, proj_tm, gates_tm = meta
        in_specs += [
            pl.BlockSpec((BLOCK, qw), lambda i: (0, C_MK // qw)),
            pl.BlockSpec((1, vw, BLOCK), lambda i: (0, F_MV // vw, 0)),
            pl.BlockSpec((1, LANES, BLOCK), lambda i: (0, 0, 0)),
        ]
        args += [proj_m, proj_tm, gates_tm]
    if final:
        h_fwd_t, norm_g_b = final_in
        in_specs += [feat_spec(vw, 0), feat_spec(vw, F_MO), pl.BlockSpec((vw, BLOCK), lambda i: (0, 0))]
        args += [h_fwd_t, proj_t, norm_g_b]
    n_heads = batch * M_HEADS
    return pl.pallas_call(
        functools.partial(_mlstm_kernel, rev=rev, final=final),
        grid=(nc,),
        in_specs=in_specs,
        out_specs=feat_spec(vw, 0),
        out_shape=jax.ShapeDtypeStruct((batch, vw, seq), BF16),
        scratch_shapes=[pltpu.VMEM((n_heads, M_V_DIM, M_QK_DIM), F32), pltpu.VMEM((n_heads, LANES), F32),
                        pltpu.VMEM((6, n_heads, LANES), F32), pltpu.VMEM((BLOCK, n_heads), F32)],
        compiler_params=_cparams(("arbitrary",)),
        name=name,
    )(*args)


def _merge_kernel(at_ref, mt_ref, ga_ref, gb_ref, wa_ref, wb_ref, o_ref):
    tn = (((0,), (0,)), ((), ()))
    ya = lax.dot_general(at_ref[...], wa_ref[...], tn, preferred_element_type=F32)
    yb = lax.dot_general(mt_ref[...], wb_ref[...], tn, preferred_element_type=F32)
    o_ref[...] = (jax.nn.sigmoid(ga_ref[...].astype(F32)) * ya + jax.nn.sigmoid(gb_ref[...].astype(F32)) * yb).astype(BF16)


def _merge_call(attn_t, mem_t, proj, wa, wb, *, tm, name):
    rows = proj.shape[0]
    row_spec = pl.BlockSpec((tm, D_MODEL), lambda i: (i, 0))
    tiles_per_seq = attn_t.shape[2] // tm
    feat_spec = pl.BlockSpec((None, D_MODEL, tm), lambda i: (i // tiles_per_seq, 0, i % tiles_per_seq))
    w_spec = pl.BlockSpec((D_MODEL, D_MODEL), lambda i: (0, 0), pipeline_mode=pl.Buffered(1))
    return pl.pallas_call(
        _merge_kernel,
        grid=(rows // tm,),
        in_specs=[feat_spec, feat_spec,
                  pl.BlockSpec((tm, D_MODEL), lambda i: (i, C_GA // D_MODEL)),
                  pl.BlockSpec((tm, D_MODEL), lambda i: (i, C_GB // D_MODEL)),
                  w_spec, w_spec],
        out_specs=row_spec,
        out_shape=jax.ShapeDtypeStruct((rows, D_MODEL), BF16),
        compiler_params=_cparams(("arbitrary",)),
        name=name,
    )(attn_t, mem_t, proj, proj, wa, wb)


def _post_kernel(mg_ref, x_ref, eg_ref, eb_ref, wo_ref, g1_ref, b1_ref, wr_hi_ref, wr_lo_ref, rb_ref, low_ref, cnt_in_ref,
                 x1_joint_ref, x1_ref, ri_ref, rw_ref, cnt_ref):
    del x1_joint_ref

    @pl.when(pl.program_id(0) == 0)
    def _():
        cnt_ref[...] = cnt_in_ref[...]

    x0 = _ln_rows(x_ref[...], eg_ref[...], eb_ref[...], LN_EPS)
    y = jnp.dot(mg_ref[...], wo_ref[...], preferred_element_type=F32)
    x1 = _ln_rows(ALPHA * x0 + y, g1_ref[...], b1_ref[...], LN_EPS)
    x1_ref[...] = x1

    hi = x1.astype(BF16)
    lo = (x1 - hi.astype(F32)).astype(BF16)
    logits = (jnp.dot(hi, wr_hi_ref[...], preferred_element_type=F32)
              + jnp.dot(lo, wr_hi_ref[...], preferred_element_type=F32)
              + jnp.dot(hi, wr_lo_ref[...], preferred_element_type=F32)) + rb_ref[...]
    lane = lax.broadcasted_iota(I32, logits.shape, 1)
    big = jnp.int32(LANES)

    def first_max(valid):
        vmax = jnp.max(jnp.where(valid, logits, -jnp.inf), axis=1, keepdims=True)
        idx = jnp.min(jnp.where(valid & (logits == vmax), lane, big), axis=1, keepdims=True)
        return vmax, idx

    is_grp = lane < N_GROUPS
    gmax, grp = first_max(is_grp)
    p_grp = 1.0 / jnp.sum(jnp.where(is_grp, jnp.exp(logits - gmax), 0.0), axis=1, keepdims=True)
    e_lo = N_GROUPS + grp * EXPERTS_PER_GROUP
    in_grp = (lane >= e_lo) & (lane < e_lo + EXPERTS_PER_GROUP)
    l1, i1 = first_max(in_grp)
    l2, i2 = first_max(in_grp & (lane != i1))
    t2 = jnp.exp(l2 - l1)
    w1 = p_grp / (1.0 + t2)
    w2 = p_grp * t2 / (1.0 + t2)
    e1 = i1 - N_GROUPS
    e2 = i2 - N_GROUPS

    oh1 = (lane == e1).astype(F32)
    oh2 = (lane == e2).astype(F32)
    cnt = oh1 + oh2
    before = jnp.dot(low_ref[...], cnt.astype(BF16), preferred_element_type=F32) + cnt_ref[...]
    r1 = jnp.sum(oh1 * before, axis=1, keepdims=True).astype(I32)
    r2 = jnp.sum(oh2 * before, axis=1, keepdims=True).astype(I32)
    cnt_ref[...] = cnt_ref[...] + jnp.sum(cnt, axis=0, keepdims=True)
    ri_ref[...] = jnp.where(lane == 0, e1, jnp.where(lane == 1, e2, jnp.where(lane == 2, r1, jnp.where(lane == 3, r2, 0))))
    rw_ref[...] = jnp.where(lane == 0, w1, jnp.where(lane == 1, w2, 0.0))


def _post_call(merged, x, eg, eb, wo, g1, b1, wr_hi, wr_lo, rb, low, cnt_in, x1_joint, *, tm, row_off, total_rows, name):
    rows = merged.shape[0]
    off = row_off // tm
    row_spec = pl.BlockSpec((tm, D_MODEL), lambda i: (i, 0))
    vec_spec = pl.BlockSpec((1, D_MODEL), lambda i: (0, 0))
    lane_spec = pl.BlockSpec((1, LANES), lambda i: (0, 0))
    rt_spec = pl.BlockSpec((D_MODEL, LANES), lambda i: (0, 0))
    assert row_off % tm == 0 and x1_joint.shape == (total_rows, D_MODEL)
    return pl.pallas_call(
        _post_kernel,
        grid=(rows // tm,),
        in_specs=[row_spec, row_spec, vec_spec, vec_spec,
                  pl.BlockSpec((D_MODEL, D_MODEL), lambda i: (0, 0), pipeline_mode=pl.Buffered(1)),
                  vec_spec, vec_spec, rt_spec, rt_spec, lane_spec,
                  pl.BlockSpec((tm, tm), lambda i: (0, 0)), lane_spec, pl.BlockSpec(memory_space=pl.ANY)],
        out_specs=[pl.BlockSpec((tm, D_MODEL), lambda i: (i + off, 0)),
                   pl.BlockSpec((tm, LANES), lambda i: (i, 0)), pl.BlockSpec((tm, LANES), lambda i: (i, 0)),
                   lane_spec],
        out_shape=[jax.ShapeDtypeStruct((total_rows, D_MODEL), F32), jax.ShapeDtypeStruct((rows, LANES), I32),
                   jax.ShapeDtypeStruct((rows, LANES), F32), jax.ShapeDtypeStruct((1, LANES), F32)],
        input_output_aliases={12: 0},
        compiler_params=_cparams(("arbitrary",)),
        name=name,
    )(merged, x, eg, eb, wo, g1, b1, wr_hi, wr_lo, rb, low, cnt_in, x1_joint)


def _row_copy(src, s, dst, d, sem):
    return pltpu.make_async_copy(src.at[pl.ds(s, 1), :], dst.at[pl.ds(d, 1), :], sem)


GATHER_SLOTS = 3
GATE_STAGE = GATHER_SLOTS


def _expert_kernel(blk_e_ref, n_act_ref, tok_ref, x_ref, wg_ref, wu_ref, wd_ref, y_ref, xbuf, sem):
    del blk_e_ref
    i = pl.program_id(0)
    n_act = n_act_ref[0]

    def start(blk, r, priority=0):
        slot = blk % GATHER_SLOTS
        _row_copy(x_ref, tok_ref[blk * SLOT_BLOCK + r], xbuf.at[slot], r, sem.at[slot]).start(priority=priority)

    def wait(blk):
        slot = blk % GATHER_SLOTS
        pltpu.make_async_copy(x_ref.at[pl.ds(0, SLOT_BLOCK), :], xbuf.at[slot], sem.at[slot]).wait()

    @pl.when(i == 0)
    def _():
        lax.fori_loop(0, SLOT_BLOCK, lambda r, c: (start(0, r), start(1, r), c)[2], 0, unroll=4)

    @pl.when(i < n_act)
    def _():
        wait(i)
        xb = xbuf[i % GATHER_SLOTS].astype(BF16)
        half = D_EXPERT // 2
        xbuf[GATE_STAGE, :, 0:half] = jnp.dot(xb, wg_ref[0, :, 0:half], preferred_element_type=F32)
        for r in range(SLOT_BLOCK):
            start(i + 2, r, priority=r % 2)
        gate_hi = jnp.dot(xb, wg_ref[0, :, half:], preferred_element_type=F32)
        up = jnp.dot(xb, wu_ref[0], preferred_element_type=F32)
        gate = jnp.concatenate([xbuf[GATE_STAGE, :, 0:half], gate_hi], axis=1)
        hdn = (gate * jax.nn.sigmoid(gate) * up).astype(BF16)
        y_ref[...] = jnp.dot(hdn, wd_ref[0], preferred_element_type=F32)

    @pl.when(i >= n_act)
    def _():
        @pl.when(i == n_act)
        def _():
            wait(i)
            wait(i + 1)
        y_ref[...] = jnp.zeros_like(y_ref)


def _expert_call(blk_e, n_act, row_tok, x1, wg, wu, wd, *, name):
    n_blocks = blk_e.shape[0]
    grid_spec = pltpu.PrefetchScalarGridSpec(
        num_scalar_prefetch=3,
        grid=(n_blocks,),
        in_specs=[
            pl.BlockSpec(memory_space=pl.ANY),
            pl.BlockSpec((1, D_MODEL, D_EXPERT), lambda i, be, na, tok: (be[i], 0, 0)),
            pl.BlockSpec((1, D_MODEL, D_EXPERT), lambda i, be, na, tok: (be[i], 0, 0)),
            pl.BlockSpec((1, D_EXPERT, D_MODEL), lambda i, be, na, tok: (be[i], 0, 0)),
        ],
        out_specs=pl.BlockSpec((SLOT_BLOCK, D_MODEL), lambda i, be, na, tok: (i, 0)),
        scratch_shapes=[pltpu.VMEM((GATHER_SLOTS + 1, SLOT_BLOCK, D_MODEL), F32),
                        pltpu.SemaphoreType.DMA((GATHER_SLOTS,))],
    )
    return pl.pallas_call(
        _expert_kernel,
        grid_spec=grid_spec,
        out_shape=jax.ShapeDtypeStruct((n_blocks * SLOT_BLOCK, D_MODEL), F32),
        compiler_params=_cparams(("arbitrary",)),
        name=name,
    )(blk_e, n_act, row_tok, x1, wg, wu, wd)


COMBINE_STAGE = 4


def _combine_kernel(dest_ref, x1_ref, rw_ref, g2_ref, b2_ref, yr_ref, o_ref, ybuf, sem, *, tile):
    i = pl.program_id(0)
    last = pl.num_programs(0) - 1

    def start(t, par, r, k):
        _row_copy(yr_ref, dest_ref[2 * (t * tile + r) + k], ybuf.at[2 * par + k], r, sem.at[par]).start(priority=k)

    def wait(par):
        for k in range(2):
            pltpu.make_async_copy(yr_ref.at[pl.ds(0, tile), :], ybuf.at[2 * par + k], sem.at[par]).wait()

    @pl.when(i == 0)
    def _():
        lax.fori_loop(0, tile, lambda r, c: (start(i, 0, r, 0), start(i, 0, r, 1), c)[2], 0, unroll=4)

    par = i % 2
    wait(par)
    rw = rw_ref[...]
    s = ALPHA * x1_ref[...] + rw[:, 0:1] * ybuf[2 * par] + rw[:, 1:2] * ybuf[2 * par + 1]
    ybuf[COMBINE_STAGE, 0:8, :] = jnp.broadcast_to(b2_ref[...], (8, D_MODEL))
    nxt = jnp.minimum(i + 1, last)
    for r in range(tile):
        for k in range(2):
            start(nxt, 1 - par, r, k)
    o_ref[...] = _ln_rows(s, g2_ref[...], ybuf[COMBINE_STAGE, 0:1, :], LN_EPS)

    @pl.when(i == last)
    def _():
        wait(1 - par)


def _combine_call(dest, x1, rw, g2, b2, yr, *, row_off, tile, name):
    rows = rw.shape[0]
    off = row_off // tile
    vec_spec = pl.BlockSpec((1, D_MODEL), lambda i, d: (0, 0))
    grid_spec = pltpu.PrefetchScalarGridSpec(
        num_scalar_prefetch=1,
        grid=(rows // tile,),
        in_specs=[pl.BlockSpec((tile, D_MODEL), lambda i, d: (i + off, 0)),
                  pl.BlockSpec((tile, LANES), lambda i, d: (i, 0)), vec_spec, vec_spec,
                  pl.BlockSpec(memory_space=pl.ANY)],
        out_specs=pl.BlockSpec((tile, D_MODEL), lambda i, d: (i, 0)),
        scratch_shapes=[pltpu.VMEM((5, tile, D_MODEL), F32), pltpu.SemaphoreType.DMA((2,))],
    )
    return pl.pallas_call(
        functools.partial(_combine_kernel, tile=tile),
        grid_spec=grid_spec,
        out_shape=jax.ShapeDtypeStruct((rows, D_MODEL), F32),
        compiler_params=_cparams(("arbitrary",)),
        name=name,
    )(dest.reshape(-1), x1, rw, g2, b2, yr)


def _rope_tables(pos):
    half = HEAD_DIM // 2
    inv = ROPE_THETA ** (-jnp.arange(half, dtype=F32) / half)
    ang = pos.astype(F32)[:, None] * inv[None, :]
    cos, sin = jnp.cos(ang), jnp.sin(ang)
    return jnp.concatenate([cos, cos], axis=1), jnp.concatenate([-sin, sin], axis=1), cos.T, sin.T


def kernel(x_prompt, x_sample, meta_tokens, ln_emb_g, ln_emb_b, w_in, attn_sink, m_gate_bias, m_norm_g, w_br_attn, w_br_mlstm, w_out, ln1_g, ln1_b, w_router_group, b_router_group, w_router_expert, b_router_expert, w_expert_gate, w_expert_up, w_expert_down, ln2_g, ln2_b):
    assert w_in.shape[0] == DEPTH == 1
    row = lambda v: v.reshape(1, -1).astype(F32)
    eg, eb = row(ln_emb_g), row(ln_emb_b)
    g1, b1, g2, b2 = row(ln1_g[0]), row(ln1_b[0]), row(ln2_g[0]), row(ln2_b[0])

    w = w_in[0]
    sec = {}
    off = 0
    for nm, width in (("aq", 2048), ("ak", 512), ("av", 512), ("mq", 1024), ("mk", 1024), ("mv", 2048), ("mo", 2048),
                      ("mg", 32), ("bg", 4096)):
        sec[nm] = w[:, off:off + width]
        off += width
    w_tok = jnp.concatenate([sec[n] for n in ("bg", "mk", "ak", "av")], axis=1).astype(BF16)
    w_feat_t = jnp.concatenate([sec[n] for n in ("mv", "mo", "aq", "mq")], axis=1).T.astype(BF16)
    w_gate_t = jnp.pad(sec["mg"], ((0, 0), (0, LANES - 4 * M_HEADS))).T.astype(BF16)
    wa, wb, wo = w_br_attn[0].astype(BF16), w_br_mlstm[0].astype(BF16), w_out[0].astype(BF16)
    wr = jnp.pad(jnp.concatenate([w_router_group[0], w_router_expert[0]], axis=1),
                 ((0, 0), (0, LANES - N_GROUPS - N_EXPERTS)))
    wr_hi = wr.astype(BF16)
    wr_lo = (wr - wr_hi.astype(F32)).astype(BF16)
    rb = jnp.pad(jnp.concatenate([b_router_group[0], b_router_expert[0]]), (0, LANES - N_GROUPS - N_EXPERTS)).reshape(1, LANES)
    weg, weu, wed = w_expert_gate[0].astype(BF16), w_expert_up[0].astype(BF16), w_expert_down[0].astype(BF16)
    gate_bias = m_gate_bias[0].reshape(4 * M_HEADS, 1).astype(F32)
    norm_g_b = jnp.broadcast_to(m_norm_g[0].astype(F32)[:, None], (M_HEADS * M_V_DIM, BLOCK))
    sink = attn_sink[0].astype(F32)
    eye = jnp.eye(BLOCK, dtype=BF16)

    xm = jnp.pad(meta_tokens.astype(F32), ((FRONT_PAD, 0), (0, 0)))
    meta = _proj_call(xm, eg, eb, _rope_tables(jnp.arange(BLOCK) - FRONT_PAD), w_tok, w_feat_t, w_gate_t, tm=BLOCK,
                      zero_front=True, name="proj_meta")
    proj_m = meta[0]

    groups = []
    cnt = jnp.zeros((1, LANES), F32)
    total_rows = x_prompt.shape[0] * x_prompt.shape[1] + x_sample.shape[0] * x_sample.shape[1]
    x1 = jnp.zeros((total_rows, D_MODEL), F32)
    row_off = 0
    for gi, xg in enumerate((x_prompt, x_sample)):
        batch, seq, _ = xg.shape
        assert seq % BLOCK == 0
        nb = seq // BLOCK
        rows = batch * seq
        x2 = xg.reshape(rows, D_MODEL)
        tm = _pick_tile(seq, 1024)
        proj, proj_t, gates_t = _proj_call(x2, eg, eb, _rope_tables(jnp.arange(seq) + N_META), w_tok, w_feat_t, w_gate_t,
                                           tm=tm, zero_front=False, name=f"proj_g{gi}")
        attn_t = _attn_call(sink, proj, proj_t, proj_m, batch=batch, nb=nb, name=f"attn_g{gi}")
        h_fwd_t = _mlstm_call(proj, proj_t, gates_t, gate_bias, eye, meta, None,
                              batch=batch, nc=nb, rev=False, name=f"mlstm_fwd_g{gi}")
        mem_t = _mlstm_call(proj, proj_t, gates_t, gate_bias, eye, None, (h_fwd_t, norm_g_b),
                            batch=batch, nc=nb, rev=True, name=f"mlstm_bwd_g{gi}")
        tp = _pick_tile(seq, 512)
        merged = _merge_call(attn_t, mem_t, proj, wa, wb, tm=tp, name=f"merge_g{gi}")
        low = (jnp.arange(tp)[None, :] < jnp.arange(tp)[:, None]).astype(BF16)
        x1, ri, rw, cnt = _post_call(merged, x2, eg, eb, wo, g1, b1, wr_hi, wr_lo, rb, low, cnt, x1, tm=tp,
                                     row_off=row_off, total_rows=total_rows, name=f"post_g{gi}")
        groups.append((xg.shape, row_off, ri[:, :4], rw))
        row_off += rows

    counts = cnt[0, :N_EXPERTS].astype(I32)
    padded = (counts + SLOT_BLOCK - 1) // SLOT_BLOCK * SLOT_BLOCK
    pad_ends = jnp.cumsum(padded)
    pad_starts = pad_ends - padded
    n_blocks = -(-2 * total_rows // SLOT_BLOCK) + N_EXPERTS
    blk_start = jnp.arange(n_blocks, dtype=I32) * SLOT_BLOCK
    blk_e = jnp.minimum(jnp.sum((pad_ends[None, :] <= blk_start[:, None]).astype(I32), axis=1), N_EXPERTS - 1)
    n_act = (pad_ends[-1:] // SLOT_BLOCK).astype(I32)
    ri_all = jnp.concatenate([g[2] for g in groups], axis=0)
    dest = pad_starts[ri_all[:, 0:2]] + ri_all[:, 2:4]
    tok = jnp.broadcast_to(jnp.arange(total_rows, dtype=I32)[:, None], dest.shape)
    row_tok = jnp.zeros(((n_blocks + 1) * SLOT_BLOCK,), I32).at[dest.reshape(-1)].add(tok.reshape(-1))

    yr = _expert_call(blk_e, n_act, row_tok, x1, weg, weu, wed, name="experts")
    outs = []
    for gi, (shape, off, _, rw) in enumerate(groups):
        rows = rw.shape[0]
        out = _combine_call(dest[off:off + rows], x1, rw, g2, b2, yr, row_off=off, tile=_pick_tile(rows, 512),
                            name=f"combine_g{gi}")
        outs.append(out.reshape(shape))
    return tuple(outs)
```
